```python
import math, functools
import jax, jax.numpy as jnp
from jax import lax
import numpy as np

D_MODEL = 4096
BATCH = 2
SEQ = 8192
DEPTH = 2
DEC_BATCH = 8
DEC_SEQ = 64
PAST_LEN = 2048

CHUNK = 64
Q_BLOCK = 128
HEAD_DIM = 128
SB_HEADS = 16
DSA_HEADS = 16
DSA_KV_HEADS = 4
DSA_GROUP = DSA_HEADS // DSA_KV_HEADS
IDX_HEADS = 16
IDX_DIM = 64
DSA_TOPK_MAX = 256
FOX_HEADS = 16
N_BRANCHES = 3
BRANCH_WIDTH = 2048
N_BUCKETS = 32
MAX_DISTANCE = 128
N_GROUPS = 4
EXPERTS_PER_GROUP = 8
N_EXPERTS = N_GROUPS * EXPERTS_PER_GROUP
TOP_K = 2
EXPERT_FF = 1024
MOE_BLOCK = 256
EPS = 1e-6
ATTN_SCALE = HEAD_DIM ** -0.5
IDX_SCALE = IDX_DIM ** -0.5
IDX_W_SCALE = IDX_HEADS ** -0.5
IN_SPLIT = (SB_HEADS * HEAD_DIM, SB_HEADS * HEAD_DIM, SB_HEADS * HEAD_DIM,
            DSA_HEADS * HEAD_DIM, DSA_KV_HEADS * HEAD_DIM, DSA_KV_HEADS * HEAD_DIM,
            IDX_HEADS * IDX_DIM, IDX_DIM, IDX_HEADS,
            FOX_HEADS * HEAD_DIM, FOX_HEADS * HEAD_DIM, FOX_HEADS * HEAD_DIM, FOX_HEADS,
            N_BRANCHES * D_MODEL)
IN_WIDTH = sum(IN_SPLIT)

kernel_name = 'chunk_causal_hybrid_sb_dsa_fox_hmoe_step'


def rms_norm(x, g):
    xf = x.astype(jnp.float32)
    y = xf * lax.rsqrt(jnp.mean(xf * xf, axis=-1, keepdims=True) + EPS)
    return y.astype(x.dtype) * g


def t5_bucket(rel):
    half = N_BUCKETS // 2
    exact = half // 2
    side = jnp.where(rel < 0, half, 0)
    a = jnp.abs(rel)
    far = exact + (jnp.log(jnp.maximum(a, 1).astype(jnp.float32) / exact)
                   / math.log(MAX_DISTANCE / exact) * (half - exact)).astype(jnp.int32)
    far = jnp.minimum(far, half - 1)
    return side + jnp.where(a < exact, a, far)


def sweep(block_fn, q_parts, qpos, kv_parts):
    s = qpos.shape[0]
    if s % Q_BLOCK:
        return block_fn(q_parts, qpos, kv_parts)
    nb = s // Q_BLOCK

    def split(a):
        return jnp.moveaxis(a.reshape(a.shape[0], nb, Q_BLOCK, *a.shape[2:]), 1, 0)

    out = lax.map(lambda blk: block_fn(blk[0], blk[1], kv_parts),
                  (tuple(split(a) for a in q_parts), qpos.reshape(nb, Q_BLOCK)))
    out = jnp.moveaxis(out, 0, 1)
    return out.reshape(out.shape[0], s, *out.shape[3:])


def sb_block(q_parts, qpos, kv_parts):
    (q,) = q_parts
    k, v, kpos = kv_parts
    z = jnp.einsum('bqhd,bkhd->bhqk', q, k).astype(jnp.float32) * ATTN_SCALE
    mask = kpos[None, :] < qpos[:, None]
    log_beta = jax.nn.log_sigmoid(z)
    log_keep = jnp.where(mask, log_beta - z, 0.0)
    between = lax.cumsum(log_keep, axis=3, reverse=True) - log_keep
    w = jnp.where(mask, jnp.exp(log_beta + between), 0.0)
    return jnp.einsum('bhqk,bkhd->bqhd', w.astype(v.dtype), v)


def fox_block(q_parts, qpos, kv_parts):
    q, dq = q_parts
    k, v, dk, kpos = kv_parts
    z = jnp.einsum('bqhd,bkhd->bhqk', q, k).astype(jnp.float32) * ATTN_SCALE
    z = z + jnp.transpose(dq, (0, 2, 1))[..., None] - jnp.transpose(dk, (0, 2, 1))[:, :, None, :]
    z = jnp.where(kpos[None, :] <= qpos[:, None], z, -jnp.inf)
    p = jax.nn.softmax(z, axis=-1)
    return jnp.einsum('bhqk,bkhd->bqhd', p.astype(v.dtype), v)


def dsa_block(q_parts, qpos, kv_parts, topk):
    q, qi, wi = q_parts
    k, v, ki, kpos, rel_bias = kv_parts
    b, qb = q.shape[:2]
    s = jnp.einsum('bqjd,bkd->bqjk', qi, ki).astype(jnp.float32) * IDX_SCALE
    score = jnp.einsum('bqjk,bqj->bqk', jax.nn.relu(s), wi.astype(jnp.float32) * IDX_W_SCALE)
    admissible = (kpos[None, :] // CHUNK) <= (qpos[:, None] // CHUNK)
    score = jnp.where(admissible, score, -jnp.inf)
    top_val, top_idx = lax.top_k(score, topk)
    valid = jnp.isfinite(top_val)
    take = jax.vmap(lambda a, i: a[i])
    kg = take(k, top_idx)
    vg = take(v, top_idx)
    qg = q.reshape(b, qb, DSA_KV_HEADS, DSA_GROUP, HEAD_DIM)
    z = jnp.einsum('bqhgd,bqkhd->bqhgk', qg, kg).astype(jnp.float32) * ATTN_SCALE
    bias = rel_bias[t5_bucket(qpos[None, :, None] - kpos[top_idx])].astype(jnp.float32)
    z = z + jnp.moveaxis(bias.reshape(b, qb, topk, DSA_KV_HEADS, DSA_GROUP), 2, 4)
    z = jnp.where(valid[:, :, None, None, :], z, -jnp.inf)
    p = jax.nn.softmax(z, axis=-1)
    o = jnp.einsum('bqhgk,bqkhd->bqhgd', p.astype(vg.dtype), vg)
    return o.reshape(b, qb, DSA_HEADS, HEAD_DIM)


def token_mixers(h, past, w_in, q_norm_dsa, k_norm_dsa, q_norm_fox, k_norm_fox,
                 b_forget, w_branch, w_out, rel_bias):
    b, s, _ = h.shape
    cols = np.cumsum(IN_SPLIT)[:-1].tolist()
    (sb_q, sb_k, sb_v, ds_q, ds_k, ds_v, ix_q, ix_k, ix_w,
     fx_q, fx_k, fx_v, fx_f, gate) = jnp.split(h @ w_in, cols, axis=-1)
    hd = lambda a: a.reshape(b, s, -1, HEAD_DIM)
    sb_q, sb_k, sb_v, ds_v, fx_v = hd(sb_q), hd(sb_k), hd(sb_v), hd(ds_v), hd(fx_v)
    ds_q = rms_norm(hd(ds_q), q_norm_dsa)
    ds_k = rms_norm(hd(ds_k), k_norm_dsa)
    fx_q = rms_norm(hd(fx_q), q_norm_fox)
    fx_k = rms_norm(hd(fx_k), k_norm_fox)
    ix_q = ix_q.reshape(b, s, IDX_HEADS, IDX_DIM)
    log_f = jax.nn.log_sigmoid(fx_f.astype(jnp.float32) + b_forget.astype(jnp.float32))
    new_rows = (sb_k, sb_v, ds_k, ds_v, ix_k, fx_k, fx_v, log_f)
    if past is None:
        keys = new_rows
    else:
        keys = tuple(jnp.concatenate([pc.astype(r.dtype), r], axis=1) for pc, r in zip(past, new_rows))
    k_sb, v_sb, k_ds, v_ds, k_ix, k_fx, v_fx, lf_all = keys
    n_keys = k_sb.shape[1]
    kpos = jnp.arange(n_keys, dtype=jnp.int32)
    qpos = (n_keys - s) + jnp.arange(s, dtype=jnp.int32)
    d_all = jnp.cumsum(lf_all.astype(jnp.float32), axis=1)
    d_q = d_all[:, n_keys - s:]
    o_sb = sweep(sb_block, (sb_q,), qpos, (k_sb, v_sb, kpos))
    o_ds = sweep(functools.partial(dsa_block, topk=min(DSA_TOPK_MAX, n_keys // 4)),
                 (ds_q, ix_q, ix_w), qpos, (k_ds, v_ds, k_ix, kpos, rel_bias))
    o_fx = sweep(fox_block, (fx_q, d_q), qpos, (k_fx, v_fx, d_all, kpos))
    gate = jax.nn.sigmoid(gate.astype(jnp.float32)).astype(h.dtype).reshape(b, s, N_BRANCHES, D_MODEL)
    merged = sum(gate[:, :, r] * (o.reshape(b, s, BRANCH_WIDTH) @ w_branch[r])
                 for r, o in enumerate((o_sb, o_ds, o_fx)))
    return merged @ w_out, new_rows


def grouped_experts(h, expert, weight, w_gate, w_up, w_down):
    t, d = h.shape
    n = t * TOP_K
    e_flat = expert.reshape(n)
    order = jnp.argsort(e_flat)
    e_sorted = e_flat[order]
    tok_sorted = (order // TOP_K).astype(jnp.int32)
    w_sorted = weight.reshape(n)[order].astype(h.dtype)
    counts = jnp.zeros((N_EXPERTS,), jnp.int32).at[e_flat].add(1)
    padded = (counts + MOE_BLOCK - 1) // MOE_BLOCK * MOE_BLOCK
    start = jnp.cumsum(counts) - counts
    pad_end = jnp.cumsum(padded)
    pad_start = pad_end - padded
    dest = pad_start[e_sorted] + jnp.arange(n, dtype=jnp.int32) - start[e_sorted]
    n_blocks = -(-n // MOE_BLOCK) + N_EXPERTS
    slots = n_blocks * MOE_BLOCK
    slot_tok = jnp.full((slots,), t, jnp.int32).at[dest].set(tok_sorted)
    slot_w = jnp.zeros((slots,), h.dtype).at[dest].set(w_sorted)
    block_expert = jnp.minimum(
        jnp.searchsorted(pad_end, jnp.arange(n_blocks, dtype=jnp.int32) * MOE_BLOCK, side='right'),
        N_EXPERTS - 1)
    h_pad = jnp.concatenate([h, jnp.zeros((1, d), h.dtype)], axis=0)

    def run_block(blk):
        tok, e = blk
        xb = h_pad[tok]
        a = jax.nn.silu(xb @ w_gate[e]) * (xb @ w_up[e])
        return a @ w_down[e]

    yb = lax.map(run_block, (slot_tok.reshape(n_blocks, MOE_BLOCK), block_expert))
    y = jnp.zeros((t + 1, d), h.dtype).at[slot_tok].add(yb.reshape(slots, d) * slot_w[:, None])
    return y[:t]


def hier_moe(h, w_grp, b_grp, w_exp, b_exp, w_gate, w_up, w_down):
    t = h.shape[0]
    g_logit = (h @ w_grp).astype(jnp.float32) + b_grp.astype(jnp.float32)
    g_prob = jax.nn.softmax(g_logit, axis=-1)
    grp = jnp.argmax(g_logit, axis=-1).astype(jnp.int32)
    p_grp = jnp.take_along_axis(g_prob, grp[:, None], axis=-1)
    e_logit = ((h @ w_exp).astype(jnp.float32) + b_exp.astype(jnp.float32)).reshape(t, N_GROUPS, EXPERTS_PER_GROUP)
    e_logit = jnp.take_along_axis(e_logit, grp[:, None, None], axis=1)[:, 0]
    top_p, top_i = lax.top_k(jax.nn.softmax(e_logit, axis=-1), TOP_K)
    weight = p_grp * top_p / jnp.sum(top_p, axis=-1, keepdims=True)
    expert = grp[:, None] * EXPERTS_PER_GROUP + top_i.astype(jnp.int32)
    return grouped_experts(h, expert, weight, w_gate, w_up, w_down)


def trunk(x, c, past, w_mod, b_mod, g_norm1, g_norm2, w_in, q_norm_dsa, k_norm_dsa,
          q_norm_fox, k_norm_fox, b_forget, w_branch, w_out, rel_bias, w_route_grp,
          b_route_grp, w_route_exp, b_route_exp, w_exp_gate, w_exp_up, w_exp_down):
    b, s, d = x.shape
    rows = []
    for l in range(DEPTH):
        mod = jax.nn.silu(c) @ w_mod[l] + b_mod[l]
        sh1, sc1, g1, sh2, sc2, g2 = [m[:, None, :] for m in jnp.split(mod, 6, axis=-1)]
        h = rms_norm(x, g_norm1[l]) * (1 + sc1) + sh1
        layer_past = None if past is None else tuple(pc[l] for pc in past)
        y, new = token_mixers(h, layer_past, w_in[l], q_norm_dsa[l], k_norm_dsa[l],
                              q_norm_fox[l], k_norm_fox[l], b_forget[l], w_branch[l],
                              w_out[l], rel_bias)
        x = x + g1 * y
        h = rms_norm(x, g_norm2[l]) * (1 + sc2) + sh2
        f = hier_moe(h.reshape(b * s, d), w_route_grp[l], b_route_grp[l], w_route_exp[l],
                     b_route_exp[l], w_exp_gate[l], w_exp_up[l], w_exp_down[l])
        x = x + g2 * f.reshape(b, s, d)
        rows.append(new)
    return x, tuple(jnp.stack(r) for r in zip(*rows))


def setup_inputs(seed: int = 0) -> dict:
    key = jax.random.key(seed)
    ks = iter(jax.random.split(key, 48))
    nrm = lambda shape, scale=1.0: jax.random.normal(next(ks), shape, jnp.float32) * scale
    d = D_MODEL
    return {
        'x_prompt': nrm((BATCH, SEQ, d)),
        'x_sample': nrm((DEC_BATCH, DEC_SEQ, d)),
        'c_prompt': nrm((BATCH, d)),
        'c_sample': nrm((DEC_BATCH, d)),
        'cache_sb_k': nrm((DEPTH, DEC_BATCH, PAST_LEN, SB_HEADS, HEAD_DIM)),
        'cache_sb_v': nrm((DEPTH, DEC_BATCH, PAST_LEN, SB_HEADS, HEAD_DIM)),
        'cache_dsa_k': nrm((DEPTH, DEC_BATCH, PAST_LEN, DSA_KV_HEADS, HEAD_DIM)),
        'cache_dsa_v': nrm((DEPTH, DEC_BATCH, PAST_LEN, DSA_KV_HEADS, HEAD_DIM)),
        'cache_dsa_kidx': nrm((DEPTH, DEC_BATCH, PAST_LEN, IDX_DIM)),
        'cache_fox_k': nrm((DEPTH, DEC_BATCH, PAST_LEN, FOX_HEADS, HEAD_DIM)),
        'cache_fox_v': nrm((DEPTH, DEC_BATCH, PAST_LEN, FOX_HEADS, HEAD_DIM)),
        'cache_fox_logf': jax.nn.log_sigmoid(3.0 + nrm((DEPTH, DEC_BATCH, PAST_LEN, FOX_HEADS), 0.5)),
        'w_mod': nrm((DEPTH, d, 6 * d), 0.5 * d ** -0.5),
        'b_mod': nrm((DEPTH, 6 * d), 0.02),
        'g_norm1': 1.0 + nrm((DEPTH, d), 0.05),
        'g_norm2': 1.0 + nrm((DEPTH, d), 0.05),
        'w_in': nrm((DEPTH, d, IN_WIDTH), d ** -0.5),
        'q_norm_dsa': 1.0 + nrm((DEPTH, HEAD_DIM), 0.05),
        'k_norm_dsa': 1.0 + nrm((DEPTH, HEAD_DIM), 0.05),
        'q_norm_fox': 1.0 + nrm((DEPTH, HEAD_DIM), 0.05),
        'k_norm_fox': 1.0 + nrm((DEPTH, HEAD_DIM), 0.05),
        'b_forget': 3.0 + nrm((DEPTH, FOX_HEADS), 0.5),
        'w_branch': nrm((DEPTH, N_BRANCHES, BRANCH_WIDTH, d), BRANCH_WIDTH ** -0.5),
        'w_out': nrm((DEPTH, d, d), d ** -0.5),
        'rel_bias': nrm((N_BUCKETS, DSA_HEADS), 0.5),
        'w_route_grp': nrm((DEPTH, d, N_GROUPS), d ** -0.5),
        'b_route_grp': nrm((DEPTH, N_GROUPS), 0.01),
        'w_route_exp': nrm((DEPTH, d, N_EXPERTS), d ** -0.5),
        'b_route_exp': nrm((DEPTH, N_EXPERTS), 0.01),
        'w_exp_gate': nrm((DEPTH, N_EXPERTS, d, EXPERT_FF), d ** -0.5),
        'w_exp_up': nrm((DEPTH, N_EXPERTS, d, EXPERT_FF), d ** -0.5),
        'w_exp_down': nrm((DEPTH, N_EXPERTS, EXPERT_FF, d), EXPERT_FF ** -0.5),
    }


def reference(x_prompt, x_sample, c_prompt, c_sample, cache_sb_k, cache_sb_v, cache_dsa_k,
              cache_dsa_v, cache_dsa_kidx, cache_fox_k, cache_fox_v, cache_fox_logf,
              w_mod, b_mod, g_norm1, g_norm2, w_in, q_norm_dsa, k_norm_dsa, q_norm_fox,
              k_norm_fox, b_forget, w_branch, w_out, rel_bias, w_route_grp, b_route_grp,
              w_route_exp, b_route_exp, w_exp_gate, w_exp_up, w_exp_down):
    weights = (w_mod, b_mod, g_norm1, g_norm2, w_in, q_norm_dsa, k_norm_dsa, q_norm_fox,
               k_norm_fox, b_forget, w_branch, w_out, rel_bias, w_route_grp, b_route_grp,
               w_route_exp, b_route_exp, w_exp_gate, w_exp_up, w_exp_down)
    y_prompt, p_rows = trunk(x_prompt, c_prompt, None, *weights)
    (p_sb_k, p_sb_v, p_dsa_k, p_dsa_v, p_dsa_kidx, p_fox_k, p_fox_v, p_fox_logf) = p_rows
    past = (cache_sb_k, cache_sb_v, cache_dsa_k, cache_dsa_v, cache_dsa_kidx,
            cache_fox_k, cache_fox_v, cache_fox_logf)
    y_sample, s_rows = trunk(x_sample, c_sample, past, *weights)
    (s_sb_k, s_sb_v, s_dsa_k, s_dsa_v, s_dsa_kidx, s_fox_k, s_fox_v, s_fox_logf) = s_rows
    return (y_prompt, y_sample,
            p_sb_k, p_sb_v, p_dsa_k, p_dsa_v, p_dsa_kidx, p_fox_k, p_fox_v, p_fox_logf,
            s_sb_k, s_sb_v, s_dsa_k, s_dsa_v, s_dsa_kidx, s_fox_k, s_fox_v, s_fox_logf)
```

```python
import functools
import math

import jax
import jax.numpy as jnp
from jax import lax
from jax.experimental import pallas as pl
from jax.experimental.pallas import tpu as pltpu

F32 = jnp.float32
BF16 = jnp.bfloat16
I32 = jnp.int32

EPS = 1e-6
CHUNK = 64
DSA_TOPK_MAX = 256
MAX_DISTANCE = 128
TOP_K = 2
LANE = 128
NEG_BIG = -1e30
VMEM_LIMIT = 56 * 1024 * 1024


def _cparams(sem):
    return pltpu.CompilerParams(dimension_semantics=sem, vmem_limit_bytes=VMEM_LIMIT)


def _pick(n, pref):
    if n <= pref:
        return n
    t = pref
    while n % t:
        t //= 2
    return t


def _norm_mod_body(x_ref, g_ref, sc_ref, sh_ref, o_ref):
    x = x_ref[0]
    ms = jnp.mean(x * x, axis=-1, keepdims=True)
    y = x * lax.rsqrt(ms + EPS) * g_ref[...]
    o_ref[0] = (y * (1.0 + sc_ref[0]) + sh_ref[0]).astype(o_ref.dtype)


def _norm_mod(x, g, sc, sh, out_dtype=BF16):
    b, s, d = x.shape
    ts = _pick(s, 256)
    return pl.pallas_call(
        _norm_mod_body,
        grid=(b, s // ts),
        in_specs=[pl.BlockSpec((1, ts, d), lambda i, j: (i, j, 0)),
                  pl.BlockSpec((1, d), lambda i, j: (0, 0)),
                  pl.BlockSpec((1, 1, d), lambda i, j: (i, 0, 0)),
                  pl.BlockSpec((1, 1, d), lambda i, j: (i, 0, 0))],
        out_specs=pl.BlockSpec((1, ts, d), lambda i, j: (i, j, 0)),
        out_shape=jax.ShapeDtypeStruct((b, s, d), out_dtype),
        compiler_params=_cparams(("parallel", "parallel")),
        name="norm_mod",
    )(x, g.reshape(1, d), sc, sh)


def _log_sigmoid(x):
    return jnp.minimum(x, 0.0) - jnp.log1p(jnp.exp(-jnp.abs(x)))


def _epi_none(acc):
    return acc


def _epi_bias(acc, bias):
    return acc + bias


def _epi_head_rms(acc, gain):
    hd = gain.shape[-1]
    outs = []
    for c in range(acc.shape[-1] // hd):
        blk = acc[:, c * hd:(c + 1) * hd]
        ms = jnp.mean(blk * blk, axis=-1, keepdims=True)
        outs.append(blk * lax.rsqrt(ms + EPS) * gain)
    return jnp.concatenate(outs, axis=-1) if len(outs) > 1 else outs[0]


def _epi_masked_logsig(acc, bias, mask):
    return jnp.where(mask > 0.0, _log_sigmoid(acc + bias), acc)


def _epi_sigmoid(acc):
    return jax.nn.sigmoid(acc)


def _epi_residual(acc, res, gate):
    return res + gate * acc


def _mm_body(*refs, epi, n_extra, silu_a):
    a_ref, b_ref = refs[0], refs[1]
    extras = refs[2:2 + n_extra]
    o_ref = refs[2 + n_extra]
    a = a_ref[0]
    if silu_a:
        a = a.astype(F32)
        a = a * jax.nn.sigmoid(a)
    acc = jnp.dot(a.astype(BF16), b_ref[...].astype(BF16), preferred_element_type=F32)
    vals = []
    for r in extras:
        v = r[...]
        vals.append(v[0] if v.ndim == 3 else v)
    o_ref[0] = epi(acc, *vals).astype(o_ref.dtype)


def _matmul(a, b, *, epi=_epi_none, extras=(), out_dtype=F32, tm=1024, tn=512, silu_a=False, name="matmul"):
    bsz, s, k = a.shape
    n = b.shape[1]
    tm = _pick(s, tm)
    tn = _pick(n, tn)
    in_specs = [pl.BlockSpec((1, tm, k), lambda bb, i, j: (bb, i, 0)),
                pl.BlockSpec((k, tn), lambda bb, i, j: (0, j))]
    args = [a, b]
    for arr, kind in extras:
        if kind == "row":
            in_specs.append(pl.BlockSpec((1, tn), lambda bb, i, j: (0, j)))
        elif kind == "batchrow":
            in_specs.append(pl.BlockSpec((1, 1, tn), lambda bb, i, j: (bb, 0, j)))
        elif kind == "tile":
            in_specs.append(pl.BlockSpec((1, tm, tn), lambda bb, i, j: (bb, i, j)))
        else:
            in_specs.append(pl.BlockSpec(arr.shape, lambda bb, i, j, nd=arr.ndim: (0,) * nd))
        args.append(arr)
    return pl.pallas_call(
        functools.partial(_mm_body, epi=epi, n_extra=len(extras), silu_a=silu_a),
        grid=(bsz, s // tm, n // tn),
        in_specs=in_specs,
        out_specs=pl.BlockSpec((1, tm, tn), lambda bb, i, j: (bb, i, j)),
        out_shape=jax.ShapeDtypeStruct((bsz, s, n), out_dtype),
        compiler_params=_cparams(("parallel", "parallel", "parallel")),
        name=name,
    )(*args)


def _sb_body(q_ref, k_ref, v_ref, o_ref, *, tq, tk, off, scale):
    i = pl.program_id(2)
    q_lo = off + i * tq
    q = (q_ref[0].astype(F32) * scale).astype(BF16)
    row = lax.broadcasted_iota(I32, (tq, tk), 0)
    col = lax.broadcasted_iota(I32, (tq, tk), 1)
    tri_r = lax.broadcasted_iota(I32, (tk, tk), 0)
    tri_c = lax.broadcasted_iota(I32, (tk, tk), 1)
    suffix = jnp.where(tri_r >= tri_c, 1.0, 0.0).astype(BF16)
    n_blocks = (q_lo + tq - 2) // tk + 1
    n_plain = q_lo // tk

    def step(kb, carry, masked):
        run, acc = carry
        ks = pl.multiple_of(kb * tk, tk)
        k = k_ref[0, pl.ds(ks, tk), :].astype(BF16)
        v = v_ref[0, pl.ds(ks, tk), :].astype(BF16)
        z = lax.dot_general(q, k, (((1,), (1,)), ((), ())), preferred_element_type=F32)
        log_beta = _log_sigmoid(z)
        log_keep = log_beta - z
        if masked:
            mask = (ks + col) < (q_lo + row)
            log_keep = jnp.where(mask, log_keep, 0.0)
        hi = log_keep.astype(BF16)
        lo = (log_keep - hi.astype(F32)).astype(BF16)
        incl = (jnp.dot(hi, suffix, preferred_element_type=F32)
                + jnp.dot(lo, suffix, preferred_element_type=F32))
        w = jnp.exp(log_beta + (incl - log_keep) + run)
        if masked:
            w = jnp.where(mask, w, 0.0)
        acc = acc + jnp.dot(w.astype(BF16), v, preferred_element_type=F32)
        run = run + incl[:, 0:1]
        return run, acc

    carry = (jnp.zeros((tq, 1), F32), jnp.zeros((tq, q.shape[1]), F32))
    carry = lax.fori_loop(0, n_blocks - n_plain,
                          lambda n, c: step(n_blocks - 1 - n, c, True), carry)
    carry = lax.fori_loop(0, n_plain,
                          lambda n, c: step(n_plain - 1 - n, c, False), carry)
    o_ref[0] = carry[1].astype(o_ref.dtype)


def _sb_attention(q, k, v, *, heads, hd, off, tq, tk):
    b, s, _ = q.shape
    lp = k.shape[1]
    return pl.pallas_call(
        functools.partial(_sb_body, tq=tq, tk=tk, off=off, scale=hd ** -0.5),
        grid=(b, heads, s // tq),
        in_specs=[pl.BlockSpec((1, tq, hd), lambda bb, h, i: (bb, i, h)),
                  pl.BlockSpec((1, lp, hd), lambda bb, h, i: (bb, 0, h)),
                  pl.BlockSpec((1, lp, hd), lambda bb, h, i: (bb, 0, h))],
        out_specs=pl.BlockSpec((1, tq, hd), lambda bb, h, i: (bb, i, h)),
        out_shape=jax.ShapeDtypeStruct(q.shape, BF16),
        compiler_params=_cparams(("parallel", "parallel", "arbitrary")),
        name="sb_attention",
    )(q, k, v)


def _cumsum_body(x_ref, o_ref):
    x = x_ref[0]
    n = x.shape[1]
    lane = lax.broadcasted_iota(I32, x.shape, 1)
    shift = 1
    while shift < n:
        x = x + jnp.where(lane >= shift, pltpu.roll(x, shift, axis=1), 0.0)
        shift *= 2
    o_ref[0] = x


def _cumsum_lanes(x):
    b, h, n = x.shape
    return pl.pallas_call(
        _cumsum_body,
        grid=(b,),
        in_specs=[pl.BlockSpec((1, h, n), lambda i: (i, 0, 0))],
        out_specs=pl.BlockSpec((1, h, n), lambda i: (i, 0, 0)),
        out_shape=jax.ShapeDtypeStruct(x.shape, F32),
        compiler_params=_cparams(("parallel",)),
        name="cumsum_logf",
    )(x)


def _fox_body(q_ref, k_ref, v_ref, dq_ref, dk_ref, o_ref, *, tq, tk, off, scale):
    i = pl.program_id(2)
    q_lo = off + i * tq
    q = (q_ref[0].astype(F32) * scale).astype(BF16)
    dq = dq_ref[0, 0]
    row = lax.broadcasted_iota(I32, (tq, tk), 0)
    col = lax.broadcasted_iota(I32, (tq, tk), 1)
    n_blocks = (q_lo + tq - 1) // tk + 1
    n_plain = q_lo // tk
    rows_per_blk = tk // LANE

    def step(kb, carry, masked):
        m, l, acc = carry
        ks = pl.multiple_of(kb * tk, tk)
        k = k_ref[0, pl.ds(ks, tk), :].astype(BF16)
        v = v_ref[0, pl.ds(ks, tk), :].astype(BF16)
        dk_rows = dk_ref[0, 0, pl.ds(kb * rows_per_blk, rows_per_blk), :]
        dk = jnp.concatenate([dk_rows[r:r + 1] for r in range(rows_per_blk)], axis=1)
        z = lax.dot_general(q, k, (((1,), (1,)), ((), ())), preferred_element_type=F32)
        z = z + dq - dk
        if masked:
            mask = (ks + col) <= (q_lo + row)
            z = jnp.where(mask, z, NEG_BIG)
        m_new = jnp.maximum(m, jnp.max(z, axis=1, keepdims=True))
        p = jnp.exp(z - m_new)
        if masked:
            p = jnp.where(mask, p, 0.0)
        alpha = jnp.exp(m - m_new)
        l = alpha * l + jnp.sum(p, axis=1, keepdims=True)
        acc = alpha * acc + jnp.dot(p.astype(BF16), v, preferred_element_type=F32)
        return m_new, l, acc

    carry = (jnp.full((tq, 1), NEG_BIG, F32), jnp.zeros((tq, 1), F32), jnp.zeros((tq, q.shape[1]), F32))
    carry = lax.fori_loop(0, n_blocks - n_plain,
                          lambda n, c: step(n_blocks - 1 - n, c, True), carry)
    carry = lax.fori_loop(0, n_plain,
                          lambda n, c: step(n_plain - 1 - n, c, False), carry)
    o_ref[0] = (carry[2] / carry[1]).astype(o_ref.dtype)


def _fox_attention(q, k, v, d_q, d_k, *, heads, hd, off, tq, tk):
    b, s, _ = q.shape
    lp = k.shape[1]
    return pl.pallas_call(
        functools.partial(_fox_body, tq=tq, tk=tk, off=off, scale=hd ** -0.5),
        grid=(b, heads, s // tq),
        in_specs=[pl.BlockSpec((1, tq, hd), lambda bb, h, i: (bb, i, h)),
                  pl.BlockSpec((1, lp, hd), lambda bb, h, i: (bb, 0, h)),
                  pl.BlockSpec((1, lp, hd), lambda bb, h, i: (bb, 0, h)),
                  pl.BlockSpec((1, 1, tq, 1), lambda bb, h, i: (bb, h, i, 0)),
                  pl.BlockSpec((1, 1, lp // LANE, LANE), lambda bb, h, i: (bb, h, 0, 0))],
        out_specs=pl.BlockSpec((1, tq, hd), lambda bb, h, i: (bb, i, h)),
        out_shape=jax.ShapeDtypeStruct(q.shape, BF16),
        compiler_params=_cparams(("parallel", "parallel", "arbitrary")),
        name="fox_attention",
    )(q, k, v, d_q, d_k)


def _sortable(x):
    u = lax.bitcast_convert_type(x, I32)
    return u ^ (lax.shift_right_arithmetic(u, 31) & 0x7FFFFFFF)


def _indexer_body(qi_ref, wi_ref, ka_ref, kb_ref, sc_ref, thr_ref, key_ref, *, tq, tk, off, topk, idx_heads, w_scale):
    i = pl.program_id(1)
    q_lo = off + i * tq
    lp = ka_ref.shape[1]
    adm_end = ((q_lo + tq - 1) // CHUNK + 1) * CHUNK
    n_adm = (adm_end + tk - 1) // tk
    n_all = lp // tk
    wi = wi_ref[0] * w_scale
    qi = qi_ref[0].astype(BF16)
    pair = ka_ref.shape[2]
    w_cols = [wi[:, j:j + 1] for j in range(idx_heads)]
    q_pairs = [qi[:, p * pair:(p + 1) * pair] for p in range(idx_heads // 2)]
    row = lax.broadcasted_iota(I32, (tq, tk), 0)
    col = lax.broadcasted_iota(I32, (tq, tk), 1)
    q_chunk = (q_lo + row) // CHUNK

    def score_step(kb, _):
        ks = pl.multiple_of(kb * tk, tk)
        k_even = ka_ref[0, pl.ds(ks, tk), :].astype(BF16)
        k_odd = kb_ref[0, pl.ds(ks, tk), :].astype(BF16)
        total = jnp.zeros((tq, tk), F32)
        for p in range(idx_heads // 2):
            s0 = lax.dot_general(q_pairs[p], k_even, (((1,), (1,)), ((), ())), preferred_element_type=F32)
            s1 = lax.dot_general(q_pairs[p], k_odd, (((1,), (1,)), ((), ())), preferred_element_type=F32)
            total = total + jnp.maximum(s0, 0.0) * w_cols[2 * p] + jnp.maximum(s1, 0.0) * w_cols[2 * p + 1]
        admissible = ((ks + col) // CHUNK) <= q_chunk
        total = jnp.where(admissible, total, -jnp.inf)
        sc_ref[0, :, pl.ds(ks, tk)] = total
        key_ref[:, pl.ds(ks, tk)] = _sortable(total)
        return 0

    lax.fori_loop(0, n_adm, score_step, 0)

    def fill_step(kb, _):
        ks = pl.multiple_of(kb * tk, tk)
        sc_ref[0, :, pl.ds(ks, tk)] = jnp.full((tq, tk), -jnp.inf, F32)
        return 0

    lax.fori_loop(n_adm, n_all, fill_step, 0)

    sign = jnp.int32(-2 ** 31)

    def bit_step(n, t_bits):
        bit = lax.shift_left(jnp.int32(1), 31 - n)
        cand_bits = t_bits | bit
        cand = cand_bits ^ sign

        def count_step(kb, cnt):
            ks = pl.multiple_of(kb * tk, tk)
            keys = key_ref[:, pl.ds(ks, tk)]
            hit = jnp.where(keys >= cand, 1.0, 0.0)
            for c in range(tk // LANE):
                cnt = cnt + hit[:, c * LANE:(c + 1) * LANE]
            return cnt

        cnt = lax.fori_loop(0, n_adm, count_step, jnp.zeros((tq, LANE), F32))
        enough = jnp.sum(cnt, axis=1, keepdims=True) >= float(topk)
        return jnp.where(enough, cand_bits, t_bits)

    t_bits = lax.fori_loop(0, 32, bit_step, jnp.zeros((tq, 1), I32))
    t_key = t_bits ^ sign
    t_u = t_key ^ (lax.shift_right_arithmetic(t_key, 31) & 0x7FFFFFFF)
    thr_ref[0] = lax.bitcast_convert_type(t_u, F32)


def _indexer(qi, wi, k_even, k_odd, *, off, topk, idx_heads, tq, tk):
    b, s, _ = qi.shape
    lp = k_even.shape[1]
    idx_dim = qi.shape[2] // idx_heads
    w_scale = (idx_heads ** -0.5) * (idx_dim ** -0.5)
    return pl.pallas_call(
        functools.partial(_indexer_body, tq=tq, tk=tk, off=off, topk=topk, idx_heads=idx_heads, w_scale=w_scale),
        grid=(b, s // tq),
        in_specs=[pl.BlockSpec((1, tq, qi.shape[2]), lambda bb, i: (bb, i, 0)),
                  pl.BlockSpec((1, tq, idx_heads), lambda bb, i: (bb, i, 0)),
                  pl.BlockSpec((1, lp, k_even.shape[2]), lambda bb, i: (bb, 0, 0)),
                  pl.BlockSpec((1, lp, k_odd.shape[2]), lambda bb, i: (bb, 0, 0))],
        out_specs=[pl.BlockSpec((1, tq, lp), lambda bb, i: (bb, i, 0)),
                   pl.BlockSpec((1, tq, 1), lambda bb, i: (bb, i, 0))],
        out_shape=[jax.ShapeDtypeStruct((b, s, lp), F32), jax.ShapeDtypeStruct((b, s, 1), F32)],
        scratch_shapes=[pltpu.VMEM((tq, lp), I32)],
        compiler_params=_cparams(("parallel", "arbitrary")),
        name="dsa_indexer",
    )(qi, wi, k_even, k_odd)


def _bias_tile_body(rb_ref, o_ref, *, n_buckets, heads, n_off):
    half = n_buckets // 2
    exact = half // 2
    row = lax.broadcasted_iota(I32, (LANE, LANE), 0)
    col = lax.broadcasted_iota(I32, (LANE, LANE), 1)
    for d in range(n_off):
        rel = d * LANE + row - col
        side = jnp.where(rel < 0, half, 0)
        a = jnp.abs(rel)
        far = exact + (jnp.log(jnp.maximum(a, 1).astype(F32) / exact)
                       / math.log(MAX_DISTANCE / exact) * (half - exact)).astype(I32)
        far = jnp.minimum(far, half - 1)
        bucket = side + jnp.where(a < exact, a, far)
        for h in range(heads):
            tile = jnp.zeros((LANE, LANE), F32)
            for bkt in range(n_buckets):
                tile = jnp.where(bucket == bkt, rb_ref[bkt, h], tile)
            o_ref[d, h] = tile


def _bias_tiles(rel_bias, n_off=3):
    n_buckets, heads = rel_bias.shape
    return pl.pallas_call(
        functools.partial(_bias_tile_body, n_buckets=n_buckets, heads=heads, n_off=n_off),
        in_specs=[pl.BlockSpec(memory_space=pltpu.SMEM)],
        out_specs=pl.BlockSpec(memory_space=pltpu.VMEM),
        out_shape=jax.ShapeDtypeStruct((n_off, heads, LANE, LANE), F32),
        name="dsa_bias_tiles",
    )(rel_bias)


def _dsa_body(q_ref, k_ref, v_ref, sc_ref, thr_ref, bias_ref, o_ref, *, tq, tk, off, scale, group, hd):
    i = pl.program_id(2)
    q_lo = off + i * tq
    n_off = bias_ref.shape[0]
    adm_end = ((q_lo + tq - 1) // CHUNK + 1) * CHUNK
    n_blocks = (adm_end + tk - 1) // tk
    thr = thr_ref[0]
    qs = [(q_ref[0, :, g * hd:(g + 1) * hd].astype(F32) * scale).astype(BF16) for g in range(group)]

    def step(kb, carry):
        ks = pl.multiple_of(kb * tk, tk)
        k = k_ref[0, pl.ds(ks, tk), :].astype(BF16)
        v = v_ref[0, pl.ds(ks, tk), :].astype(BF16)
        sc = sc_ref[0, :, pl.ds(ks, tk)]
        sel = jnp.logical_and(sc >= thr, jnp.abs(sc) < jnp.inf)
        d_idx = jnp.clip((q_lo - ks) // LANE, 0, n_off - 1)
        out = []
        for g in range(group):
            m, l, acc = carry[g]
            z = lax.dot_general(qs[g], k, (((1,), (1,)), ((), ())), preferred_element_type=F32)
            z = z + bias_ref[d_idx, g, 0:tq, :]
            z = jnp.where(sel, z, NEG_BIG)
            m_new = jnp.maximum(m, jnp.max(z, axis=1, keepdims=True))
            p = jnp.where(sel, jnp.exp(z - m_new), 0.0)
            alpha = jnp.exp(m - m_new)
            l = alpha * l + jnp.sum(p, axis=1, keepdims=True)
            acc = alpha * acc + jnp.dot(p.astype(BF16), v, preferred_element_type=F32)
            out.append((m_new, l, acc))
        return tuple(out)

    init = tuple((jnp.full((tq, 1), NEG_BIG, F32), jnp.zeros((tq, 1), F32), jnp.zeros((tq, hd), F32))
                 for _ in range(group))
    fin = lax.fori_loop(0, n_blocks, step, init)
    for g in range(group):
        o_ref[0, :, g * hd:(g + 1) * hd] = (fin[g][2] / fin[g][1]).astype(o_ref.dtype)


def _dsa_attention(q, k, v, score, thr, bias_tiles, *, kv_heads, group, hd, off, tq):
    b, s, _ = q.shape
    lp = k.shape[1]
    tk = LANE
    n_off = bias_tiles.shape[0]
    return pl.pallas_call(
        functools.partial(_dsa_body, tq=tq, tk=tk, off=off, scale=hd ** -0.5, group=group, hd=hd),
        grid=(b, kv_heads, s // tq),
        in_specs=[pl.BlockSpec((1, tq, group * hd), lambda bb, h, i: (bb, i, h)),
                  pl.BlockSpec((1, lp, hd), lambda bb, h, i: (bb, 0, h)),
                  pl.BlockSpec((1, lp, hd), lambda bb, h, i: (bb, 0, h)),
                  pl.BlockSpec((1, tq, lp), lambda bb, h, i: (bb, i, 0)),
                  pl.BlockSpec((1, tq, 1), lambda bb, h, i: (bb, i, 0)),
                  pl.BlockSpec((n_off, group, LANE, LANE), lambda bb, h, i: (0, h, 0, 0))],
        out_specs=pl.BlockSpec((1, tq, group * hd), lambda bb, h, i: (bb, i, h)),
        out_shape=jax.ShapeDtypeStruct(q.shape, BF16),
        compiler_params=_cparams(("parallel", "parallel", "arbitrary")),
        name="dsa_attention",
    )(q, k, v, score, thr, bias_tiles)


def _merge_body(o0_ref, o1_ref, o2_ref, g0_ref, g1_ref, g2_ref, w_ref, out_ref):
    total = None
    for r, (o_ref, g_ref) in enumerate(((o0_ref, g0_ref), (o1_ref, g1_ref), (o2_ref, g2_ref))):
        y = jnp.dot(o_ref[0], w_ref[r], preferred_element_type=F32) * g_ref[0]
        total = y if total is None else total + y
    out_ref[0] = total.astype(out_ref.dtype)


def _merge(branches, gate, w_branch):
    b, s, width = branches[0].shape
    d = w_branch.shape[2]
    tm = _pick(s, 512)
    tn = _pick(d, 512)
    nj = d // tn
    o_spec = pl.BlockSpec((1, tm, width), lambda bb, i, j: (bb, i, 0))
    g_specs = [pl.BlockSpec((1, tm, tn), lambda bb, i, j, r=r: (bb, i, r * nj + j)) for r in range(3)]
    return pl.pallas_call(
        _merge_body,
        grid=(b, s // tm, nj),
        in_specs=[o_spec, o_spec, o_spec, *g_specs,
                  pl.BlockSpec((3, width, tn), lambda bb, i, j: (0, 0, j))],
        out_specs=pl.BlockSpec((1, tm, tn), lambda bb, i, j: (bb, i, j)),
        out_shape=jax.ShapeDtypeStruct((b, s, d), BF16),
        compiler_params=_cparams(("parallel", "parallel", "parallel")),
        name="branch_merge",
    )(*branches, gate, gate, gate, w_branch)


def _router_body(h_ref, w_ref, b_ref, info_ref, cnt_ref, run_ref, *, n_groups, per_group):
    step = pl.program_id(0)

    @pl.when(step == 0)
    def _():
        run_ref[...] = jnp.zeros_like(run_ref)

    tm = h_ref.shape[0]
    logits = jnp.dot(h_ref[...].astype(BF16), w_ref[...], preferred_element_type=F32) + b_ref[...]
    lane = lax.broadcasted_iota(I32, (tm, LANE), 1).astype(F32)
    far = float(4 * LANE)
    is_grp = lane < n_groups
    g_logit = jnp.where(is_grp, logits, -jnp.inf)
    g_max = jnp.max(g_logit, axis=1, keepdims=True)
    grp = jnp.min(jnp.where(g_logit == g_max, lane, far), axis=1, keepdims=True)
    p_grp = 1.0 / jnp.sum(jnp.where(is_grp, jnp.exp(logits - g_max), 0.0), axis=1, keepdims=True)
    e_lo = n_groups + grp * per_group
    in_grp = jnp.logical_and(lane >= e_lo, lane < e_lo + per_group)
    e_logit = jnp.where(in_grp, logits, -jnp.inf)
    m1 = jnp.max(e_logit, axis=1, keepdims=True)
    i1 = jnp.min(jnp.where(e_logit == m1, lane, far), axis=1, keepdims=True)
    e_logit2 = jnp.where(lane == i1, -jnp.inf, e_logit)
    m2 = jnp.max(e_logit2, axis=1, keepdims=True)
    i2 = jnp.min(jnp.where(jnp.logical_and(e_logit2 == m2, in_grp), lane, far), axis=1, keepdims=True)
    e2 = jnp.exp(m2 - m1)
    w1 = p_grp / (1.0 + e2)
    w2 = p_grp * e2 / (1.0 + e2)
    hit1 = lane == i1
    hit2 = lane == i2
    onehot = jnp.where(jnp.logical_or(hit1, hit2), 1.0, 0.0)
    r_i = lax.broadcasted_iota(I32, (tm, tm), 0)
    c_i = lax.broadcasted_iota(I32, (tm, tm), 1)
    earlier = jnp.where(c_i < r_i, 1.0, 0.0).astype(BF16)
    before = jnp.dot(earlier, onehot.astype(BF16), preferred_element_type=F32) + run_ref[...]
    r1 = jnp.sum(jnp.where(hit1, before, 0.0), axis=1, keepdims=True)
    r2 = jnp.sum(jnp.where(hit2, before, 0.0), axis=1, keepdims=True)
    run_ref[...] = run_ref[...] + jnp.sum(onehot, axis=0, keepdims=True)
    cnt_ref[...] = run_ref[...]
    vals = (i1 - n_groups, i2 - n_groups, w1, w2, r1, r2)
    info = jnp.zeros((tm, LANE), F32)
    for pos, val in enumerate(vals):
        info = jnp.where(lane == pos, val, info)
    info_ref[...] = info


def _router(h, w_route, b_route, *, n_groups, per_group):
    t, d = h.shape
    tm = _pick(t, 256)
    return pl.pallas_call(
        functools.partial(_router_body, n_groups=n_groups, per_group=per_group),
        grid=(t // tm,),
        in_specs=[pl.BlockSpec((tm, d), lambda i: (i, 0)),
                  pl.BlockSpec((d, LANE), lambda i: (0, 0)),
                  pl.BlockSpec((1, LANE), lambda i: (0, 0))],
        out_specs=[pl.BlockSpec((tm, LANE), lambda i: (i, 0)),
                   pl.BlockSpec((1, LANE), lambda i: (0, 0))],
        out_shape=[jax.ShapeDtypeStruct((t, LANE), F32), jax.ShapeDtypeStruct((1, LANE), F32)],
        scratch_shapes=[pltpu.VMEM((1, LANE), F32)],
        compiler_params=_cparams(("arbitrary",)),
        name="moe_router",
    )(h, w_route, b_route)


def _dispatch_body(dest_ref, h_ref, slots_in_ref, slots_ref, sem, *, tm):
    del slots_in_ref
    base = pl.program_id(0) * tm

    def copy(r, kk):
        return pltpu.make_async_copy(h_ref.at[pl.ds(r, 1)], slots_ref.at[pl.ds(dest_ref[(base + r) * TOP_K + kk], 1)], sem)

    def start(r, _):
        for kk in range(TOP_K):
            copy(r, kk).start()
        return 0

    def wait(r, _):
        for kk in range(TOP_K):
            copy(r, kk).wait()
        return 0

    lax.fori_loop(0, tm, start, 0)
    lax.fori_loop(0, tm, wait, 0)


def _dispatch(h, dest, n_slots):
    t, d = h.shape
    tm = _pick(t, 256)
    grid_spec = pltpu.PrefetchScalarGridSpec(
        num_scalar_prefetch=1,
        grid=(t // tm,),
        in_specs=[pl.BlockSpec((tm, d), lambda i, dest_ref: (i, 0)),
                  pl.BlockSpec(memory_space=pl.ANY)],
        out_specs=pl.BlockSpec(memory_space=pl.ANY),
        scratch_shapes=[pltpu.SemaphoreType.DMA(())],
    )
    return pl.pallas_call(
        functools.partial(_dispatch_body, tm=tm),
        grid_spec=grid_spec,
        out_shape=jax.ShapeDtypeStruct((n_slots, d), h.dtype),
        input_output_aliases={2: 0},
        compiler_params=_cparams(("arbitrary",)),
        name="moe_dispatch",
    )(dest, h, jnp.zeros((n_slots, d), h.dtype))


def _experts_body(be_ref, nb_ref, x_ref, wg_ref, wu_ref, wd_ref, o_ref):
    blk = pl.program_id(0)
    f = pl.program_id(1)

    @pl.when(blk < nb_ref[0])
    def _():
        x = x_ref[...].astype(BF16)
        gate = jnp.dot(x, wg_ref[0], preferred_element_type=F32)
        up = jnp.dot(x, wu_ref[0], preferred_element_type=F32)
        act = (gate * jax.nn.sigmoid(gate) * up).astype(BF16)
        part = jnp.dot(act, wd_ref[0], preferred_element_type=F32)

        @pl.when(f == 0)
        def _():
            o_ref[...] = part

        @pl.when(f > 0)
        def _():
            o_ref[...] = o_ref[...] + part

    @pl.when(jnp.logical_and(blk >= nb_ref[0], f == 0))
    def _():
        o_ref[...] = jnp.zeros_like(o_ref)


def _experts(slots, block_expert, n_used, w_gate, w_up, w_down, *, bm):
    n_slots, d = slots.shape
    ff = w_gate.shape[2]
    tf = _pick(ff, 256)
    nblk = n_slots // bm

    def x_map(i, f, be, nb):
        return (jnp.minimum(i, nb[0] - 1), 0)

    def w_in_map(i, f, be, nb):
        live = i < nb[0]
        return (be[jnp.minimum(i, nb[0] - 1)], 0, jnp.where(live, f, ff // tf - 1))

    def w_out_map(i, f, be, nb):
        live = i < nb[0]
        return (be[jnp.minimum(i, nb[0] - 1)], jnp.where(live, f, ff // tf - 1), 0)

    grid_spec = pltpu.PrefetchScalarGridSpec(
        num_scalar_prefetch=2,
        grid=(nblk, ff // tf),
        in_specs=[pl.BlockSpec((bm, d), x_map),
                  pl.BlockSpec((1, d, tf), w_in_map),
                  pl.BlockSpec((1, d, tf), w_in_map),
                  pl.BlockSpec((1, tf, d), w_out_map)],
        out_specs=pl.BlockSpec((bm, d), lambda i, f, be, nb: (i, 0)),
    )
    return pl.pallas_call(
        _experts_body,
        grid_spec=grid_spec,
        out_shape=jax.ShapeDtypeStruct((n_slots, d), F32),
        compiler_params=_cparams(("arbitrary", "arbitrary")),
        name="moe_experts",
    )(block_expert, n_used, slots, w_gate, w_up, w_down)


def _combine_body(dest_ref, y_ref, x_ref, g_ref, wt_ref, o_ref, buf_ref, sem, *, tm, per_batch):
    base = (pl.program_id(0) * per_batch + pl.program_id(1)) * tm

    def copy(r, kk):
        return pltpu.make_async_copy(y_ref.at[pl.ds(dest_ref[(base + r) * TOP_K + kk], 1)],
                                     buf_ref.at[kk, pl.ds(r, 1)], sem)

    def start(r, _):
        for kk in range(TOP_K):
            copy(r, kk).start()
        return 0

    def wait(r, _):
        for kk in range(TOP_K):
            copy(r, kk).wait()
        return 0

    lax.fori_loop(0, tm, start, 0)
    lax.fori_loop(0, tm, wait, 0)
    wt = wt_ref[0]
    f = buf_ref[0] * wt[:, 0:1] + buf_ref[1] * wt[:, 1:2]
    o_ref[0] = x_ref[0] + g_ref[0] * f


def _combine(y_slots, dest, weights, x, gate):
    b, s, d = x.shape
    tm = _pick(s, 256)
    per_batch = s // tm
    grid_spec = pltpu.PrefetchScalarGridSpec(
        num_scalar_prefetch=1,
        grid=(b, per_batch),
        in_specs=[pl.BlockSpec(memory_space=pl.ANY),
                  pl.BlockSpec((1, tm, d), lambda bb, i, dest_ref: (bb, i, 0)),
                  pl.BlockSpec((1, 1, d), lambda bb, i, dest_ref: (bb, 0, 0)),
                  pl.BlockSpec((1, tm, TOP_K), lambda bb, i, dest_ref: (bb, i, 0))],
        out_specs=pl.BlockSpec((1, tm, d), lambda bb, i, dest_ref: (bb, i, 0)),
        scratch_shapes=[pltpu.VMEM((TOP_K, tm, d), F32), pltpu.SemaphoreType.DMA(())],
    )
    return pl.pallas_call(
        functools.partial(_combine_body, tm=tm, per_batch=per_batch),
        grid_spec=grid_spec,
        out_shape=jax.ShapeDtypeStruct(x.shape, F32),
        compiler_params=_cparams(("arbitrary", "arbitrary")),
        name="moe_combine",
    )(dest, y_slots, x, gate, weights)


def _pad_keys(a, lp):
    pad = lp - a.shape[1]
    if pad == 0:
        return a
    return jnp.concatenate([a, jnp.zeros((a.shape[0], pad) + a.shape[2:], a.dtype)], axis=1)


def _token_mixers(h, past, lw, dims):
    b, s, d = h.shape
    hd, sbh, dsh, kvh, ixh, ixd, fxh = (dims[k] for k in ("hd", "sb_heads", "dsa_heads", "kv_heads",
                                                          "idx_heads", "idx_dim", "fox_heads"))
    seg = lw["w_in_segments"]
    sb_q = _matmul(h, seg["sb_q"], out_dtype=BF16, name="in_sb_q")
    sb_k = _matmul(h, seg["sb_k"], name="in_sb_k")
    sb_v = _matmul(h, seg["sb_v"], name="in_sb_v")
    ds_q = _matmul(h, seg["ds_q"], epi=_epi_head_rms, extras=[(lw["q_norm_dsa"], "const")], out_dtype=BF16,
                   name="in_ds_q")
    ds_k = _matmul(h, seg["ds_k"], epi=_epi_head_rms, extras=[(lw["k_norm_dsa"], "const")], name="in_ds_k")
    ds_v = _matmul(h, seg["ds_v"], name="in_ds_v")
    ix_q = _matmul(h, seg["ix_q"], out_dtype=BF16, name="in_ix_q")
    small = _matmul(h, seg["small"], epi=_epi_masked_logsig,
                    extras=[(lw["small_bias"], "row"), (lw["small_mask"], "row")], name="in_small")
    fx_q = _matmul(h, seg["fx_q"], epi=_epi_head_rms, extras=[(lw["q_norm_fox"], "const")], out_dtype=BF16,
                   name="in_fx_q")
    fx_k = _matmul(h, seg["fx_k"], epi=_epi_head_rms, extras=[(lw["k_norm_fox"], "const")], name="in_fx_k")
    fx_v = _matmul(h, seg["fx_v"], name="in_fx_v")
    gate = _matmul(h, seg["gate"], epi=_epi_sigmoid, name="in_gate")
    ix_k = small[:, :, :ixd]
    ix_w = small[:, :, ixd:ixd + ixh]
    log_f = small[:, :, ixd + ixh:ixd + ixh + fxh]
    new_rows = (sb_k, sb_v, ds_k, ds_v, ix_k, fx_k, fx_v, log_f)

    if past is None:
        keys = new_rows
        n_keys = s
    else:
        shaped = (sb_k, sb_v, ds_k, ds_v, ix_k, fx_k, fx_v, log_f)
        keys = tuple(jnp.concatenate([pc.reshape(pc.shape[0], pc.shape[1], -1), r], axis=1)
                     for pc, r in zip(past, shaped))
        n_keys = keys[0].shape[1]
    off = n_keys - s
    tk = 256 if n_keys % 256 == 0 else LANE
    lp = -(-n_keys // tk) * tk
    k_sb, v_sb, k_ds, v_ds, k_ix, k_fx, v_fx, lf_all = (_pad_keys(a, lp) for a in keys)
    tq = _pick(s, 256)

    o_sb = _sb_attention(sb_q, k_sb, v_sb, heads=sbh, hd=hd, off=off, tq=tq, tk=tk)

    d_all = _cumsum_lanes(jnp.transpose(lf_all, (0, 2, 1)))
    d_q = d_all[:, :, off:off + s, None]
    o_fx = _fox_attention(fx_q, k_fx, v_fx, d_q, d_all.reshape(b, fxh, lp // LANE, LANE),
                          heads=fxh, hd=hd, off=off, tq=tq, tk=tk)

    topk = min(DSA_TOPK_MAX, n_keys // 4)
    zeros = jnp.zeros_like(k_ix)
    k_even = jnp.concatenate([k_ix, zeros], axis=2)
    k_odd = jnp.concatenate([zeros, k_ix], axis=2)
    tq_d = _pick(s, LANE)
    score, thr = _indexer(ix_q, ix_w, k_even, k_odd, off=off, topk=topk, idx_heads=ixh, tq=tq_d, tk=tk)
    o_ds = _dsa_attention(ds_q, k_ds, v_ds, score, thr, lw["bias_tiles"], kv_heads=kvh, group=dsh // kvh,
                          hd=hd, off=off, tq=tq_d)

    merged = _merge((o_sb, o_ds, o_fx), gate, lw["w_branch"])
    return merged, new_rows


def _hier_moe(x, h, g2, lw, dims):
    b, s, d = h.shape
    t = b * s
    ng, ne = dims["n_groups"], dims["n_experts"]
    info, counts = _router(h.reshape(t, d), lw["w_route"], lw["b_route"], n_groups=ng, per_group=ne // ng)
    expert = info[:, 0:TOP_K].astype(I32)
    weights = info[:, TOP_K:2 * TOP_K]
    rank = info[:, 2 * TOP_K:3 * TOP_K].astype(I32)
    counts = counts[0, ng:ng + ne].astype(I32)
    bm = 512 if t * TOP_K >= 512 * ne else 256
    padded = (counts + bm - 1) // bm * bm
    pad_end = jnp.cumsum(padded)
    pad_start = pad_end - padded
    dest = (pad_start[expert] + rank).reshape(t * TOP_K)
    n_blocks = -(-(t * TOP_K) // bm) + ne
    block_expert = jnp.minimum(
        jnp.searchsorted(pad_end, jnp.arange(n_blocks, dtype=I32) * bm, side="right"), ne - 1).astype(I32)
    n_used = (pad_end[-1:] // bm).astype(I32)
    slots = _dispatch(h.reshape(t, d), dest, n_blocks * bm)
    y_slots = _experts(slots, block_expert, n_used, lw["w_exp_gate"], lw["w_exp_up"], lw["w_exp_down"], bm=bm)
    return _combine(y_slots, dest, weights.reshape(b, s, TOP_K), x, g2)


def _trunk(x, mod, past, layer_weights, dims):
    rows = []
    for l, lw in enumerate(layer_weights):
        sh1, sc1, g1, sh2, sc2, g2 = mod[l]
        h = _norm_mod(x, lw["g_norm1"], sc1, sh1)
        layer_past = None if past is None else tuple(pc[l] for pc in past)
        merged, new = _token_mixers(h, layer_past, lw, dims)
        x = _matmul(merged, lw["w_out"], epi=_epi_residual, extras=[(x, "tile"), (g1, "batchrow")], name="out_proj")
        h = _norm_mod(x, lw["g_norm2"], sc2, sh2, out_dtype=F32)
        x = _hier_moe(x, h, g2, lw, dims)
        rows.append(new)
    return x, rows


def _prepare_layer(l, dims, w_in, q_norm_dsa, k_norm_dsa, q_norm_fox, k_norm_fox, b_forget, w_branch, w_out,
                   bias_tiles, w_route_grp, b_route_grp, w_route_exp, b_route_exp, w_exp_gate, w_exp_up,
                   w_exp_down, g_norm1, g_norm2):
    hd, sbh, dsh, kvh, ixh, ixd, fxh, d = (dims[k] for k in ("hd", "sb_heads", "dsa_heads", "kv_heads",
                                                             "idx_heads", "idx_dim", "fox_heads", "d"))
    widths = [("sb_q", sbh * hd), ("sb_k", sbh * hd), ("sb_v", sbh * hd), ("ds_q", dsh * hd), ("ds_k", kvh * hd),
              ("ds_v", kvh * hd), ("ix_q", ixh * ixd), ("ix_k", ixd), ("ix_w", ixh), ("fx_q", fxh * hd),
              ("fx_k", fxh * hd), ("fx_v", fxh * hd), ("fx_f", fxh), ("gate", 3 * d)]
    cols = {}
    start = 0
    for name, w in widths:
        cols[name] = (start, start + w)
        start += w
    wl = w_in[l]
    seg = {name: wl[:, a:bnd].astype(BF16) for name, (a, bnd) in cols.items()
           if name not in ("ix_k", "ix_w", "fx_f")}
    n_small = ixd + ixh + fxh
    small_w = -(-n_small // LANE) * LANE
    small = jnp.concatenate([wl[:, cols["ix_k"][0]:cols["ix_k"][1]], wl[:, cols["ix_w"][0]:cols["ix_w"][1]],
                             wl[:, cols["fx_f"][0]:cols["fx_f"][1]], jnp.zeros((d, small_w - n_small), F32)], axis=1)
    seg["small"] = small.astype(BF16)
    lane = jnp.arange(small_w)
    is_f = jnp.logical_and(lane >= ixd + ixh, lane < n_small)
    small_bias = jnp.zeros((small_w,), F32).at[ixd + ixh:n_small].set(b_forget[l].astype(F32))
    ng, ne = dims["n_groups"], dims["n_experts"]
    w_route = jnp.concatenate([w_route_grp[l], w_route_exp[l], jnp.zeros((d, LANE - ng - ne), F32)], axis=1)
    b_route = jnp.concatenate([b_route_grp[l], b_route_exp[l], jnp.zeros((LANE - ng - ne,), F32)]).reshape(1, LANE)
    return {
        "w_in_segments": seg,
        "small_bias": small_bias.reshape(1, small_w),
        "small_mask": is_f.astype(F32).reshape(1, small_w),
        "q_norm_dsa": q_norm_dsa[l].reshape(1, hd), "k_norm_dsa": k_norm_dsa[l].reshape(1, hd),
        "q_norm_fox": q_norm_fox[l].reshape(1, hd), "k_norm_fox": k_norm_fox[l].reshape(1, hd),
        "w_branch": w_branch[l].astype(BF16), "w_out": w_out[l].astype(BF16),
        "bias_tiles": bias_tiles,
        "w_route": w_route.astype(BF16), "b_route": b_route.astype(F32),
        "w_exp_gate": w_exp_gate[l].astype(BF16), "w_exp_up": w_exp_up[l].astype(BF16),
        "w_exp_down": w_exp_down[l].astype(BF16),
        "g_norm1": g_norm1[l], "g_norm2": g_norm2[l],
    }


def kernel(x_prompt, x_sample, c_prompt, c_sample, cache_sb_k, cache_sb_v, cache_dsa_k, cache_dsa_v, cache_dsa_kidx, cache_fox_k, cache_fox_v, cache_fox_logf, w_mod, b_mod, g_norm1, g_norm2, w_in, q_norm_dsa, k_norm_dsa, q_norm_fox, k_norm_fox, b_forget, w_branch, w_out, rel_bias, w_route_grp, b_route_grp, w_route_exp, b_route_exp, w_exp_gate, w_exp_up, w_exp_down):
    depth = w_in.shape[0]
    d = x_prompt.shape[-1]
    hd = q_norm_dsa.shape[-1]
    dims = {
        "d": d, "hd": hd,
        "sb_heads": cache_sb_k.shape[3], "dsa_heads": rel_bias.shape[1], "kv_heads": cache_dsa_k.shape[3],
        "idx_dim": cache_dsa_kidx.shape[-1], "fox_heads": cache_fox_k.shape[3],
        "n_groups": w_route_grp.shape[-1], "n_experts": w_route_exp.shape[-1],
    }
    fixed = (3 * dims["sb_heads"] * hd + dims["dsa_heads"] * hd + 2 * dims["kv_heads"] * hd + dims["idx_dim"]
             + 3 * dims["fox_heads"] * hd + dims["fox_heads"] + 3 * d)
    dims["idx_heads"] = (w_in.shape[2] - fixed) // (dims["idx_dim"] + 1)

    bias_tiles = _bias_tiles(rel_bias)
    layers = [_prepare_layer(l, dims, w_in, q_norm_dsa, k_norm_dsa, q_norm_fox, k_norm_fox, b_forget, w_branch,
                             w_out, bias_tiles, w_route_grp, b_route_grp, w_route_exp, b_route_exp, w_exp_gate,
                             w_exp_up, w_exp_down, g_norm1, g_norm2) for l in range(depth)]

    nb_p, nb_s = c_prompt.shape[0], c_sample.shape[0]
    rows = -(-(nb_p + nb_s) // 8) * 8
    c_all = jnp.concatenate([c_prompt, c_sample, jnp.zeros((rows - nb_p - nb_s, d), F32)], axis=0)[None]
    mods_p, mods_s = [], []
    for l in range(depth):
        mod = _matmul(c_all, w_mod[l], epi=_epi_bias, extras=[(b_mod[l].reshape(1, -1), "row")], silu_a=True,
                      name="adaln_mod")[0]
        mods_p.append([m[:, None, :] for m in jnp.split(mod[:nb_p], 6, axis=-1)])
        mods_s.append([m[:, None, :] for m in jnp.split(mod[nb_p:nb_p + nb_s], 6, axis=-1)])

    past = (cache_sb_k, cache_sb_v, cache_dsa_k, cache_dsa_v, cache_dsa_kidx, cache_fox_k, cache_fox_v,
            cache_fox_logf)
    y_p, rows_p = _trunk(x_prompt, mods_p, None, layers, dims)
    y_s, rows_s = _trunk(x_sample, mods_s, past, layers, dims)

    head_counts = (dims["sb_heads"], dims["sb_heads"], dims["kv_heads"], dims["kv_heads"], None,
                   dims["fox_heads"], dims["fox_heads"], None)

    def stack(rows):
        outs = []
        for idx, heads in enumerate(head_counts):
            a = jnp.stack([r[idx] for r in rows])
            if heads is not None:
                a = a.reshape(a.shape[0], a.shape[1], a.shape[2], heads, hd)
            outs.append(a)
        return outs

    return (y_p, y_s, *stack(rows_p), *stack(rows_s))
```

```python
import functools
import math

import jax
import jax.numpy as jnp
from jax import lax
from jax.experimental import pallas as pl
from jax.experimental.pallas import tpu as pltpu

F32 = jnp.float32
BF16 = jnp.bfloat16
I32 = jnp.int32

EPS = 1e-6
CHUNK = 64
DSA_TOPK_MAX = 256
MAX_DISTANCE = 128
TOP_K = 2
LANE = 128
NEG_BIG = -1e30
LOG2E = math.log2(math.e)
F32_EXP2_UNDERFLOW = -151.0
VMEM_LIMIT = 56 * 1024 * 1024


def _cparams(sem):
    return pltpu.CompilerParams(dimension_semantics=sem, vmem_limit_bytes=VMEM_LIMIT)


def _pick(n, pref):
    if n <= pref:
        return n
    t = pref
    while n % t:
        t //= 2
    return t


def _norm_mod_body(x_ref, g_ref, sc_ref, sh_ref, o_ref):
    x = x_ref[0]
    ms = jnp.mean(x * x, axis=-1, keepdims=True)
    y = x * lax.rsqrt(ms + EPS) * g_ref[...]
    o_ref[0] = (y * (1.0 + sc_ref[0]) + sh_ref[0]).astype(o_ref.dtype)


def _norm_mod(x, g, sc, sh, out_dtype=BF16):
    b, s, d = x.shape
    ts = _pick(s, 256)
    return pl.pallas_call(
        _norm_mod_body,
        grid=(b, s // ts),
        in_specs=[pl.BlockSpec((1, ts, d), lambda i, j: (i, j, 0)),
                  pl.BlockSpec((1, d), lambda i, j: (0, 0)),
                  pl.BlockSpec((1, 1, d), lambda i, j: (i, 0, 0)),
                  pl.BlockSpec((1, 1, d), lambda i, j: (i, 0, 0))],
        out_specs=pl.BlockSpec((1, ts, d), lambda i, j: (i, j, 0)),
        out_shape=jax.ShapeDtypeStruct((b, s, d), out_dtype),
        compiler_params=_cparams(("parallel", "parallel")),
        name="norm_mod",
    )(x, g.reshape(1, d), sc, sh)


def _log_sigmoid(x):
    return jnp.minimum(x, 0.0) - jnp.log1p(jnp.exp(-jnp.abs(x)))


def _epi_none(acc):
    return acc


def _epi_bias(acc, bias):
    return acc + bias


def _epi_head_rms(acc, gain):
    hd = gain.shape[-1]
    outs = []
    for c in range(acc.shape[-1] // hd):
        blk = acc[:, c * hd:(c + 1) * hd]
        ms = jnp.mean(blk * blk, axis=-1, keepdims=True)
        outs.append(blk * lax.rsqrt(ms + EPS) * gain)
    return jnp.concatenate(outs, axis=-1) if len(outs) > 1 else outs[0]


def _epi_masked_logsig(acc, bias, mask):
    return jnp.where(mask > 0.0, _log_sigmoid(acc + bias), acc)


def _epi_sigmoid(acc):
    return jax.nn.sigmoid(acc)


def _epi_residual(acc, res, gate):
    return res + gate * acc


def _mm_body(*refs, epi, n_extra, silu_a):
    a_ref, b_ref = refs[0], refs[1]
    extras = refs[2:2 + n_extra]
    o_ref = refs[2 + n_extra]
    a = a_ref[0]
    if silu_a:
        a = a.astype(F32)
        a = a * jax.nn.sigmoid(a)
    acc = jnp.dot(a.astype(BF16), b_ref[...].astype(BF16), preferred_element_type=F32)
    vals = []
    for r in extras:
        v = r[...]
        vals.append(v[0] if v.ndim == 3 else v)
    o_ref[0] = epi(acc, *vals).astype(o_ref.dtype)


def _matmul(a, b, *, epi=_epi_none, extras=(), out_dtype=F32, tm=1024, tn=512, silu_a=False, name="matmul"):
    bsz, s, k = a.shape
    n = b.shape[1]
    tm = _pick(s, tm)
    tn = _pick(n, tn)
    in_specs = [pl.BlockSpec((1, tm, k), lambda bb, i, j: (bb, i, 0)),
                pl.BlockSpec((k, tn), lambda bb, i, j: (0, j))]
    args = [a, b]
    for arr, kind in extras:
        if kind == "row":
            in_specs.append(pl.BlockSpec((1, tn), lambda bb, i, j: (0, j)))
        elif kind == "batchrow":
            in_specs.append(pl.BlockSpec((1, 1, tn), lambda bb, i, j: (bb, 0, j)))
        elif kind == "tile":
            in_specs.append(pl.BlockSpec((1, tm, tn), lambda bb, i, j: (bb, i, j)))
        else:
            in_specs.append(pl.BlockSpec(arr.shape, lambda bb, i, j, nd=arr.ndim: (0,) * nd))
        args.append(arr)
    return pl.pallas_call(
        functools.partial(_mm_body, epi=epi, n_extra=len(extras), silu_a=silu_a),
        grid=(bsz, s // tm, n // tn),
        in_specs=in_specs,
        out_specs=pl.BlockSpec((1, tm, tn), lambda bb, i, j: (bb, i, j)),
        out_shape=jax.ShapeDtypeStruct((bsz, s, n), out_dtype),
        compiler_params=_cparams(("parallel", "parallel", "parallel")),
        name=name,
    )(*args)


def _sb_body(q_ref, kt_ref, v_ref, o_ref, run_ref, acc_ref, *, tq, tk, cw, off, scale):
    i = pl.program_id(2)
    q_lo = off + i * tq
    q = (q_ref[0].astype(F32) * (scale * LOG2E)).astype(BF16)
    tri_r = lax.broadcasted_iota(I32, (2 * cw, cw), 0)
    tri_c = lax.broadcasted_iota(I32, (2 * cw, cw), 1)
    suffix = jnp.where(jnp.where(tri_r >= cw, tri_r - cw, tri_r) >= tri_c, 1.0, 0.0).astype(BF16)
    n_blocks = (q_lo + tq - 2) // tk + 1
    n_plain = q_lo // tk
    run_ref[...] = jnp.zeros_like(run_ref)
    acc_ref[...] = jnp.zeros_like(acc_ref)

    def tile(kb, masked):
        ks = pl.multiple_of(kb * tk, tk)
        kt = kt_ref[0, :, pl.ds(ks, tk)]
        v = v_ref[0, pl.ds(ks, tk), :]
        z_all = jnp.dot(q, kt, preferred_element_type=F32)
        run = jnp.max(run_ref[...], axis=1, keepdims=True)
        ws = [None] * (tk // cw)
        for j in reversed(range(tk // cw)):
            z = z_all[:, j * cw:(j + 1) * cw]
            log_keep = jnp.minimum(-z, 0.0) - jnp.log2(1.0 + jnp.exp2(jnp.minimum(z, -z)))
            if masked:
                row = lax.broadcasted_iota(I32, (tq, cw), 0)
                col = lax.broadcasted_iota(I32, (tq, cw), 1)
                mask = (ks + j * cw + col) < (q_lo + row)
                log_keep = jnp.where(mask, log_keep, 0.0)
            hi = log_keep.astype(BF16)
            lo = (log_keep - hi.astype(F32)).astype(BF16)
            incl = jnp.dot(jnp.concatenate([hi, lo], axis=1), suffix,
                           preferred_element_type=F32)
            w = jnp.exp2(z + incl + run)
            if masked:
                w = jnp.where(mask, w, 0.0)
            ws[j] = w.astype(BF16)
            run = run + jnp.sum(log_keep, axis=1, keepdims=True)
        w_all = jnp.concatenate(ws, axis=1) if len(ws) > 1 else ws[0]
        acc_ref[...] = acc_ref[...] + jnp.dot(w_all, v, preferred_element_type=F32)
        run_ref[...] = jnp.broadcast_to(run, run_ref.shape)

    def masked_step(n, carry):
        tile(n_blocks - 1 - n, True)
        return carry

    lax.fori_loop(0, n_blocks - n_plain, masked_step, 0)

    def alive():
        return jnp.max(run_ref[...]) > F32_EXP2_UNDERFLOW

    def cond(carry):
        kb, go = carry
        return jnp.logical_and(kb >= 0, go)

    def plain_step(carry):
        kb, _ = carry
        tile(kb, False)
        return kb - 1, alive()

    lax.while_loop(cond, plain_step, (n_plain - 1, alive()))
    o_ref[0] = acc_ref[...].astype(o_ref.dtype)


def _sb_attention(q, kt, v, *, heads, hd, off, tq, tk):
    b, s, _ = q.shape
    lp = v.shape[1]
    cw = min(tk, 256)
    return pl.pallas_call(
        functools.partial(_sb_body, tq=tq, tk=tk, cw=cw, off=off, scale=hd ** -0.5),
        grid=(b, heads, s // tq),
        in_specs=[pl.BlockSpec((1, tq, hd), lambda bb, h, i: (bb, i, h)),
                  pl.BlockSpec((1, hd, lp), lambda bb, h, i: (bb, h, 0)),
                  pl.BlockSpec((1, lp, hd), lambda bb, h, i: (bb, 0, h))],
        out_specs=pl.BlockSpec((1, tq, hd), lambda bb, h, i: (bb, i, h)),
        out_shape=jax.ShapeDtypeStruct(q.shape, BF16),
        scratch_shapes=[pltpu.VMEM((tq, LANE), F32), pltpu.VMEM((tq, hd), F32)],
        compiler_params=_cparams(("parallel", "parallel", "arbitrary")),
        name="sb_attention",
    )(q, kt, v)


def _cumsum_body(x_ref, o_ref):
    x = x_ref[0]
    n = x.shape[1]
    lane = lax.broadcasted_iota(I32, x.shape, 1)
    shift = 1
    while shift < n:
        x = x + jnp.where(lane >= shift, pltpu.roll(x, shift, axis=1), 0.0)
        shift *= 2
    o_ref[0] = x


def _cumsum_lanes(x):
    b, h, n = x.shape
    return pl.pallas_call(
        _cumsum_body,
        grid=(b,),
        in_specs=[pl.BlockSpec((1, h, n), lambda i: (i, 0, 0))],
        out_specs=pl.BlockSpec((1, h, n), lambda i: (i, 0, 0)),
        out_shape=jax.ShapeDtypeStruct(x.shape, F32),
        compiler_params=_cparams(("parallel",)),
        name="cumsum_logf",
    )(x)


def _online_softmax_step(z_cols, v, m_ref, l_ref, acc_ref, shift_extra=None):
    m_prev = m_ref[...]
    m_cur = z_cols[0]
    for z in z_cols[1:]:
        m_cur = jnp.maximum(m_cur, z)
    m_cur = jnp.max(m_cur, axis=1, keepdims=True)
    if shift_extra is not None:
        m_cur = m_cur + shift_extra
    m_next = jnp.maximum(m_prev, m_cur)
    shift = m_next if shift_extra is None else m_next - shift_extra
    ps = [jnp.exp2(z - shift) for z in z_cols]
    alpha = jnp.exp2(m_prev - m_next)
    l_new = alpha * l_ref[...]
    for p in ps:
        l_new = l_new + p
    l_ref[...] = l_new
    m_ref[...] = m_next
    p_all = jnp.concatenate([p.astype(BF16) for p in ps], axis=1) if len(ps) > 1 else ps[0].astype(BF16)
    acc_ref[...] = alpha * acc_ref[...] + jnp.dot(p_all, v, preferred_element_type=F32)


def _fox_body(q_ref, kt_ref, v_ref, dq_ref, dk_ref, o_ref, m_ref, l_ref, acc_ref, *, tq, tk, off, scale):
    i = pl.program_id(2)
    q_lo = off + i * tq
    q = (q_ref[0].astype(F32) * (scale * LOG2E)).astype(BF16)
    dq = dq_ref[0, 0] * LOG2E
    n_blocks = (q_lo + tq - 1) // tk + 1
    n_plain = q_lo // tk
    n_cols = tk // LANE
    m_ref[...] = jnp.full_like(m_ref, NEG_BIG)
    l_ref[...] = jnp.zeros_like(l_ref)
    acc_ref[...] = jnp.zeros_like(acc_ref)

    def step(kb, masked):
        ks = pl.multiple_of(kb * tk, tk)
        kt = kt_ref[0, :, pl.ds(ks, tk)]
        v = v_ref[0, pl.ds(ks, tk), :]
        dk = dk_ref[0, 0, :, pl.ds(ks, tk)] * LOG2E
        z = jnp.dot(q, kt, preferred_element_type=F32) - dk
        if masked:
            row = lax.broadcasted_iota(I32, (tq, tk), 0)
            col = lax.broadcasted_iota(I32, (tq, tk), 1)
            z = jnp.where((ks + col) <= (q_lo + row), z, NEG_BIG)
        _online_softmax_step([z[:, c * LANE:(c + 1) * LANE] for c in range(n_cols)], v, m_ref, l_ref, acc_ref,
                             shift_extra=dq)

    def masked_step(n, carry):
        step(n_blocks - 1 - n, True)
        return carry

    def plain_step(n, carry):
        step(n_plain - 1 - n, False)
        return carry

    lax.fori_loop(0, n_blocks - n_plain, masked_step, 0)
    lax.fori_loop(0, n_plain, plain_step, 0)
    o_ref[0] = (acc_ref[...] / jnp.sum(l_ref[...], axis=1, keepdims=True)).astype(o_ref.dtype)


def _fox_attention(q, kt, v, d_q, d_k, *, heads, hd, off, tq, tk):
    b, s, _ = q.shape
    lp = v.shape[1]
    return pl.pallas_call(
        functools.partial(_fox_body, tq=tq, tk=tk, off=off, scale=hd ** -0.5),
        grid=(b, heads, s // tq),
        in_specs=[pl.BlockSpec((1, tq, hd), lambda bb, h, i: (bb, i, h)),
                  pl.BlockSpec((1, hd, lp), lambda bb, h, i: (bb, h, 0)),
                  pl.BlockSpec((1, lp, hd), lambda bb, h, i: (bb, 0, h)),
                  pl.BlockSpec((1, 1, tq, 1), lambda bb, h, i: (bb, h, i, 0)),
                  pl.BlockSpec((1, 1, 1, lp), lambda bb, h, i: (bb, h, 0, 0))],
        out_specs=pl.BlockSpec((1, tq, hd), lambda bb, h, i: (bb, i, h)),
        out_shape=jax.ShapeDtypeStruct(q.shape, BF16),
        scratch_shapes=[pltpu.VMEM((tq, LANE), F32), pltpu.VMEM((tq, LANE), F32), pltpu.VMEM((tq, hd), F32)],
        compiler_params=_cparams(("parallel", "parallel", "arbitrary")),
        name="fox_attention",
    )(q, kt, v, d_q, d_k)


def _sortable(x):
    u = lax.bitcast_convert_type(x, I32)
    return u ^ (lax.shift_right_arithmetic(u, 31) & 0x7FFFFFFF)


def _indexer_body(qi_ref, wi_ref, ka_ref, kb_ref, sc_ref, thr_ref, key_ref, *, tq, tk, off, topk, idx_heads, w_scale):
    i = pl.program_id(1)
    q_lo = off + i * tq
    lp = ka_ref.shape[1]
    adm_end = ((q_lo + tq - 1) // CHUNK + 1) * CHUNK
    n_adm = (adm_end + tk - 1) // tk
    n_all = lp // tk
    wi = wi_ref[0] * w_scale
    qi = qi_ref[0].astype(BF16)
    pair = ka_ref.shape[2]
    w_cols = [wi[:, j:j + 1] for j in range(idx_heads)]
    q_pairs = [qi[:, p * pair:(p + 1) * pair] for p in range(idx_heads // 2)]
    row = lax.broadcasted_iota(I32, (tq, tk), 0)
    col = lax.broadcasted_iota(I32, (tq, tk), 1)
    q_chunk = (q_lo + row) // CHUNK

    def score_step(kb, _):
        ks = pl.multiple_of(kb * tk, tk)
        k_even = ka_ref[0, pl.ds(ks, tk), :].astype(BF16)
        k_odd = kb_ref[0, pl.ds(ks, tk), :].astype(BF16)
        total = jnp.zeros((tq, tk), F32)
        for p in range(idx_heads // 2):
            s0 = lax.dot_general(q_pairs[p], k_even, (((1,), (1,)), ((), ())), preferred_element_type=F32)
            s1 = lax.dot_general(q_pairs[p], k_odd, (((1,), (1,)), ((), ())), preferred_element_type=F32)
            total = total + jnp.maximum(s0, 0.0) * w_cols[2 * p] + jnp.maximum(s1, 0.0) * w_cols[2 * p + 1]
        admissible = ((ks + col) // CHUNK) <= q_chunk
        total = jnp.where(admissible, total, -jnp.inf)
        sc_ref[0, :, pl.ds(ks, tk)] = total
        key_ref[:, pl.ds(ks, tk)] = _sortable(total)
        return 0

    lax.fori_loop(0, n_adm, score_step, 0)

    def fill_step(kb, _):
        ks = pl.multiple_of(kb * tk, tk)
        sc_ref[0, :, pl.ds(ks, tk)] = jnp.full((tq, tk), -jnp.inf, F32)
        return 0

    lax.fori_loop(n_adm, n_all, fill_step, 0)

    sign = jnp.int32(-2 ** 31)

    def bit_step(n, t_bits):
        bit = lax.shift_left(jnp.int32(1), 31 - n)
        cand_bits = t_bits | bit
        cand = cand_bits ^ sign

        def count_step(kb, cnt):
            ks = pl.multiple_of(kb * tk, tk)
            keys = key_ref[:, pl.ds(ks, tk)]
            hit = jnp.where(keys >= cand, 1.0, 0.0)
            for c in range(tk // LANE):
                cnt = cnt + hit[:, c * LANE:(c + 1) * LANE]
            return cnt

        cnt = lax.fori_loop(0, n_adm, count_step, jnp.zeros((tq, LANE), F32))
        enough = jnp.sum(cnt, axis=1, keepdims=True) >= float(topk)
        return jnp.where(enough, cand_bits, t_bits)

    t_bits = lax.fori_loop(0, 32, bit_step, jnp.zeros((tq, 1), I32))
    t_key = t_bits ^ sign
    t_u = t_key ^ (lax.shift_right_arithmetic(t_key, 31) & 0x7FFFFFFF)
    thr_ref[0] = lax.bitcast_convert_type(t_u, F32)


def _indexer(qi, wi, k_even, k_odd, *, off, topk, idx_heads, tq, tk):
    b, s, _ = qi.shape
    lp = k_even.shape[1]
    idx_dim = qi.shape[2] // idx_heads
    w_scale = (idx_heads ** -0.5) * (idx_dim ** -0.5)
    return pl.pallas_call(
        functools.partial(_indexer_body, tq=tq, tk=tk, off=off, topk=topk, idx_heads=idx_heads, w_scale=w_scale),
        grid=(b, s // tq),
        in_specs=[pl.BlockSpec((1, tq, qi.shape[2]), lambda bb, i: (bb, i, 0)),
                  pl.BlockSpec((1, tq, idx_heads), lambda bb, i: (bb, i, 0)),
                  pl.BlockSpec((1, lp, k_even.shape[2]), lambda bb, i: (bb, 0, 0)),
                  pl.BlockSpec((1, lp, k_odd.shape[2]), lambda bb, i: (bb, 0, 0))],
        out_specs=[pl.BlockSpec((1, tq, lp), lambda bb, i: (bb, i, 0)),
                   pl.BlockSpec((1, tq, 1), lambda bb, i: (bb, i, 0))],
        out_shape=[jax.ShapeDtypeStruct((b, s, lp), F32), jax.ShapeDtypeStruct((b, s, 1), F32)],
        scratch_shapes=[pltpu.VMEM((tq, lp), I32)],
        compiler_params=_cparams(("parallel", "arbitrary")),
        name="dsa_indexer",
    )(qi, wi, k_even, k_odd)


def _bias_tile_body(rb_ref, o_ref, *, n_buckets, heads, n_off):
    half = n_buckets // 2
    exact = half // 2
    row = lax.broadcasted_iota(I32, (LANE, LANE), 0)
    col = lax.broadcasted_iota(I32, (LANE, LANE), 1)
    buckets = []
    for d in range(n_off):
        rel = d * LANE + row - col
        side = jnp.where(rel < 0, half, 0)
        a = jnp.abs(rel)
        far = exact + (jnp.log(jnp.maximum(a, 1).astype(F32) / exact)
                       / math.log(MAX_DISTANCE / exact) * (half - exact)).astype(I32)
        far = jnp.minimum(far, half - 1)
        buckets.append(side + jnp.where(a < exact, a, far))
    for h in range(heads):
        tiles = []
        for bucket in buckets:
            tile = jnp.zeros((LANE, LANE), F32)
            for bkt in range(n_buckets):
                tile = jnp.where(bucket == bkt, rb_ref[bkt, h], tile)
            tiles.append(tile)
        for d in range(n_off):
            o_ref[d, h] = tiles[d] - tiles[n_off - 1]


def _bias_tiles(rel_bias, n_off=3):
    n_buckets, heads = rel_bias.shape
    return pl.pallas_call(
        functools.partial(_bias_tile_body, n_buckets=n_buckets, heads=heads, n_off=n_off),
        in_specs=[pl.BlockSpec(memory_space=pltpu.SMEM)],
        out_specs=pl.BlockSpec(memory_space=pltpu.VMEM),
        out_shape=jax.ShapeDtypeStruct((n_off, heads, LANE, LANE), F32),
        name="dsa_bias_tiles",
    )(rel_bias)


def _dsa_body(q_ref, kt_ref, v_ref, sc_ref, thr_ref, bias_ref, o_ref, m_ref, l_ref, acc_ref, *, tq, tk, off, scale,
              group, hd):
    i = pl.program_id(2)
    q_lo = off + i * tq
    n_off = bias_ref.shape[0]
    adm_end = ((q_lo + tq - 1) // CHUNK + 1) * CHUNK
    n_blocks = (adm_end + tk - 1) // tk
    n_far = jnp.maximum(q_lo - LANE, 0) // tk
    n_cols = tk // LANE
    thr = thr_ref[0]
    qs = [(q_ref[0, :, g * hd:(g + 1) * hd].astype(F32) * (scale * LOG2E)).astype(BF16) for g in range(group)]
    m_ref[...] = jnp.full_like(m_ref, NEG_BIG)
    l_ref[...] = jnp.zeros_like(l_ref)
    acc_ref[...] = jnp.zeros_like(acc_ref)

    def step(kb, near):
        ks = pl.multiple_of(kb * tk, tk)
        kt = kt_ref[0, :, pl.ds(ks, tk)]
        v = v_ref[0, pl.ds(ks, tk), :]
        sc = sc_ref[0, :, pl.ds(ks, tk)]
        sel = jnp.logical_and(sc >= thr, jnp.abs(sc) < jnp.inf)
        neg = jnp.where(sel, 0.0, NEG_BIG)
        for g in range(group):
            z = jnp.dot(qs[g], kt, preferred_element_type=F32) + neg
            z_cols = [z[:, c * LANE:(c + 1) * LANE] for c in range(n_cols)]
            if near:
                for c in range(n_cols):
                    parts = []
                    for r in range(0, tq, LANE):
                        d_idx = jnp.clip((q_lo + r - ks - c * LANE) // LANE, 0, n_off - 1)
                        parts.append(bias_ref[d_idx, g, 0:min(LANE, tq - r), :])
                    bias = jnp.concatenate(parts, axis=0) if len(parts) > 1 else parts[0]
                    z_cols[c] = z_cols[c] + bias * LOG2E
            _online_softmax_step(z_cols, v, m_ref.at[g], l_ref.at[g], acc_ref.at[g])

    def far_step(kb, carry):
        step(kb, False)
        return carry

    def near_step(kb, carry):
        step(kb, True)
        return carry

    lax.fori_loop(0, n_far, far_step, 0)
    lax.fori_loop(n_far, n_blocks, near_step, 0)
    for g in range(group):
        o_ref[0, :, g * hd:(g + 1) * hd] = (acc_ref[g] / jnp.sum(l_ref[g], axis=1, keepdims=True)).astype(o_ref.dtype)


def _dsa_attention(q, kt, v, score, thr, bias_tiles, *, kv_heads, group, hd, off, tq, tk):
    b, s, _ = q.shape
    lp = v.shape[1]
    n_off = bias_tiles.shape[0]
    return pl.pallas_call(
        functools.partial(_dsa_body, tq=tq, tk=tk, off=off, scale=hd ** -0.5, group=group, hd=hd),
        grid=(b, kv_heads, s // tq),
        in_specs=[pl.BlockSpec((1, tq, group * hd), lambda bb, h, i: (bb, i, h)),
                  pl.BlockSpec((1, hd, lp), lambda bb, h, i: (bb, h, 0)),
                  pl.BlockSpec((1, lp, hd), lambda bb, h, i: (bb, 0, h)),
                  pl.BlockSpec((1, tq, lp), lambda bb, h, i: (bb, i, 0)),
                  pl.BlockSpec((1, tq, 1), lambda bb, h, i: (bb, i, 0)),
                  pl.BlockSpec((n_off, group, LANE, LANE), lambda bb, h, i: (0, h, 0, 0))],
        out_specs=pl.BlockSpec((1, tq, group * hd), lambda bb, h, i: (bb, i, h)),
        out_shape=jax.ShapeDtypeStruct(q.shape, BF16),
        scratch_shapes=[pltpu.VMEM((group, tq, LANE), F32), pltpu.VMEM((group, tq, LANE), F32),
                        pltpu.VMEM((group, tq, hd), F32)],
        compiler_params=_cparams(("parallel", "parallel", "arbitrary")),
        name="dsa_attention",
    )(q, kt, v, score, thr, bias_tiles)


def _merge_body(o0_ref, o1_ref, o2_ref, g0_ref, g1_ref, g2_ref, w_ref, out_ref):
    total = None
    for r, (o_ref, g_ref) in enumerate(((o0_ref, g0_ref), (o1_ref, g1_ref), (o2_ref, g2_ref))):
        y = jnp.dot(o_ref[0], w_ref[r], preferred_element_type=F32) * g_ref[0]
        total = y if total is None else total + y
    out_ref[0] = total.astype(out_ref.dtype)


def _merge(branches, gate, w_branch):
    b, s, width = branches[0].shape
    d = w_branch.shape[2]
    tm = _pick(s, 512)
    tn = _pick(d, 512)
    nj = d // tn
    o_spec = pl.BlockSpec((1, tm, width), lambda bb, i, j: (bb, i, 0))
    g_specs = [pl.BlockSpec((1, tm, tn), lambda bb, i, j, r=r: (bb, i, r * nj + j)) for r in range(3)]
    return pl.pallas_call(
        _merge_body,
        grid=(b, s // tm, nj),
        in_specs=[o_spec, o_spec, o_spec, *g_specs,
                  pl.BlockSpec((3, width, tn), lambda bb, i, j: (0, 0, j))],
        out_specs=pl.BlockSpec((1, tm, tn), lambda bb, i, j: (bb, i, j)),
        out_shape=jax.ShapeDtypeStruct((b, s, d), BF16),
        compiler_params=_cparams(("parallel", "parallel", "parallel")),
        name="branch_merge",
    )(*branches, gate, gate, gate, w_branch)


def _router_body(h_ref, w_ref, b_ref, info_ref, cnt_ref, run_ref, *, n_groups, per_group):
    step = pl.program_id(0)

    @pl.when(step == 0)
    def _():
        run_ref[...] = jnp.zeros_like(run_ref)

    tm = h_ref.shape[0]
    logits = jnp.dot(h_ref[...].astype(BF16), w_ref[...], preferred_element_type=F32) + b_ref[...]
    lane = lax.broadcasted_iota(I32, (tm, LANE), 1).astype(F32)
    far = float(4 * LANE)
    is_grp = lane < n_groups
    g_logit = jnp.where(is_grp, logits, -jnp.inf)
    g_max = jnp.max(g_logit, axis=1, keepdims=True)
    grp = jnp.min(jnp.where(g_logit == g_max, lane, far), axis=1, keepdims=True)
    p_grp = 1.0 / jnp.sum(jnp.where(is_grp, jnp.exp(logits - g_max), 0.0), axis=1, keepdims=True)
    e_lo = n_groups + grp * per_group
    in_grp = jnp.logical_and(lane >= e_lo, lane < e_lo + per_group)
    e_logit = jnp.where(in_grp, logits, -jnp.inf)
    m1 = jnp.max(e_logit, axis=1, keepdims=True)
    i1 = jnp.min(jnp.where(e_logit == m1, lane, far), axis=1, keepdims=True)
    e_logit2 = jnp.where(lane == i1, -jnp.inf, e_logit)
    m2 = jnp.max(e_logit2, axis=1, keepdims=True)
    i2 = jnp.min(jnp.where(jnp.logical_and(e_logit2 == m2, in_grp), lane, far), axis=1, keepdims=True)
    e2 = jnp.exp(m2 - m1)
    w1 = p_grp / (1.0 + e2)
    w2 = p_grp * e2 / (1.0 + e2)
    hit1 = lane == i1
    hit2 = lane == i2
    onehot = jnp.where(jnp.logical_or(hit1, hit2), 1.0, 0.0)
    r_i = lax.broadcasted_iota(I32, (tm, tm), 0)
    c_i = lax.broadcasted_iota(I32, (tm, tm), 1)
    earlier = jnp.where(c_i < r_i, 1.0, 0.0).astype(BF16)
    before = jnp.dot(earlier, onehot.astype(BF16), preferred_element_type=F32) + run_ref[...]
    r1 = jnp.sum(jnp.where(hit1, before, 0.0), axis=1, keepdims=True)
    r2 = jnp.sum(jnp.where(hit2, before, 0.0), axis=1, keepdims=True)
    run_ref[...] = run_ref[...] + jnp.sum(onehot, axis=0, keepdims=True)
    cnt_ref[...] = run_ref[...]
    vals = (i1 - n_groups, i2 - n_groups, w1, w2, r1, r2)
    info = jnp.zeros((tm, LANE), F32)
    for pos, val in enumerate(vals):
        info = jnp.where(lane == pos, val, info)
    info_ref[...] = info


def _router(h, w_route, b_route, *, n_groups, per_group):
    t, d = h.shape
    tm = _pick(t, 256)
    return pl.pallas_call(
        functools.partial(_router_body, n_groups=n_groups, per_group=per_group),
        grid=(t // tm,),
        in_specs=[pl.BlockSpec((tm, d), lambda i: (i, 0)),
                  pl.BlockSpec((d, LANE), lambda i: (0, 0)),
                  pl.BlockSpec((1, LANE), lambda i: (0, 0))],
        out_specs=[pl.BlockSpec((tm, LANE), lambda i: (i, 0)),
                   pl.BlockSpec((1, LANE), lambda i: (0, 0))],
        out_shape=[jax.ShapeDtypeStruct((t, LANE), F32), jax.ShapeDtypeStruct((1, LANE), F32)],
        scratch_shapes=[pltpu.VMEM((1, LANE), F32)],
        compiler_params=_cparams(("arbitrary",)),
        name="moe_router",
    )(h, w_route, b_route)


def _dispatch_body(dest_ref, h_ref, slots_in_ref, slots_ref, sem, *, tm):
    del slots_in_ref
    base = pl.program_id(0) * tm

    def copy(r, kk):
        return pltpu.make_async_copy(h_ref.at[pl.ds(r, 1)], slots_ref.at[pl.ds(dest_ref[(base + r) * TOP_K + kk], 1)], sem)

    def start(r, _):
        for kk in range(TOP_K):
            copy(r, kk).start()
        return 0

    def wait(r, _):
        for kk in range(TOP_K):
            copy(r, kk).wait()
        return 0

    lax.fori_loop(0, tm, start, 0)
    lax.fori_loop(0, tm, wait, 0)


def _dispatch(h, dest, n_slots):
    t, d = h.shape
    tm = _pick(t, 256)
    grid_spec = pltpu.PrefetchScalarGridSpec(
        num_scalar_prefetch=1,
        grid=(t // tm,),
        in_specs=[pl.BlockSpec((tm, d), lambda i, dest_ref: (i, 0)),
                  pl.BlockSpec(memory_space=pl.ANY)],
        out_specs=pl.BlockSpec(memory_space=pl.ANY),
        scratch_shapes=[pltpu.SemaphoreType.DMA(())],
    )
    return pl.pallas_call(
        functools.partial(_dispatch_body, tm=tm),
        grid_spec=grid_spec,
        out_shape=jax.ShapeDtypeStruct((n_slots, d), h.dtype),
        input_output_aliases={2: 0},
        compiler_params=_cparams(("arbitrary",)),
        name="moe_dispatch",
    )(dest, h, jnp.zeros((n_slots, d), h.dtype))


def _experts_body(be_ref, nb_ref, x_ref, wg_ref, wu_ref, wd_ref, o_ref):
    blk = pl.program_id(0)
    f = pl.program_id(1)

    @pl.when(blk < nb_ref[0])
    def _():
        x = x_ref[...].astype(BF16)
        gate = jnp.dot(x, wg_ref[0], preferred_element_type=F32)
        up = jnp.dot(x, wu_ref[0], preferred_element_type=F32)
        act = (gate * jax.nn.sigmoid(gate) * up).astype(BF16)
        part = jnp.dot(act, wd_ref[0], preferred_element_type=F32)

        @pl.when(f == 0)
        def _():
            o_ref[...] = part

        @pl.when(f > 0)
        def _():
            o_ref[...] = o_ref[...] + part

    @pl.when(jnp.logical_and(blk >= nb_ref[0], f == 0))
    def _():
        o_ref[...] = jnp.zeros_like(o_ref)


def _experts(slots, block_expert, n_used, w_gate, w_up, w_down, *, bm):
    n_slots, d = slots.shape
    ff = w_gate.shape[2]
    tf = _pick(ff, 256)
    nblk = n_slots // bm

    def x_map(i, f, be, nb):
        return (jnp.minimum(i, nb[0] - 1), 0)

    def w_in_map(i, f, be, nb):
        live = i < nb[0]
        return (be[jnp.minimum(i, nb[0] - 1)], 0, jnp.where(live, f, ff // tf - 1))

    def w_out_map(i, f, be, nb):
        live = i < nb[0]
        return (be[jnp.minimum(i, nb[0] - 1)], jnp.where(live, f, ff // tf - 1), 0)

    grid_spec = pltpu.PrefetchScalarGridSpec(
        num_scalar_prefetch=2,
        grid=(nblk, ff // tf),
        in_specs=[pl.BlockSpec((bm, d), x_map),
                  pl.BlockSpec((1, d, tf), w_in_map),
                  pl.BlockSpec((1, d, tf), w_in_map),
                  pl.BlockSpec((1, tf, d), w_out_map)],
        out_specs=pl.BlockSpec((bm, d), lambda i, f, be, nb: (i, 0)),
    )
    return pl.pallas_call(
        _experts_body,
        grid_spec=grid_spec,
        out_shape=jax.ShapeDtypeStruct((n_slots, d), F32),
        compiler_params=_cparams(("arbitrary", "arbitrary")),
        name="moe_experts",
    )(block_expert, n_used, slots, w_gate, w_up, w_down)


def _combine_body(dest_ref, y_ref, x_ref, g_ref, wt_ref, o_ref, buf_ref, sem, *, tm, per_batch):
    base = (pl.program_id(0) * per_batch + pl.program_id(1)) * tm

    def copy(r, kk):
        return pltpu.make_async_copy(y_ref.at[pl.ds(dest_ref[(base + r) * TOP_K + kk], 1)],
                                     buf_ref.at[kk, pl.ds(r, 1)], sem)

    def start(r, _):
        for kk in range(TOP_K):
            copy(r, kk).start()
        return 0

    def wait(r, _):
        for kk in range(TOP_K):
            copy(r, kk).wait()
        return 0

    lax.fori_loop(0, tm, start, 0)
    lax.fori_loop(0, tm, wait, 0)
    wt = wt_ref[0]
    f = buf_ref[0] * wt[:, 0:1] + buf_ref[1] * wt[:, 1:2]
    o_ref[0] = x_ref[0] + g_ref[0] * f


def _combine(y_slots, dest, weights, x, gate):
    b, s, d = x.shape
    tm = _pick(s, 256)
    per_batch = s // tm
    grid_spec = pltpu.PrefetchScalarGridSpec(
        num_scalar_prefetch=1,
        grid=(b, per_batch),
        in_specs=[pl.BlockSpec(memory_space=pl.ANY),
                  pl.BlockSpec((1, tm, d), lambda bb, i, dest_ref: (bb, i, 0)),
                  pl.BlockSpec((1, 1, d), lambda bb, i, dest_ref: (bb, 0, 0)),
                  pl.BlockSpec((1, tm, TOP_K), lambda bb, i, dest_ref: (bb, i, 0))],
        out_specs=pl.BlockSpec((1, tm, d), lambda bb, i, dest_ref: (bb, i, 0)),
        scratch_shapes=[pltpu.VMEM((TOP_K, tm, d), F32), pltpu.SemaphoreType.DMA(())],
    )
    return pl.pallas_call(
        functools.partial(_combine_body, tm=tm, per_batch=per_batch),
        grid_spec=grid_spec,
        out_shape=jax.ShapeDtypeStruct(x.shape, F32),
        compiler_params=_cparams(("arbitrary", "arbitrary")),
        name="moe_combine",
    )(dest, y_slots, x, gate, weights)


def _mixer_tiles(s, n_keys):
    k_base = next((t for t in (1024, 512, 256) if n_keys % t == 0), LANE)
    return {
        "lp": -(-n_keys // k_base) * k_base,
        "sb": (_pick(s, 256), min(k_base, 512)),
        "fox": (_pick(s, 512), k_base),
        "dsa": (_pick(s, 256), k_base),
        "idx": (_pick(s, LANE), min(k_base, 256)),
    }


def _pad_keys(a, lp):
    pad = lp - a.shape[1]
    if pad == 0:
        return a
    return jnp.concatenate([a, jnp.zeros((a.shape[0], pad) + a.shape[2:], a.dtype)], axis=1)


def _token_mixers(h, past, lw, dims):
    b, s, d = h.shape
    hd, sbh, dsh, kvh, ixh, ixd, fxh = (dims[k] for k in ("hd", "sb_heads", "dsa_heads", "kv_heads",
                                                          "idx_heads", "idx_dim", "fox_heads"))
    seg = lw["w_in_segments"]
    sb_q = _matmul(h, seg["sb_q"], out_dtype=BF16, name="in_sb_q")
    sb_k = _matmul(h, seg["sb_k"], name="in_sb_k")
    sb_v = _matmul(h, seg["sb_v"], name="in_sb_v")
    ds_q = _matmul(h, seg["ds_q"], epi=_epi_head_rms, extras=[(lw["q_norm_dsa"], "const")], out_dtype=BF16,
                   name="in_ds_q")
    ds_k = _matmul(h, seg["ds_k"], epi=_epi_head_rms, extras=[(lw["k_norm_dsa"], "const")], name="in_ds_k")
    ds_v = _matmul(h, seg["ds_v"], name="in_ds_v")
    ix_q = _matmul(h, seg["ix_q"], out_dtype=BF16, name="in_ix_q")
    small = _matmul(h, seg["small"], epi=_epi_masked_logsig,
                    extras=[(lw["small_bias"], "row"), (lw["small_mask"], "row")], name="in_small")
    fx_q = _matmul(h, seg["fx_q"], epi=_epi_head_rms, extras=[(lw["q_norm_fox"], "const")], out_dtype=BF16,
                   name="in_fx_q")
    fx_k = _matmul(h, seg["fx_k"], epi=_epi_head_rms, extras=[(lw["k_norm_fox"], "const")], name="in_fx_k")
    fx_v = _matmul(h, seg["fx_v"], name="in_fx_v")
    gate = _matmul(h, seg["gate"], epi=_epi_sigmoid, name="in_gate")
    ix_k = small[:, :, :ixd]
    ix_w = small[:, :, ixd:ixd + ixh]
    log_f = small[:, :, ixd + ixh:ixd + ixh + fxh]
    new_rows = (sb_k, sb_v, ds_k, ds_v, ix_k, fx_k, fx_v, log_f)

    if past is None:
        keys = new_rows
        n_keys = s
    else:
        shaped = (sb_k, sb_v, ds_k, ds_v, ix_k, fx_k, fx_v, log_f)
        keys = tuple(jnp.concatenate([pc.reshape(pc.shape[0], pc.shape[1], -1), r], axis=1)
                     for pc, r in zip(past, shaped))
        n_keys = keys[0].shape[1]
    off = n_keys - s
    tiles = _mixer_tiles(s, n_keys)
    lp = tiles["lp"]
    k_sb, v_sb, k_ds, v_ds, k_ix, k_fx, v_fx, lf_all = (_pad_keys(a, lp) for a in keys)

    def as_kt(a):
        return jnp.transpose(a.astype(BF16), (0, 2, 1))

    tq, tk = tiles["sb"]
    o_sb = _sb_attention(sb_q, as_kt(k_sb), v_sb.astype(BF16), heads=sbh, hd=hd, off=off, tq=tq, tk=tk)

    d_all = _cumsum_lanes(jnp.transpose(lf_all, (0, 2, 1)))
    d_q = d_all[:, :, off:off + s, None]
    tq, tk = tiles["fox"]
    o_fx = _fox_attention(fx_q, as_kt(k_fx), v_fx.astype(BF16), d_q, d_all.reshape(b, fxh, 1, lp),
                          heads=fxh, hd=hd, off=off, tq=tq, tk=tk)

    topk = min(DSA_TOPK_MAX, n_keys // 4)
    zeros = jnp.zeros_like(k_ix)
    k_even = jnp.concatenate([k_ix, zeros], axis=2)
    k_odd = jnp.concatenate([zeros, k_ix], axis=2)
    tq, tk = tiles["idx"]
    score, thr = _indexer(ix_q, ix_w, k_even, k_odd, off=off, topk=topk, idx_heads=ixh, tq=tq, tk=tk)
    tq, tk = tiles["dsa"]
    o_ds = _dsa_attention(ds_q, as_kt(k_ds), v_ds.astype(BF16), score, thr, lw["bias_tiles"], kv_heads=kvh,
                          group=dsh // kvh, hd=hd, off=off, tq=tq, tk=tk)

    merged = _merge((o_sb, o_ds, o_fx), gate, lw["w_branch"])
    return merged, new_rows


def _hier_moe(x, h, g2, lw, dims):
    b, s, d = h.shape
    t = b * s
    ng, ne = dims["n_groups"], dims["n_experts"]
    info, counts = _router(h.reshape(t, d), lw["w_route"], lw["b_route"], n_groups=ng, per_group=ne // ng)
    expert = info[:, 0:TOP_K].astype(I32)
    weights = info[:, TOP_K:2 * TOP_K]
    rank = info[:, 2 * TOP_K:3 * TOP_K].astype(I32)
    counts = counts[0, ng:ng + ne].astype(I32)
    bm = 512 if t * TOP_K >= 512 * ne else 256
    padded = (counts + bm - 1) // bm * bm
    pad_end = jnp.cumsum(padded)
    pad_start = pad_end - padded
    dest = (pad_start[expert] + rank).reshape(t * TOP_K)
    n_blocks = -(-(t * TOP_K) // bm) + ne
    block_expert = jnp.minimum(
        jnp.searchsorted(pad_end, jnp.arange(n_blocks, dtype=I32) * bm, side="right"), ne - 1).astype(I32)
    n_used = (pad_end[-1:] // bm).astype(I32)
    slots = _dispatch(h.reshape(t, d), dest, n_blocks * bm)
    y_slots = _experts(slots, block_expert, n_used, lw["w_exp_gate"], lw["w_exp_up"], lw["w_exp_down"], bm=bm)
    return _combine(y_slots, dest, weights.reshape(b, s, TOP_K), x, g2)


def _trunk(x, mod, past, layer_weights, dims):
    rows = []
    for l, lw in enumerate(layer_weights):
        sh1, sc1, g1, sh2, sc2, g2 = mod[l]
        h = _norm_mod(x, lw["g_norm1"], sc1, sh1)
        layer_past = None if past is None else tuple(pc[l] for pc in past)
        merged, new = _token_mixers(h, layer_past, lw, dims)
        x = _matmul(merged, lw["w_out"], epi=_epi_residual, extras=[(x, "tile"), (g1, "batchrow")], name="out_proj")
        h = _norm_mod(x, lw["g_norm2"], sc2, sh2, out_dtype=F32)
        x = _hier_moe(x, h, g2, lw, dims)
        rows.append(new)
    return x, rows


def _prepare_layer(l, dims, w_in, q_norm_dsa, k_norm_dsa, q_norm_fox, k_norm_fox, b_forget, w_branch, w_out,
                   bias_tiles, w_route_grp, b_route_grp, w_route_exp, b_route_exp, w_exp_gate, w_exp_up,
                   w_exp_down, g_norm1, g_norm2):
    hd, sbh, dsh, kvh, ixh, ixd, fxh, d = (dims[k] for k in ("hd", "sb_heads", "dsa_heads", "kv_heads",
                                                             "idx_heads", "idx_dim", "fox_heads", "d"))
    widths = [("sb_q", sbh * hd), ("sb_k", sbh * hd), ("sb_v", sbh * hd), ("ds_q", dsh * hd), ("ds_k", kvh * hd),
              ("ds_v", kvh * hd), ("ix_q", ixh * ixd), ("ix_k", ixd), ("ix_w", ixh), ("fx_q", fxh * hd),
              ("fx_k", fxh * hd), ("fx_v", fxh * hd), ("fx_f", fxh), ("gate", 3 * d)]
    cols = {}
    start = 0
    for name, w in widths:
        cols[name] = (start, start + w)
        start += w
    wl = w_in[l]
    seg = {name: wl[:, a:bnd].astype(BF16) for name, (a, bnd) in cols.items()
           if name not in ("ix_k", "ix_w", "fx_f")}
    n_small = ixd + ixh + fxh
    small_w = -(-n_small // LANE) * LANE
    small = jnp.concatenate([wl[:, cols["ix_k"][0]:cols["ix_k"][1]], wl[:, cols["ix_w"][0]:cols["ix_w"][1]],
                             wl[:, cols["fx_f"][0]:cols["fx_f"][1]], jnp.zeros((d, small_w - n_small), F32)], axis=1)
    seg["small"] = small.astype(BF16)
    lane = jnp.arange(small_w)
    is_f = jnp.logical_and(lane >= ixd + ixh, lane < n_small)
    small_bias = jnp.zeros((small_w,), F32).at[ixd + ixh:n_small].set(b_forget[l].astype(F32))
    ng, ne = dims["n_groups"], dims["n_experts"]
    w_route = jnp.concatenate([w_route_grp[l], w_route_exp[l], jnp.zeros((d, LANE - ng - ne), F32)], axis=1)
    b_route = jnp.concatenate([b_route_grp[l], b_route_exp[l], jnp.zeros((LANE - ng - ne,), F32)]).reshape(1, LANE)
    return {
        "w_in_segments": seg,
        "small_bias": small_bias.reshape(1, small_w),
        "small_mask": is_f.astype(F32).reshape(1, small_w),
        "q_norm_dsa": q_norm_dsa[l].reshape(1, hd), "k_norm_dsa": k_norm_dsa[l].reshape(1, hd),
        "q_norm_fox": q_norm_fox[l].reshape(1, hd), "k_norm_fox": k_norm_fox[l].reshape(1, hd),
        "w_branch": w_branch[l].astype(BF16), "w_out": w_out[l].astype(BF16),
        "bias_tiles": bias_tiles,
        "w_route": w_route.astype(BF16), "b_route": b_route.astype(F32),
        "w_exp_gate": w_exp_gate[l].astype(BF16), "w_exp_up": w_exp_up[l].astype(BF16),
        "w_exp_down": w_exp_down[l].astype(BF16),
        "g_norm1": g_norm1[l], "g_norm2": g_norm2[l],
    }


def kernel(x_prompt, x_sample, c_prompt, c_sample, cache_sb_k, cache_sb_v, cache_dsa_k, cache_dsa_v, cache_dsa_kidx, cache_fox_k, cache_fox_v, cache_fox_logf, w_mod, b_mod, g_norm1, g_norm2, w_in, q_norm_dsa, k_norm_dsa, q_norm_fox, k_norm_fox, b_forget, w_branch, w_out, rel_bias, w_route_grp, b_route_grp, w_route_exp, b_route_exp, w_exp_gate, w_exp_up, w_exp_down):
    depth = w_in.shape[0]
    d = x_prompt.shape[-1]
    hd = q_norm_dsa.shape[-1]
    dims = {
        "d": d, "hd": hd,
        "sb_heads": cache_sb_k.shape[3], "dsa_heads": rel_bias.shape[1], "kv_heads": cache_dsa_k.shape[3],
        "idx_dim": cache_dsa_kidx.shape[-1], "fox_heads": cache_fox_k.shape[3],
        "n_groups": w_route_grp.shape[-1], "n_experts": w_route_exp.shape[-1],
    }
    fixed = (3 * dims["sb_heads"] * hd + dims["dsa_heads"] * hd + 2 * dims["kv_heads"] * hd + dims["idx_dim"]
             + 3 * dims["fox_heads"] * hd + dims["fox_heads"] + 3 * d)
    dims["idx_heads"] = (w_in.shape[2] - fixed) // (dims["idx_dim"] + 1)

    bias_tiles = _bias_tiles(rel_bias)
    layers = [_prepare_layer(l, dims, w_in, q_norm_dsa, k_norm_dsa, q_norm_fox, k_norm_fox, b_forget, w_branch,
                             w_out, bias_tiles, w_route_grp, b_route_grp, w_route_exp, b_route_exp, w_exp_gate,
                             w_exp_up, w_exp_down, g_norm1, g_norm2) for l in range(depth)]

    nb_p, nb_s = c_prompt.shape[0], c_sample.shape[0]
    rows = -(-(nb_p + nb_s) // 8) * 8
    c_all = jnp.concatenate([c_prompt, c_sample, jnp.zeros((rows - nb_p - nb_s, d), F32)], axis=0)[None]
    mods_p, mods_s = [], []
    for l in range(depth):
        mod = _matmul(c_all, w_mod[l], epi=_epi_bias, extras=[(b_mod[l].reshape(1, -1), "row")], silu_a=True,
                      name="adaln_mod")[0]
        mods_p.append([m[:, None, :] for m in jnp.split(mod[:nb_p], 6, axis=-1)])
        mods_s.append([m[:, None, :] for m in jnp.split(mod[nb_p:nb_p + nb_s], 6, axis=-1)])

    past = (cache_sb_k, cache_sb_v, cache_dsa_k, cache_dsa_v, cache_dsa_kidx, cache_fox_k, cache_fox_v,
            cache_fox_logf)
    y_p, rows_p = _trunk(x_prompt, mods_p, None, layers, dims)
    y_s, rows_s = _trunk(x_sample, mods_s, past, layers, dims)

    head_counts = (dims["sb_heads"], dims["sb_heads"], dims["kv_heads"], dims["kv_heads"], None,
                   dims["fox_heads"], dims["fox_heads"], None)

    def stack(rows):
        outs = []
        for idx, heads in enumerate(head_counts):
            a = jnp.stack([r[idx] for r in rows])
            if heads is not None:
                a = a.reshape(a.shape[0], a.shape[1], a.shape[2], heads, hd)
            outs.append(a)
        return outs

    return (y_p, y_s, *stack(rows_p), *stack(rows_s))
```

```python
import functools
import math

import jax
import jax.numpy as jnp
from jax import lax
from jax.experimental import pallas as pl
from jax.experimental.pallas import tpu as pltpu

F32 = jnp.float32
BF16 = jnp.bfloat16
I32 = jnp.int32

EPS = 1e-6
CHUNK = 64
DSA_TOPK_MAX = 256
MAX_DISTANCE = 128
TOP_K = 2
LANE = 128
NEG_BIG = -1e30
LOG2E = math.log2(math.e)
F32_EXP2_UNDERFLOW = -151.0
VMEM_LIMIT = 56 * 1024 * 1024
MM_TM, MM_TN = 1024, 512


def _cparams(sem):
    return pltpu.CompilerParams(dimension_semantics=sem, vmem_limit_bytes=VMEM_LIMIT)


def _pick(n, pref):
    if n <= pref:
        return n
    t = pref
    while n % t:
        t //= 2
    return t


def _norm_mod_body(x_ref, g_ref, sc_ref, sh_ref, o_ref):
    x = x_ref[0]
    ms = jnp.mean(x * x, axis=-1, keepdims=True)
    y = x * lax.rsqrt(ms + EPS) * g_ref[...]
    o_ref[0] = (y * (1.0 + sc_ref[0]) + sh_ref[0]).astype(o_ref.dtype)


def _norm_mod(x, g, sc, sh, out_dtype=BF16):
    b, s, d = x.shape
    ts = _pick(s, 256)
    return pl.pallas_call(
        _norm_mod_body,
        grid=(b, s // ts),
        in_specs=[pl.BlockSpec((1, ts, d), lambda i, j: (i, j, 0)),
                  pl.BlockSpec((1, d), lambda i, j: (0, 0)),
                  pl.BlockSpec((1, 1, d), lambda i, j: (i, 0, 0)),
                  pl.BlockSpec((1, 1, d), lambda i, j: (i, 0, 0))],
        out_specs=pl.BlockSpec((1, ts, d), lambda i, j: (i, j, 0)),
        out_shape=jax.ShapeDtypeStruct((b, s, d), out_dtype),
        compiler_params=_cparams(("parallel", "parallel")),
        name="norm_mod",
    )(x, g.reshape(1, d), sc, sh)


def _log_sigmoid(x):
    return jnp.minimum(x, 0.0) - jnp.log1p(jnp.exp(-jnp.abs(x)))


def _epi_none(acc):
    return acc


def _epi_bias(acc, bias):
    return acc + bias


def _epi_head_rms(acc, gain):
    hd = gain.shape[-1]
    outs = []
    for c in range(acc.shape[-1] // hd):
        blk = acc[:, c * hd:(c + 1) * hd]
        ms = jnp.mean(blk * blk, axis=-1, keepdims=True)
        outs.append(blk * lax.rsqrt(ms + EPS) * gain)
    return jnp.concatenate(outs, axis=-1) if len(outs) > 1 else outs[0]


def _epi_masked_logsig(acc, bias, mask):
    return jnp.where(mask > 0.0, _log_sigmoid(acc + bias), acc)


def _epi_sigmoid(acc):
    return jax.nn.sigmoid(acc)


def _epi_residual(acc, res, gate):
    return res + gate * acc


def _mm_body(*refs, epi, n_extra, silu_a, aux):
    a_ref, b_ref = refs[0], refs[1]
    extras = refs[2:2 + n_extra]
    o_ref = refs[2 + n_extra]
    a = a_ref[0]
    if silu_a:
        a = a.astype(F32)
        a = a * jax.nn.sigmoid(a)
    acc = jnp.dot(a.astype(BF16), b_ref[...].astype(BF16), preferred_element_type=F32)
    vals = []
    for r in extras:
        v = r[...]
        vals.append(v[0] if v.ndim == 3 else v)
    res = epi(acc, *vals)
    o_ref[0] = res.astype(o_ref.dtype)
    if aux == "bf16":
        refs[3 + n_extra][0] = res.astype(BF16)
    elif aux == "bf16_t":
        refs[3 + n_extra][0] = res.T.astype(BF16)


def _matmul(a, b, *, epi=_epi_none, extras=(), out_dtype=F32, tm=MM_TM, tn=MM_TN, silu_a=False, aux=None,
            b_cols=None, name="matmul"):
    bsz, s, k = a.shape
    tm = _pick(s, tm)
    if b_cols is None:
        n = b.shape[1]
        tn = _pick(n, tn)
        b_spec = pl.BlockSpec((k, tn), lambda bb, i, j: (0, j))
    else:
        layer, start, n = b_cols
        tn = _pick(n, tn)
        assert start % tn == 0
        b_spec = pl.BlockSpec((None, k, tn), lambda bb, i, j: (layer, 0, start // tn + j))
    in_specs = [pl.BlockSpec((1, tm, k), lambda bb, i, j: (bb, i, 0)), b_spec]
    args = [a, b]
    for arr, kind in extras:
        if kind == "row":
            in_specs.append(pl.BlockSpec((1, tn), lambda bb, i, j: (0, j)))
        elif kind == "batchrow":
            in_specs.append(pl.BlockSpec((1, 1, tn), lambda bb, i, j: (bb, 0, j)))
        elif kind == "tile":
            in_specs.append(pl.BlockSpec((1, tm, tn), lambda bb, i, j: (bb, i, j)))
        else:
            in_specs.append(pl.BlockSpec(arr.shape, lambda bb, i, j, nd=arr.ndim: (0,) * nd))
        args.append(arr)
    out_specs = pl.BlockSpec((1, tm, tn), lambda bb, i, j: (bb, i, j))
    out_shape = jax.ShapeDtypeStruct((bsz, s, n), out_dtype)
    if aux == "bf16":
        out_specs = [out_specs, pl.BlockSpec((1, tm, tn), lambda bb, i, j: (bb, i, j))]
        out_shape = [out_shape, jax.ShapeDtypeStruct((bsz, s, n), BF16)]
    elif aux == "bf16_t":
        out_specs = [out_specs, pl.BlockSpec((1, tn, tm), lambda bb, i, j: (bb, j, i))]
        out_shape = [out_shape, jax.ShapeDtypeStruct((bsz, n, s), BF16)]
    return pl.pallas_call(
        functools.partial(_mm_body, epi=epi, n_extra=len(extras), silu_a=silu_a, aux=aux),
        grid=(bsz, s // tm, n // tn),
        in_specs=in_specs,
        out_specs=out_specs,
        out_shape=out_shape,
        compiler_params=_cparams(("parallel", "parallel", "parallel")),
        name=name,
    )(*args)


def _sb_body(q_ref, kt_ref, v_ref, o_ref, run_ref, acc_ref, *, tq, tk, cw, off, scale):
    i = pl.program_id(2)
    q_lo = off + i * tq
    q = (q_ref[0].astype(F32) * (scale * LOG2E)).astype(BF16)
    tri_r = lax.broadcasted_iota(I32, (2 * cw, cw), 0)
    tri_c = lax.broadcasted_iota(I32, (2 * cw, cw), 1)
    suffix = jnp.where(jnp.where(tri_r >= cw, tri_r - cw, tri_r) >= tri_c, 1.0, 0.0).astype(BF16)
    n_blocks = (q_lo + tq - 2) // tk + 1
    n_plain = q_lo // tk
    run_ref[...] = jnp.zeros_like(run_ref)
    acc_ref[...] = jnp.zeros_like(acc_ref)

    def tile(kb, masked):
        ks = pl.multiple_of(kb * tk, tk)
        kt = kt_ref[0, :, pl.ds(ks, tk)]
        v = v_ref[0, pl.ds(ks, tk), :]
        z_all = jnp.dot(q, kt, preferred_element_type=F32)
        run = jnp.max(run_ref[...], axis=1, keepdims=True)
        ws = [None] * (tk // cw)
        for j in reversed(range(tk // cw)):
            z = z_all[:, j * cw:(j + 1) * cw]
            log_keep = jnp.minimum(-z, 0.0) - jnp.log2(1.0 + jnp.exp2(jnp.minimum(z, -z)))
            if masked:
                row = lax.broadcasted_iota(I32, (tq, cw), 0)
                col = lax.broadcasted_iota(I32, (tq, cw), 1)
                mask = (ks + j * cw + col) < (q_lo + row)
                log_keep = jnp.where(mask, log_keep, 0.0)
            hi = log_keep.astype(BF16)
            lo = (log_keep - hi.astype(F32)).astype(BF16)
            incl = jnp.dot(jnp.concatenate([hi, lo], axis=1), suffix,
                           preferred_element_type=F32)
            w = jnp.exp2(z + incl + run)
            if masked:
                w = jnp.where(mask, w, 0.0)
            ws[j] = w.astype(BF16)
            run = run + jnp.sum(log_keep, axis=1, keepdims=True)
        w_all = jnp.concatenate(ws, axis=1) if len(ws) > 1 else ws[0]
        acc_ref[...] = acc_ref[...] + jnp.dot(w_all, v, preferred_element_type=F32)
        run_ref[...] = jnp.broadcast_to(run, run_ref.shape)

    def masked_step(n, carry):
        tile(n_blocks - 1 - n, True)
        return carry

    lax.fori_loop(0, n_blocks - n_plain, masked_step, 0)

    def alive():
        return jnp.max(run_ref[...]) > F32_EXP2_UNDERFLOW

    def cond(carry):
        kb, go = carry
        return jnp.logical_and(kb >= 0, go)

    def plain_step(carry):
        kb, _ = carry
        tile(kb, False)
        return kb - 1, alive()

    lax.while_loop(cond, plain_step, (n_plain - 1, alive()))
    o_ref[0] = acc_ref[...].astype(o_ref.dtype)


def _sb_attention(q, kt, v, *, heads, hd, off, tq, tk):
    b, s, _ = q.shape
    lp = v.shape[1]
    cw = min(tk, 256)
    return pl.pallas_call(
        functools.partial(_sb_body, tq=tq, tk=tk, cw=cw, off=off, scale=hd ** -0.5),
        grid=(b, heads, s // tq),
        in_specs=[pl.BlockSpec((1, tq, hd), lambda bb, h, i: (bb, i, h)),
                  pl.BlockSpec((1, hd, lp), lambda bb, h, i: (bb, h, 0)),
                  pl.BlockSpec((1, lp, hd), lambda bb, h, i: (bb, 0, h))],
        out_specs=pl.BlockSpec((1, tq, hd), lambda bb, h, i: (bb, i, h)),
        out_shape=jax.ShapeDtypeStruct(q.shape, BF16),
        scratch_shapes=[pltpu.VMEM((tq, LANE), F32), pltpu.VMEM((tq, hd), F32)],
        compiler_params=_cparams(("parallel", "parallel", "arbitrary")),
        name="sb_attention",
    )(q, kt, v)


def _cumsum_body(x_ref, o_ref):
    x = x_ref[0]
    n = x.shape[1]
    lane = lax.broadcasted_iota(I32, x.shape, 1)
    shift = 1
    while shift < n:
        x = x + jnp.where(lane >= shift, pltpu.roll(x, shift, axis=1), 0.0)
        shift *= 2
    o_ref[0] = x


def _cumsum_lanes(x):
    b, h, n = x.shape
    return pl.pallas_call(
        _cumsum_body,
        grid=(b,),
        in_specs=[pl.BlockSpec((1, h, n), lambda i: (i, 0, 0))],
        out_specs=pl.BlockSpec((1, h, n), lambda i: (i, 0, 0)),
        out_shape=jax.ShapeDtypeStruct(x.shape, F32),
        compiler_params=_cparams(("parallel",)),
        name="cumsum_logf",
    )(x)


def _online_softmax_step(z_cols, v, m_ref, l_ref, acc_ref, shift_extra=None):
    m_prev = m_ref[...]
    m_cur = z_cols[0]
    for z in z_cols[1:]:
        m_cur = jnp.maximum(m_cur, z)
    m_cur = jnp.max(m_cur, axis=1, keepdims=True)
    if shift_extra is not None:
        m_cur = m_cur + shift_extra
    m_next = jnp.maximum(m_prev, m_cur)
    shift = m_next if shift_extra is None else m_next - shift_extra
    ps = [jnp.exp2(z - shift) for z in z_cols]
    alpha = jnp.exp2(m_prev - m_next)
    l_new = alpha * l_ref[...]
    for p in ps:
        l_new = l_new + p
    l_ref[...] = l_new
    m_ref[...] = m_next
    p_all = jnp.concatenate([p.astype(BF16) for p in ps], axis=1) if len(ps) > 1 else ps[0].astype(BF16)
    acc_ref[...] = alpha * acc_ref[...] + jnp.dot(p_all, v, preferred_element_type=F32)


def _fox_body(q_ref, kt_ref, v_ref, dq_ref, dk_ref, o_ref, m_ref, l_ref, acc_ref, kmax_ref, *, tq, tk, off, scale):
    i = pl.program_id(2)
    q_lo = off + i * tq
    q = (q_ref[0].astype(F32) * (scale * LOG2E)).astype(BF16)
    dq = dq_ref[0, 0] * LOG2E
    n_blocks = (q_lo + tq - 1) // tk + 1
    n_plain = q_lo // tk
    n_cols = tk // LANE
    lp = kt_ref.shape[2]
    m_ref[...] = jnp.full_like(m_ref, NEG_BIG)
    l_ref[...] = jnp.zeros_like(l_ref)
    acc_ref[...] = jnp.zeros_like(acc_ref)

    @pl.when(i == 0)
    def _():
        kf = kt_ref[0].astype(F32)
        norm2 = jnp.max(jnp.sum(kf * kf, axis=0, keepdims=True), axis=1, keepdims=True)
        kmax_ref[...] = jnp.broadcast_to(jnp.sqrt(norm2), kmax_ref.shape)

    qf = q.astype(F32)
    cap = jnp.sqrt(jnp.sum(qf * qf, axis=1, keepdims=True)) * kmax_ref[0:1, 0:1] * 1.001 + 0.001 + dq

    def step(kb, masked):
        ks = pl.multiple_of(kb * tk, tk)
        kt = kt_ref[0, :, pl.ds(ks, tk)]
        v = v_ref[0, pl.ds(ks, tk), :]
        dk = dk_ref[0, 0, :, pl.ds(ks, tk)] * LOG2E
        z = jnp.dot(q, kt, preferred_element_type=F32) - dk
        if masked:
            row = lax.broadcasted_iota(I32, (tq, tk), 0)
            col = lax.broadcasted_iota(I32, (tq, tk), 1)
            z = jnp.where((ks + col) <= (q_lo + row), z, NEG_BIG)
        _online_softmax_step([z[:, c * LANE:(c + 1) * LANE] for c in range(n_cols)], v, m_ref, l_ref, acc_ref,
                             shift_extra=dq)

    def masked_step(n, carry):
        step(n_blocks - 1 - n, True)
        return carry

    lax.fori_loop(0, n_blocks - n_plain, masked_step, 0)

    def alive(kb):
        lane = lax.broadcasted_iota(I32, (1, lp), 1)
        dk_lo = jnp.min(jnp.where(lane < (kb + 1) * tk, dk_ref[0, 0] * LOG2E, jnp.inf), axis=1, keepdims=True)
        m_now = jnp.max(m_ref[...], axis=1, keepdims=True)
        return jnp.max(cap - dk_lo - m_now) > F32_EXP2_UNDERFLOW

    def cond(carry):
        kb, go = carry
        return jnp.logical_and(kb >= 0, go)

    def plain_step(carry):
        kb, _ = carry
        step(kb, False)
        return kb - 1, alive(kb - 1)

    lax.while_loop(cond, plain_step, (n_plain - 1, alive(n_plain - 1)))
    o_ref[0] = (acc_ref[...] / jnp.sum(l_ref[...], axis=1, keepdims=True)).astype(o_ref.dtype)


def _fox_attention(q, kt, v, d_q, d_k, *, heads, hd, off, tq, tk):
    b, s, _ = q.shape
    lp = v.shape[1]
    return pl.pallas_call(
        functools.partial(_fox_body, tq=tq, tk=tk, off=off, scale=hd ** -0.5),
        grid=(b, heads, s // tq),
        in_specs=[pl.BlockSpec((1, tq, hd), lambda bb, h, i: (bb, i, h)),
                  pl.BlockSpec((1, hd, lp), lambda bb, h, i: (bb, h, 0)),
                  pl.BlockSpec((1, lp, hd), lambda bb, h, i: (bb, 0, h)),
                  pl.BlockSpec((1, 1, tq, 1), lambda bb, h, i: (bb, h, i, 0)),
                  pl.BlockSpec((1, 1, 1, lp), lambda bb, h, i: (bb, h, 0, 0))],
        out_specs=pl.BlockSpec((1, tq, hd), lambda bb, h, i: (bb, i, h)),
        out_shape=jax.ShapeDtypeStruct(q.shape, BF16),
        scratch_shapes=[pltpu.VMEM((tq, LANE), F32), pltpu.VMEM((tq, LANE), F32), pltpu.VMEM((tq, hd), F32),
                        pltpu.VMEM((8, LANE), F32)],
        compiler_params=_cparams(("arbitrary", "arbitrary", "arbitrary")),
        name="fox_attention",
    )(q, kt, v, d_q, d_k)


def _sortable(x):
    u = lax.bitcast_convert_type(x, I32)
    return u ^ (lax.shift_right_arithmetic(u, 31) & 0x7FFFFFFF)


def _indexer_body(qi_ref, wi_ref, ka_ref, kb_ref, sc_ref, thr_ref, key_ref, *, tq, tk, off, topk, idx_heads, w_scale):
    i = pl.program_id(1)
    q_lo = off + i * tq
    lp = ka_ref.shape[1]
    adm_end = ((q_lo + tq - 1) // CHUNK + 1) * CHUNK
    n_adm = (adm_end + tk - 1) // tk
    n_all = lp // tk
    wi = wi_ref[0] * w_scale
    qi = qi_ref[0].astype(BF16)
    pair = ka_ref.shape[2]
    w_cols = [wi[:, j:j + 1] for j in range(idx_heads)]
    q_pairs = [qi[:, p * pair:(p + 1) * pair] for p in range(idx_heads // 2)]
    row = lax.broadcasted_iota(I32, (tq, tk), 0)
    col = lax.broadcasted_iota(I32, (tq, tk), 1)
    q_chunk = (q_lo + row) // CHUNK

    def score_step(kb, _):
        ks = pl.multiple_of(kb * tk, tk)
        k_even = ka_ref[0, pl.ds(ks, tk), :].astype(BF16)
        k_odd = kb_ref[0, pl.ds(ks, tk), :].astype(BF16)
        total = jnp.zeros((tq, tk), F32)
        for p in range(idx_heads // 2):
            s0 = lax.dot_general(q_pairs[p], k_even, (((1,), (1,)), ((), ())), preferred_element_type=F32)
            s1 = lax.dot_general(q_pairs[p], k_odd, (((1,), (1,)), ((), ())), preferred_element_type=F32)
            total = total + jnp.maximum(s0, 0.0) * w_cols[2 * p] + jnp.maximum(s1, 0.0) * w_cols[2 * p + 1]
        admissible = ((ks + col) // CHUNK) <= q_chunk
        total = jnp.where(admissible, total, -jnp.inf)
        sc_ref[0, :, pl.ds(ks, tk)] = total
        key_ref[:, pl.ds(ks, tk)] = _sortable(total)
        return 0

    lax.fori_loop(0, n_adm, score_step, 0)

    def fill_step(kb, _):
        ks = pl.multiple_of(kb * tk, tk)
        sc_ref[0, :, pl.ds(ks, tk)] = jnp.full((tq, tk), -jnp.inf, F32)
        return 0

    lax.fori_loop(n_adm, n_all, fill_step, 0)

    sign = jnp.int32(-2 ** 31)

    def bit_step(n, t_bits):
        bit = lax.shift_left(jnp.int32(1), 31 - n)
        cand_bits = t_bits | bit
        cand = cand_bits ^ sign

        def count_step(kb, cnt):
            ks = pl.multiple_of(kb * tk, tk)
            keys = key_ref[:, pl.ds(ks, tk)]
            hit = jnp.where(keys >= cand, 1.0, 0.0)
            for c in range(tk // LANE):
                cnt = cnt + hit[:, c * LANE:(c + 1) * LANE]
            return cnt

        cnt = lax.fori_loop(0, n_adm, count_step, jnp.zeros((tq, LANE), F32))
        enough = jnp.sum(cnt, axis=1, keepdims=True) >= float(topk)
        return jnp.where(enough, cand_bits, t_bits)

    t_bits = lax.fori_loop(0, 32, bit_step, jnp.zeros((tq, 1), I32))
    t_key = t_bits ^ sign
    t_u = t_key ^ (lax.shift_right_arithmetic(t_key, 31) & 0x7FFFFFFF)
    thr_ref[0] = lax.bitcast_convert_type(t_u, F32)


def _indexer(qi, wi, k_even, k_odd, *, off, topk, idx_heads, tq, tk):
    b, s, _ = qi.shape
    lp = k_even.shape[1]
    idx_dim = qi.shape[2] // idx_heads
    w_scale = (idx_heads ** -0.5) * (idx_dim ** -0.5)
    return pl.pallas_call(
        functools.partial(_indexer_body, tq=tq, tk=tk, off=off, topk=topk, idx_heads=idx_heads, w_scale=w_scale),
        grid=(b, s // tq),
        in_specs=[pl.BlockSpec((1, tq, qi.shape[2]), lambda bb, i: (bb, i, 0)),
                  pl.BlockSpec((1, tq, idx_heads), lambda bb, i: (bb, i, 0)),
                  pl.BlockSpec((1, lp, k_even.shape[2]), lambda bb, i: (bb, 0, 0)),
                  pl.BlockSpec((1, lp, k_odd.shape[2]), lambda bb, i: (bb, 0, 0))],
        out_specs=[pl.BlockSpec((1, tq, lp), lambda bb, i: (bb, i, 0)),
                   pl.BlockSpec((1, tq, 1), lambda bb, i: (bb, i, 0))],
        out_shape=[jax.ShapeDtypeStruct((b, s, lp), F32), jax.ShapeDtypeStruct((b, s, 1), F32)],
        scratch_shapes=[pltpu.VMEM((tq, lp), I32)],
        compiler_params=_cparams(("parallel", "arbitrary")),
        name="dsa_indexer",
    )(qi, wi, k_even, k_odd)


def _bias_tile_body(rb_ref, o_ref, *, n_buckets, heads, n_off):
    half = n_buckets // 2
    exact = half // 2
    row = lax.broadcasted_iota(I32, (LANE, LANE), 0)
    col = lax.broadcasted_iota(I32, (LANE, LANE), 1)
    buckets = []
    for d in range(n_off):
        rel = d * LANE + row - col
        side = jnp.where(rel < 0, half, 0)
        a = jnp.abs(rel)
        far = exact + (jnp.log(jnp.maximum(a, 1).astype(F32) / exact)
                       / math.log(MAX_DISTANCE / exact) * (half - exact)).astype(I32)
        far = jnp.minimum(far, half - 1)
        buckets.append(side + jnp.where(a < exact, a, far))
    for h in range(heads):
        tiles = []
        for bucket in buckets:
            tile = jnp.zeros((LANE, LANE), F32)
            for bkt in range(n_buckets):
                tile = jnp.where(bucket == bkt, rb_ref[bkt, h], tile)
            tiles.append(tile)
        for d in range(n_off):
            o_ref[d, h] = tiles[d] - tiles[n_off - 1]


def _bias_tiles(rel_bias, n_off=3):
    n_buckets, heads = rel_bias.shape
    return pl.pallas_call(
        functools.partial(_bias_tile_body, n_buckets=n_buckets, heads=heads, n_off=n_off),
        in_specs=[pl.BlockSpec(memory_space=pltpu.SMEM)],
        out_specs=pl.BlockSpec(memory_space=pltpu.VMEM),
        out_shape=jax.ShapeDtypeStruct((n_off, heads, LANE, LANE), F32),
        name="dsa_bias_tiles",
    )(rel_bias)


def _dsa_body(q_ref, kt_ref, v_ref, sc_ref, thr_ref, bias_ref, o_ref, m_ref, l_ref, acc_ref, *, tq, tk, off, scale,
              group, hd):
    i = pl.program_id(2)
    q_lo = off + i * tq
    n_off = bias_ref.shape[0]
    adm_end = ((q_lo + tq - 1) // CHUNK + 1) * CHUNK
    n_blocks = (adm_end + tk - 1) // tk
    n_far = jnp.maximum(q_lo - LANE, 0) // tk
    n_cols = tk // LANE
    thr = thr_ref[0]
    qs = [(q_ref[0, :, g * hd:(g + 1) * hd].astype(F32) * (scale * LOG2E)).astype(BF16) for g in range(group)]
    m_ref[...] = jnp.full_like(m_ref, NEG_BIG)
    l_ref[...] = jnp.zeros_like(l_ref)
    acc_ref[...] = jnp.zeros_like(acc_ref)

    def step(kb, near):
        ks = pl.multiple_of(kb * tk, tk)
        kt = kt_ref[0, :, pl.ds(ks, tk)]
        v = v_ref[0, pl.ds(ks, tk), :]
        sc = sc_ref[0, :, pl.ds(ks, tk)]
        sel = jnp.logical_and(sc >= thr, jnp.abs(sc) < jnp.inf)
        neg = jnp.where(sel, 0.0, NEG_BIG)
        for g in range(group):
            z = jnp.dot(qs[g], kt, preferred_element_type=F32) + neg
            z_cols = [z[:, c * LANE:(c + 1) * LANE] for c in range(n_cols)]
            if near:
                for c in range(n_cols):
                    parts = []
                    for r in range(0, tq, LANE):
                        d_idx = jnp.clip((q_lo + r - ks - c * LANE) // LANE, 0, n_off - 1)
                        parts.append(bias_ref[d_idx, g, 0:min(LANE, tq - r), :])
                    bias = jnp.concatenate(parts, axis=0) if len(parts) > 1 else parts[0]
                    z_cols[c] = z_cols[c] + bias * LOG2E
            _online_softmax_step(z_cols, v, m_ref.at[g], l_ref.at[g], acc_ref.at[g])

    def far_step(kb, carry):
        step(kb, False)
        return carry

    def near_step(kb, carry):
        step(kb, True)
        return carry

    lax.fori_loop(0, n_far, far_step, 0)
    lax.fori_loop(n_far, n_blocks, near_step, 0)
    for g in range(group):
        o_ref[0, :, g * hd:(g + 1) * hd] = (acc_ref[g] / jnp.sum(l_ref[g], axis=1, keepdims=True)).astype(o_ref.dtype)


def _dsa_attention(q, kt, v, score, thr, bias_tiles, *, kv_heads, group, hd, off, tq, tk):
    b, s, _ = q.shape
    lp = v.shape[1]
    n_off = bias_tiles.shape[0]
    return pl.pallas_call(
        functools.partial(_dsa_body, tq=tq, tk=tk, off=off, scale=hd ** -0.5, group=group, hd=hd),
        grid=(b, kv_heads, s // tq),
        in_specs=[pl.BlockSpec((1, tq, group * hd), lambda bb, h, i: (bb, i, h)),
                  pl.BlockSpec((1, hd, lp), lambda bb, h, i: (bb, h, 0)),
                  pl.BlockSpec((1, lp, hd), lambda bb, h, i: (bb, 0, h)),
                  pl.BlockSpec((1, tq, lp), lambda bb, h, i: (bb, i, 0)),
                  pl.BlockSpec((1, tq, 1), lambda bb, h, i: (bb, i, 0)),
                  pl.BlockSpec((n_off, group, LANE, LANE), lambda bb, h, i: (0, h, 0, 0))],
        out_specs=pl.BlockSpec((1, tq, group * hd), lambda bb, h, i: (bb, i, h)),
        out_shape=jax.ShapeDtypeStruct(q.shape, BF16),
        scratch_shapes=[pltpu.VMEM((group, tq, LANE), F32), pltpu.VMEM((group, tq, LANE), F32),
                        pltpu.VMEM((group, tq, hd), F32)],
        compiler_params=_cparams(("parallel", "parallel", "arbitrary")),
        name="dsa_attention",
    )(q, kt, v, score, thr, bias_tiles)


def _merge_body(o0_ref, o1_ref, o2_ref, g0_ref, g1_ref, g2_ref, w_ref, out_ref):
    total = None
    for r, (o_ref, g_ref) in enumerate(((o0_ref, g0_ref), (o1_ref, g1_ref), (o2_ref, g2_ref))):
        y = jnp.dot(o_ref[0], w_ref[r], preferred_element_type=F32) * g_ref[0]
        total = y if total is None else total + y
    out_ref[0] = total.astype(out_ref.dtype)


def _merge(branches, gate, w_branch):
    b, s, width = branches[0].shape
    d = w_branch.shape[2]
    tm = _pick(s, 512)
    tn = _pick(d, 512)
    nj = d // tn
    o_spec = pl.BlockSpec((1, tm, width), lambda bb, i, j: (bb, i, 0))
    g_specs = [pl.BlockSpec((1, tm, tn), lambda bb, i, j, r=r: (bb, i, r * nj + j)) for r in range(3)]
    return pl.pallas_call(
        _merge_body,
        grid=(b, s // tm, nj),
        in_specs=[o_spec, o_spec, o_spec, *g_specs,
                  pl.BlockSpec((3, width, tn), lambda bb, i, j: (0, 0, j))],
        out_specs=pl.BlockSpec((1, tm, tn), lambda bb, i, j: (bb, i, j)),
        out_shape=jax.ShapeDtypeStruct((b, s, d), BF16),
        compiler_params=_cparams(("parallel", "parallel", "parallel")),
        name="branch_merge",
    )(*branches, gate, gate, gate, w_branch)


def _router_body(h_ref, w_ref, b_ref, info_ref, cnt_ref, run_ref, *, n_groups, per_group):
    step = pl.program_id(0)

    @pl.when(step == 0)
    def _():
        run_ref[...] = jnp.zeros_like(run_ref)

    tm = h_ref.shape[0]
    logits = jnp.dot(h_ref[...].astype(BF16), w_ref[...], preferred_element_type=F32) + b_ref[...]
    lane = lax.broadcasted_iota(I32, (tm, LANE), 1).astype(F32)
    far = float(4 * LANE)
    is_grp = lane < n_groups
    g_logit = jnp.where(is_grp, logits, -jnp.inf)
    g_max = jnp.max(g_logit, axis=1, keepdims=True)
    grp = jnp.min(jnp.where(g_logit == g_max, lane, far), axis=1, keepdims=True)
    p_grp = 1.0 / jnp.sum(jnp.where(is_grp, jnp.exp(logits - g_max), 0.0), axis=1, keepdims=True)
    e_lo = n_groups + grp * per_group
    in_grp = jnp.logical_and(lane >= e_lo, lane < e_lo + per_group)
    e_logit = jnp.where(in_grp, logits, -jnp.inf)
    m1 = jnp.max(e_logit, axis=1, keepdims=True)
    i1 = jnp.min(jnp.where(e_logit == m1, lane, far), axis=1, keepdims=True)
    e_logit2 = jnp.where(lane == i1, -jnp.inf, e_logit)
    m2 = jnp.max(e_logit2, axis=1, keepdims=True)
    i2 = jnp.min(jnp.where(jnp.logical_and(e_logit2 == m2, in_grp), lane, far), axis=1, keepdims=True)
    e2 = jnp.exp(m2 - m1)
    w1 = p_grp / (1.0 + e2)
    w2 = p_grp * e2 / (1.0 + e2)
    hit1 = lane == i1
    hit2 = lane == i2
    onehot = jnp.where(jnp.logical_or(hit1, hit2), 1.0, 0.0)
    r_i = lax.broadcasted_iota(I32, (tm, tm), 0)
    c_i = lax.broadcasted_iota(I32, (tm, tm), 1)
    earlier = jnp.where(c_i < r_i, 1.0, 0.0).astype(BF16)
    before = jnp.dot(earlier, onehot.astype(BF16), preferred_element_type=F32) + run_ref[...]
    r1 = jnp.sum(jnp.where(hit1, before, 0.0), axis=1, keepdims=True)
    r2 = jnp.sum(jnp.where(hit2, before, 0.0), axis=1, keepdims=True)
    run_ref[...] = run_ref[...] + jnp.sum(onehot, axis=0, keepdims=True)
    cnt_ref[...] = run_ref[...]
    vals = (i1 - n_groups, i2 - n_groups, w1, w2, r1, r2)
    info = jnp.zeros((tm, LANE), F32)
    for pos, val in enumerate(vals):
        info = jnp.where(lane == pos, val, info)
    info_ref[...] = info


def _router(h, w_route, b_route, *, n_groups, per_group):
    t, d = h.shape
    tm = _pick(t, 256)
    return pl.pallas_call(
        functools.partial(_router_body, n_groups=n_groups, per_group=per_group),
        grid=(t // tm,),
        in_specs=[pl.BlockSpec((tm, d), lambda i: (i, 0)),
                  pl.BlockSpec((d, LANE), lambda i: (0, 0)),
                  pl.BlockSpec((1, LANE), lambda i: (0, 0))],
        out_specs=[pl.BlockSpec((tm, LANE), lambda i: (i, 0)),
                   pl.BlockSpec((1, LANE), lambda i: (0, 0))],
        out_shape=[jax.ShapeDtypeStruct((t, LANE), F32), jax.ShapeDtypeStruct((1, LANE), F32)],
        scratch_shapes=[pltpu.VMEM((1, LANE), F32)],
        compiler_params=_cparams(("arbitrary",)),
        name="moe_router",
    )(h, w_route, b_route)


def _dispatch_body(dest_ref, h_ref, slots_in_ref, slots_ref, sem, *, tm):
    del slots_in_ref
    base = pl.program_id(0) * tm

    def copy(r, kk):
        return pltpu.make_async_copy(h_ref.at[pl.ds(r, 1)], slots_ref.at[pl.ds(dest_ref[(base + r) * TOP_K + kk], 1)], sem)

    def start(r, _):
        for kk in range(TOP_K):
            copy(r, kk).start()
        return 0

    def wait(r, _):
        for kk in range(TOP_K):
            copy(r, kk).wait()
        return 0

    lax.fori_loop(0, tm, start, 0)
    lax.fori_loop(0, tm, wait, 0)


def _dispatch(h, dest, n_slots):
    t, d = h.shape
    tm = _pick(t, 256)
    grid_spec = pltpu.PrefetchScalarGridSpec(
        num_scalar_prefetch=1,
        grid=(t // tm,),
        in_specs=[pl.BlockSpec((tm, d), lambda i, dest_ref: (i, 0)),
                  pl.BlockSpec(memory_space=pl.ANY)],
        out_specs=pl.BlockSpec(memory_space=pl.ANY),
        scratch_shapes=[pltpu.SemaphoreType.DMA(())],
    )
    return pl.pallas_call(
        functools.partial(_dispatch_body, tm=tm),
        grid_spec=grid_spec,
        out_shape=jax.ShapeDtypeStruct((n_slots, d), h.dtype),
        input_output_aliases={2: 0},
        compiler_params=_cparams(("arbitrary",)),
        name="moe_dispatch",
    )(dest, h, jnp.zeros((n_slots, d), h.dtype))


def _experts_body(be_ref, nb_ref, x_ref, wg_ref, wu_ref, wd_ref, o_ref):
    blk = pl.program_id(0)
    f = pl.program_id(1)

    @pl.when(blk < nb_ref[0])
    def _():
        x = x_ref[...].astype(BF16)
        gate = jnp.dot(x, wg_ref[0], preferred_element_type=F32)
        up = jnp.dot(x, wu_ref[0], preferred_element_type=F32)
        act = (gate * jax.nn.sigmoid(gate) * up).astype(BF16)
        part = jnp.dot(act, wd_ref[0], preferred_element_type=F32)

        @pl.when(f == 0)
        def _():
            o_ref[...] = part

        @pl.when(f > 0)
        def _():
            o_ref[...] = o_ref[...] + part

    @pl.when(jnp.logical_and(blk >= nb_ref[0], f == 0))
    def _():
        o_ref[...] = jnp.zeros_like(o_ref)


def _experts(slots, block_expert, n_used, w_gate, w_up, w_down, *, bm):
    n_slots, d = slots.shape
    ff = w_gate.shape[2]
    tf = _pick(ff, 256)
    nblk = n_slots // bm

    def x_map(i, f, be, nb):
        return (jnp.minimum(i, nb[0] - 1), 0)

    def w_in_map(i, f, be, nb):
        live = i < nb[0]
        return (be[jnp.minimum(i, nb[0] - 1)], 0, jnp.where(live, f, ff // tf - 1))

    def w_out_map(i, f, be, nb):
        live = i < nb[0]
        return (be[jnp.minimum(i, nb[0] - 1)], jnp.where(live, f, ff // tf - 1), 0)

    grid_spec = pltpu.PrefetchScalarGridSpec(
        num_scalar_prefetch=2,
        grid=(nblk, ff // tf),
        in_specs=[pl.BlockSpec((bm, d), x_map),
                  pl.BlockSpec((1, d, tf), w_in_map),
                  pl.BlockSpec((1, d, tf), w_in_map),
                  pl.BlockSpec((1, tf, d), w_out_map)],
        out_specs=pl.BlockSpec((bm, d), lambda i, f, be, nb: (i, 0)),
    )
    return pl.pallas_call(
        _experts_body,
        grid_spec=grid_spec,
        out_shape=jax.ShapeDtypeStruct((n_slots, d), F32),
        compiler_params=_cparams(("arbitrary", "arbitrary")),
        name="moe_experts",
    )(block_expert, n_used, slots, w_gate, w_up, w_down)


def _combine_body(dest_ref, y_ref, x_ref, g_ref, wt_ref, o_ref, buf_ref, sem, *, tm, per_batch):
    base = (pl.program_id(0) * per_batch + pl.program_id(1)) * tm

    def copy(r, kk):
        return pltpu.make_async_copy(y_ref.at[pl.ds(dest_ref[(base + r) * TOP_K + kk], 1)],
                                     buf_ref.at[kk, pl.ds(r, 1)], sem)

    def start(r, _):
        for kk in range(TOP_K):
            copy(r, kk).start()
        return 0

    def wait(r, _):
        for kk in range(TOP_K):
            copy(r, kk).wait()
        return 0

    lax.fori_loop(0, tm, start, 0)
    lax.fori_loop(0, tm, wait, 0)
    wt = wt_ref[0]
    f = buf_ref[0] * wt[:, 0:1] + buf_ref[1] * wt[:, 1:2]
    o_ref[0] = x_ref[0] + g_ref[0] * f


def _combine(y_slots, dest, weights, x, gate):
    b, s, d = x.shape
    tm = _pick(s, 256)
    per_batch = s // tm
    grid_spec = pltpu.PrefetchScalarGridSpec(
        num_scalar_prefetch=1,
        grid=(b, per_batch),
        in_specs=[pl.BlockSpec(memory_space=pl.ANY),
                  pl.BlockSpec((1, tm, d), lambda bb, i, dest_ref: (bb, i, 0)),
                  pl.BlockSpec((1, 1, d), lambda bb, i, dest_ref: (bb, 0, 0)),
                  pl.BlockSpec((1, tm, TOP_K), lambda bb, i, dest_ref: (bb, i, 0))],
        out_specs=pl.BlockSpec((1, tm, d), lambda bb, i, dest_ref: (bb, i, 0)),
        scratch_shapes=[pltpu.VMEM((TOP_K, tm, d), F32), pltpu.SemaphoreType.DMA(())],
    )
    return pl.pallas_call(
        functools.partial(_combine_body, tm=tm, per_batch=per_batch),
        grid_spec=grid_spec,
        out_shape=jax.ShapeDtypeStruct(x.shape, F32),
        compiler_params=_cparams(("arbitrary", "arbitrary")),
        name="moe_combine",
    )(dest, y_slots, x, gate, weights)


def _mixer_tiles(s, n_keys):
    k_base = next((t for t in (1024, 512, 256) if n_keys % t == 0), None)
    if k_base is None:
        lp = -(-n_keys // LANE) * LANE
        whole = lp if lp <= 4096 else LANE
        return {"lp": lp, "sb": (_pick(s, 256), LANE), "fox": (_pick(s, 512), whole),
                "dsa": (_pick(s, 256), whole), "idx": (_pick(s, LANE), LANE)}
    return {
        "lp": n_keys,
        "sb": (_pick(s, 256), min(k_base, 512)),
        "fox": (_pick(s, 512), k_base),
        "dsa": (_pick(s, 256), k_base),
        "idx": (_pick(s, LANE), min(k_base, 512)),
    }


def _pad_keys(a, lp):
    pad = lp - a.shape[1]
    if pad == 0:
        return a
    return jnp.concatenate([a, jnp.zeros((a.shape[0], pad) + a.shape[2:], a.dtype)], axis=1)


def _token_mixers(h, past, lw, dims):
    b, s, d = h.shape
    hd, sbh, dsh, kvh, ixh, ixd, fxh = (dims[k] for k in ("hd", "sb_heads", "dsa_heads", "kv_heads",
                                                          "idx_heads", "idx_dim", "fox_heads"))
    seg = lw["w_in_segments"]
    n_keys = s if past is None else past[0].shape[1] + s
    off = n_keys - s
    tiles = _mixer_tiles(s, n_keys)
    lp = tiles["lp"]
    direct = past is None and lp == s

    def proj(name, **kw):
        w = seg[name]
        if isinstance(w, tuple):
            return _matmul(h, lw["w_in"], b_cols=w, name="in_" + name, **kw)
        return _matmul(h, w, name="in_" + name, **kw)

    def key_proj(name, **kw):
        out = proj(name, aux="bf16_t" if direct else None, **kw)
        return out if direct else (out, None)

    def val_proj(name):
        out = proj(name, aux="bf16" if direct else None)
        return out if direct else (out, None)

    sb_q = proj("sb_q", out_dtype=BF16)
    sb_k, kt_sb = key_proj("sb_k")
    sb_v, vb_sb = val_proj("sb_v")
    ds_q = proj("ds_q", epi=_epi_head_rms, extras=[(lw["q_norm_dsa"], "const")], out_dtype=BF16)
    ds_k, kt_ds = key_proj("ds_k", epi=_epi_head_rms, extras=[(lw["k_norm_dsa"], "const")])
    ds_v, vb_ds = val_proj("ds_v")
    ix_q = proj("ix_q", out_dtype=BF16)
    small = proj("small", epi=_epi_masked_logsig, extras=[(lw["small_bias"], "row"), (lw["small_mask"], "row")])
    fx_q = proj("fx_q", epi=_epi_head_rms, extras=[(lw["q_norm_fox"], "const")], out_dtype=BF16)
    fx_k, kt_fx = key_proj("fx_k", epi=_epi_head_rms, extras=[(lw["k_norm_fox"], "const")])
    fx_v, vb_fx = val_proj("fx_v")
    gate = proj("gate", epi=_epi_sigmoid)
    ix_k = small[:, :, :ixd]
    ix_w = small[:, :, ixd:ixd + ixh]
    log_f = small[:, :, ixd + ixh:ixd + ixh + fxh]
    new_rows = (sb_k, sb_v, ds_k, ds_v, ix_k, fx_k, fx_v, log_f)

    if past is None:
        keys = new_rows
    else:
        keys = tuple(jnp.concatenate([pc.reshape(pc.shape[0], pc.shape[1], -1), r], axis=1)
                     for pc, r in zip(past, new_rows))
    k_sb, v_sb, k_ds, v_ds, k_ix, k_fx, v_fx, lf_all = (_pad_keys(a, lp) for a in keys)
    if not direct:
        kt_sb, kt_ds, kt_fx = (jnp.transpose(a.astype(BF16), (0, 2, 1)) for a in (k_sb, k_ds, k_fx))
        vb_sb, vb_ds, vb_fx = (a.astype(BF16) for a in (v_sb, v_ds, v_fx))

    tq, tk = tiles["sb"]
    o_sb = _sb_attention(sb_q, kt_sb, vb_sb, heads=sbh, hd=hd, off=off, tq=tq, tk=tk)

    d_all = _cumsum_lanes(jnp.transpose(lf_all, (0, 2, 1)))
    d_q = d_all[:, :, off:off + s, None]
    tq, tk = tiles["fox"]
    o_fx = _fox_attention(fx_q, kt_fx, vb_fx, d_q, d_all.reshape(b, fxh, 1, lp),
                          heads=fxh, hd=hd, off=off, tq=tq, tk=tk)

    topk = min(DSA_TOPK_MAX, n_keys // 4)
    zeros = jnp.zeros_like(k_ix)
    k_even = jnp.concatenate([k_ix, zeros], axis=2)
    k_odd = jnp.concatenate([zeros, k_ix], axis=2)
    tq, tk = tiles["idx"]
    score, thr = _indexer(ix_q, ix_w, k_even, k_odd, off=off, topk=topk, idx_heads=ixh, tq=tq, tk=tk)
    tq, tk = tiles["dsa"]
    o_ds = _dsa_attention(ds_q, kt_ds, vb_ds, score, thr, lw["bias_tiles"], kv_heads=kvh,
                          group=dsh // kvh, hd=hd, off=off, tq=tq, tk=tk)

    merged = _merge((o_sb, o_ds, o_fx), gate, lw["w_branch"])
    return merged, new_rows


def _hier_moe(x, h, g2, lw, dims):
    b, s, d = h.shape
    t = b * s
    ng, ne = dims["n_groups"], dims["n_experts"]
    info, counts = _router(h.reshape(t, d), lw["w_route"], lw["b_route"], n_groups=ng, per_group=ne // ng)
    expert = info[:, 0:TOP_K].astype(I32)
    weights = info[:, TOP_K:2 * TOP_K]
    rank = info[:, 2 * TOP_K:3 * TOP_K].astype(I32)
    counts = counts[0, ng:ng + ne].astype(I32)
    bm = 512 if t * TOP_K >= 512 * ne else 256
    padded = (counts + bm - 1) // bm * bm
    pad_end = jnp.cumsum(padded)
    pad_start = pad_end - padded
    dest = (pad_start[expert] + rank).reshape(t * TOP_K)
    n_blocks = -(-(t * TOP_K) // bm) + ne
    block_expert = jnp.minimum(
        jnp.searchsorted(pad_end, jnp.arange(n_blocks, dtype=I32) * bm, side="right"), ne - 1).astype(I32)
    n_used = (pad_end[-1:] // bm).astype(I32)
    slots = _dispatch(h.reshape(t, d), dest, n_blocks * bm)
    y_slots = _experts(slots, block_expert, n_used, lw["w_exp_gate"], lw["w_exp_up"], lw["w_exp_down"], bm=bm)
    return _combine(y_slots, dest, weights.reshape(b, s, TOP_K), x, g2)


def _trunk(x, mod, past, layer_weights, dims):
    rows = []
    for l, lw in enumerate(layer_weights):
        sh1, sc1, g1, sh2, sc2, g2 = mod[l]
        h = _norm_mod(x, lw["g_norm1"], sc1, sh1)
        layer_past = None if past is None else tuple(pc[l] for pc in past)
        merged, new = _token_mixers(h, layer_past, lw, dims)
        x = _matmul(merged, lw["w_out"], epi=_epi_residual, extras=[(x, "tile"), (g1, "batchrow")], name="out_proj")
        h = _norm_mod(x, lw["g_norm2"], sc2, sh2, out_dtype=F32)
        x = _hier_moe(x, h, g2, lw, dims)
        rows.append(new)
    return x, rows


def _prepare_layer(l, dims, w_in, q_norm_dsa, k_norm_dsa, q_norm_fox, k_norm_fox, b_forget, w_branch, w_out,
                   bias_tiles, w_route_grp, b_route_grp, w_route_exp, b_route_exp, w_exp_gate, w_exp_up,
                   w_exp_down, g_norm1, g_norm2):
    hd, sbh, dsh, kvh, ixh, ixd, fxh, d = (dims[k] for k in ("hd", "sb_heads", "dsa_heads", "kv_heads",
                                                             "idx_heads", "idx_dim", "fox_heads", "d"))
    widths = [("sb_q", sbh * hd), ("sb_k", sbh * hd), ("sb_v", sbh * hd), ("ds_q", dsh * hd), ("ds_k", kvh * hd),
              ("ds_v", kvh * hd), ("ix_q", ixh * ixd), ("ix_k", ixd), ("ix_w", ixh), ("fx_q", fxh * hd),
              ("fx_k", fxh * hd), ("fx_v", fxh * hd), ("fx_f", fxh), ("gate", 3 * d)]
    cols = {}
    start = 0
    for name, w in widths:
        cols[name] = (start, start + w)
        start += w
    wl = w_in[l]
    seg = {}
    for name, (a, bnd) in cols.items():
        if name in ("ix_k", "ix_w", "fx_f"):
            continue
        width = bnd - a
        if width % LANE == 0 and a % _pick(width, MM_TN) == 0:
            seg[name] = (l, a, width)
        else:
            seg[name] = wl[:, a:bnd].astype(BF16)
    n_small = ixd + ixh + fxh
    small_w = -(-n_small // LANE) * LANE
    small = jnp.concatenate([wl[:, cols["ix_k"][0]:cols["ix_k"][1]], wl[:, cols["ix_w"][0]:cols["ix_w"][1]],
                             wl[:, cols["fx_f"][0]:cols["fx_f"][1]], jnp.zeros((d, small_w - n_small), F32)], axis=1)
    seg["small"] = small.astype(BF16)
    lane = jnp.arange(small_w)
    is_f = jnp.logical_and(lane >= ixd + ixh, lane < n_small)
    small_bias = jnp.zeros((small_w,), F32).at[ixd + ixh:n_small].set(b_forget[l].astype(F32))
    ng, ne = dims["n_groups"], dims["n_experts"]
    w_route = jnp.concatenate([w_route_grp[l], w_route_exp[l], jnp.zeros((d, LANE - ng - ne), F32)], axis=1)
    b_route = jnp.concatenate([b_route_grp[l], b_route_exp[l], jnp.zeros((LANE - ng - ne,), F32)]).reshape(1, LANE)
    return {
        "w_in": w_in,
        "w_in_segments": seg,
        "small_bias": small_bias.reshape(1, small_w),
        "small_mask": is_f.astype(F32).reshape(1, small_w),
        "q_norm_dsa": q_norm_dsa[l].reshape(1, hd), "k_norm_dsa": k_norm_dsa[l].reshape(1, hd),
        "q_norm_fox": q_norm_fox[l].reshape(1, hd), "k_norm_fox": k_norm_fox[l].reshape(1, hd),
        "w_branch": w_branch[l].astype(BF16), "w_out": w_out[l].astype(BF16),
        "bias_tiles": bias_tiles,
        "w_route": w_route.astype(BF16), "b_route": b_route.astype(F32),
        "w_exp_gate": w_exp_gate[l].astype(BF16), "w_exp_up": w_exp_up[l].astype(BF16),
        "w_exp_down": w_exp_down[l].astype(BF16),
        "g_norm1": g_norm1[l], "g_norm2": g_norm2[l],
    }


def kernel(x_prompt, x_sample, c_prompt, c_sample, cache_sb_k, cache_sb_v, cache_dsa_k, cache_dsa_v, cache_dsa_kidx, cache_fox_k, cache_fox_v, cache_fox_logf, w_mod, b_mod, g_norm1, g_norm2, w_in, q_norm_dsa, k_norm_dsa, q_norm_fox, k_norm_fox, b_forget, w_branch, w_out, rel_bias, w_route_grp, b_route_grp, w_route_exp, b_route_exp, w_exp_gate, w_exp_up, w_exp_down):
    depth = w_in.shape[0]
    d = x_prompt.shape[-1]
    hd = q_norm_dsa.shape[-1]
    dims = {
        "d": d, "hd": hd,
        "sb_heads": cache_sb_k.shape[3], "dsa_heads": rel_bias.shape[1], "kv_heads": cache_dsa_k.shape[3],
        "idx_dim": cache_dsa_kidx.shape[-1], "fox_heads": cache_fox_k.shape[3],
        "n_groups": w_route_grp.shape[-1], "n_experts": w_route_exp.shape[-1],
    }
    fixed = (3 * dims["sb_heads"] * hd + dims["dsa_heads"] * hd + 2 * dims["kv_heads"] * hd + dims["idx_dim"]
             + 3 * dims["fox_heads"] * hd + dims["fox_heads"] + 3 * d)
    dims["idx_heads"] = (w_in.shape[2] - fixed) // (dims["idx_dim"] + 1)

    bias_tiles = _bias_tiles(rel_bias)
    layers = [_prepare_layer(l, dims, w_in, q_norm_dsa, k_norm_dsa, q_norm_fox, k_norm_fox, b_forget, w_branch,
                             w_out, bias_tiles, w_route_grp, b_route_grp, w_route_exp, b_route_exp, w_exp_gate,
                             w_exp_up, w_exp_down, g_norm1, g_norm2) for l in range(depth)]

    nb_p, nb_s = c_prompt.shape[0], c_sample.shape[0]
    rows = -(-(nb_p + nb_s) // 8) * 8
    c_all = jnp.concatenate([c_prompt, c_sample, jnp.zeros((rows - nb_p - nb_s, d), F32)], axis=0)[None]
    mods_p, mods_s = [], []
    for l in range(depth):
        mod = _matmul(c_all, w_mod[l], epi=_epi_bias, extras=[(b_mod[l].reshape(1, -1), "row")], silu_a=True,
                      name="adaln_mod")[0]
        mods_p.append([m[:, None, :] for m in jnp.split(mod[:nb_p], 6, axis=-1)])
        mods_s.append([m[:, None, :] for m in jnp.split(mod[nb_p:nb_p + nb_s], 6, axis=-1)])

    past = (cache_sb_k, cache_sb_v, cache_dsa_k, cache_dsa_v, cache_dsa_kidx, cache_fox_k, cache_fox_v,
            cache_fox_logf)
    y_p, rows_p = _trunk(x_prompt, mods_p, None, layers, dims)
    y_s, rows_s = _trunk(x_sample, mods_s, past, layers, dims)

    head_counts = (dims["sb_heads"], dims["sb_heads"], dims["kv_heads"], dims["kv_heads"], None,
                   dims["fox_heads"], dims["fox_heads"], None)

    def stack(rows):
        outs = []
        for idx, heads in enumerate(head_counts):
            a = jnp.stack([r[idx] for r in rows])
            if heads is not None:
                a = a.reshape(a.shape[0], a.shape[1], a.shape[2], heads, hd)
            outs.append(a)
        return outs

    return (y_p, y_s, *stack(rows_p), *stack(rows_s))
```

```python
import functools
import math

import jax
import jax.numpy as jnp
from jax import lax
from jax.experimental import pallas as pl
from jax.experimental.pallas import tpu as pltpu

F32 = jnp.float32
BF16 = jnp.bfloat16
I32 = jnp.int32

EPS = 1e-6
CHUNK = 64
DSA_TOPK_MAX = 256
MAX_DISTANCE = 128
TOP_K = 2
LANE = 128
NEG_BIG = -1e30
LOG2E = math.log2(math.e)
F32_EXP2_UNDERFLOW = -151.0
VMEM_LIMIT = 56 * 1024 * 1024
MM_TM, MM_TN = 1024, 512


def _cparams(sem):
    return pltpu.CompilerParams(dimension_semantics=sem, vmem_limit_bytes=VMEM_LIMIT)


def _pick(n, pref):
    if n <= pref:
        return n
    t = pref
    while n % t:
        t //= 2
    return t


def _norm_mod_body(x_ref, g_ref, sc_ref, sh_ref, o_ref):
    x = x_ref[0]
    ms = jnp.mean(x * x, axis=-1, keepdims=True)
    y = x * lax.rsqrt(ms + EPS) * g_ref[...]
    o_ref[0] = (y * (1.0 + sc_ref[0]) + sh_ref[0]).astype(o_ref.dtype)


def _norm_mod(x, g, sc, sh, out_dtype=BF16):
    b, s, d = x.shape
    ts = _pick(s, 256)
    return pl.pallas_call(
        _norm_mod_body,
        grid=(b, s // ts),
        in_specs=[pl.BlockSpec((1, ts, d), lambda i, j: (i, j, 0)),
                  pl.BlockSpec((1, d), lambda i, j: (0, 0)),
                  pl.BlockSpec((1, 1, d), lambda i, j: (i, 0, 0)),
                  pl.BlockSpec((1, 1, d), lambda i, j: (i, 0, 0))],
        out_specs=pl.BlockSpec((1, ts, d), lambda i, j: (i, j, 0)),
        out_shape=jax.ShapeDtypeStruct((b, s, d), out_dtype),
        compiler_params=_cparams(("parallel", "parallel")),
        name="norm_mod",
    )(x, g.reshape(1, d), sc, sh)


def _log_sigmoid(x):
    return jnp.minimum(x, 0.0) - jnp.log1p(jnp.exp(-jnp.abs(x)))


def _epi_none(acc):
    return acc


def _epi_bias(acc, bias):
    return acc + bias


def _epi_head_rms(acc, gain):
    hd = gain.shape[-1]
    outs = []
    for c in range(acc.shape[-1] // hd):
        blk = acc[:, c * hd:(c + 1) * hd]
        ms = jnp.mean(blk * blk, axis=-1, keepdims=True)
        outs.append(blk * lax.rsqrt(ms + EPS) * gain)
    return jnp.concatenate(outs, axis=-1) if len(outs) > 1 else outs[0]


def _epi_masked_logsig(acc, bias, mask):
    return jnp.where(mask > 0.0, _log_sigmoid(acc + bias), acc)


def _epi_sigmoid(acc):
    return jax.nn.sigmoid(acc)


def _epi_residual(acc, res, gate):
    return res + gate * acc


def _mm_body(*refs, epi, n_extra, silu_a, aux, stacked):
    a_ref, b_ref = refs[0], refs[1]
    extras = refs[2:2 + n_extra]
    n_in = 2 + n_extra + (1 if stacked else 0)
    o_ref = refs[n_in]
    a = a_ref[0]
    if silu_a:
        a = a.astype(F32)
        a = a * jax.nn.sigmoid(a)
    acc = jnp.dot(a.astype(BF16), b_ref[...].astype(BF16), preferred_element_type=F32)
    vals = []
    for r in extras:
        v = r[...]
        vals.append(v[0] if v.ndim == 3 else v)
    res = epi(acc, *vals)
    if stacked:
        prev_ref = refs[n_in - 1]
        n_prev = prev_ref.shape[0]
        o_ref[0:n_prev, 0] = prev_ref[:, 0]
        o_ref[n_prev, 0] = res.astype(o_ref.dtype)
    else:
        o_ref[0] = res.astype(o_ref.dtype)
    if aux == "bf16":
        refs[n_in + 1][0] = res.astype(BF16)
    elif aux == "bf16_t":
        refs[n_in + 1][0] = res.T.astype(BF16)


def _matmul(a, b, *, epi=_epi_none, extras=(), out_dtype=F32, tm=MM_TM, tn=MM_TN, silu_a=False, aux=None,
            b_cols=None, stack_on=None, name="matmul"):
    bsz, s, k = a.shape
    tm = _pick(s, tm)
    if b_cols is None:
        n = b.shape[1]
        tn = _pick(n, tn)
        b_spec = pl.BlockSpec((k, tn), lambda bb, i, j: (0, j))
    else:
        layer, start, n = b_cols
        tn = _pick(n, tn)
        assert start % tn == 0
        b_spec = pl.BlockSpec((None, k, tn), lambda bb, i, j: (layer, 0, start // tn + j))
    in_specs = [pl.BlockSpec((1, tm, k), lambda bb, i, j: (bb, i, 0)), b_spec]
    args = [a, b]
    for arr, kind in extras:
        if kind == "row":
            in_specs.append(pl.BlockSpec((1, tn), lambda bb, i, j: (0, j)))
        elif kind == "batchrow":
            in_specs.append(pl.BlockSpec((1, 1, tn), lambda bb, i, j: (bb, 0, j)))
        elif kind == "tile":
            in_specs.append(pl.BlockSpec((1, tm, tn), lambda bb, i, j: (bb, i, j)))
        else:
            in_specs.append(pl.BlockSpec(arr.shape, lambda bb, i, j, nd=arr.ndim: (0,) * nd))
        args.append(arr)
    if stack_on is None:
        out_specs = pl.BlockSpec((1, tm, tn), lambda bb, i, j: (bb, i, j))
        out_shape = jax.ShapeDtypeStruct((bsz, s, n), out_dtype)
    else:
        n_prev = stack_on.shape[0]
        in_specs.append(pl.BlockSpec((n_prev, 1, tm, tn), lambda bb, i, j: (0, bb, i, j)))
        args.append(stack_on)
        out_specs = pl.BlockSpec((n_prev + 1, 1, tm, tn), lambda bb, i, j: (0, bb, i, j))
        out_shape = jax.ShapeDtypeStruct((n_prev + 1, bsz, s, n), out_dtype)
    if aux == "bf16":
        out_specs = [out_specs, pl.BlockSpec((1, tm, tn), lambda bb, i, j: (bb, i, j))]
        out_shape = [out_shape, jax.ShapeDtypeStruct((bsz, s, n), BF16)]
    elif aux == "bf16_t":
        out_specs = [out_specs, pl.BlockSpec((1, tn, tm), lambda bb, i, j: (bb, j, i))]
        out_shape = [out_shape, jax.ShapeDtypeStruct((bsz, n, s), BF16)]
    return pl.pallas_call(
        functools.partial(_mm_body, epi=epi, n_extra=len(extras), silu_a=silu_a, aux=aux,
                          stacked=stack_on is not None),
        grid=(bsz, s // tm, n // tn),
        in_specs=in_specs,
        out_specs=out_specs,
        out_shape=out_shape,
        compiler_params=_cparams(("parallel", "parallel", "parallel")),
        name=name,
    )(*args)


def _sb_body(q_ref, kt_ref, v_ref, o_ref, run_ref, acc_ref, *, tq, tk, cw, off, scale):
    i = pl.program_id(2)
    q_lo = off + i * tq
    q = (q_ref[0].astype(F32) * (scale * LOG2E)).astype(BF16)
    tri_r = lax.broadcasted_iota(I32, (2 * cw, cw), 0)
    tri_c = lax.broadcasted_iota(I32, (2 * cw, cw), 1)
    suffix = jnp.where(jnp.where(tri_r >= cw, tri_r - cw, tri_r) >= tri_c, 1.0, 0.0).astype(BF16)
    n_blocks = (q_lo + tq - 2) // tk + 1
    n_plain = q_lo // tk
    run_ref[...] = jnp.zeros_like(run_ref)
    acc_ref[...] = jnp.zeros_like(acc_ref)

    def tile(kb, masked):
        ks = pl.multiple_of(kb * tk, tk)
        kt = kt_ref[0, :, pl.ds(ks, tk)]
        v = v_ref[0, pl.ds(ks, tk), :]
        z_all = jnp.dot(q, kt, preferred_element_type=F32)
        run = jnp.max(run_ref[...], axis=1, keepdims=True)
        ws = [None] * (tk // cw)
        for j in reversed(range(tk // cw)):
            z = z_all[:, j * cw:(j + 1) * cw]
            log_keep = jnp.minimum(-z, 0.0) - jnp.log2(1.0 + jnp.exp2(jnp.minimum(z, -z)))
            if masked:
                row = lax.broadcasted_iota(I32, (tq, cw), 0)
                col = lax.broadcasted_iota(I32, (tq, cw), 1)
                mask = (ks + j * cw + col) < (q_lo + row)
                log_keep = jnp.where(mask, log_keep, 0.0)
            hi = log_keep.astype(BF16)
            lo = (log_keep - hi.astype(F32)).astype(BF16)
            incl = jnp.dot(jnp.concatenate([hi, lo], axis=1), suffix,
                           preferred_element_type=F32)
            w = jnp.exp2(z + incl + run)
            if masked:
                w = jnp.where(mask, w, 0.0)
            ws[j] = w.astype(BF16)
            run = run + jnp.sum(log_keep, axis=1, keepdims=True)
        w_all = jnp.concatenate(ws, axis=1) if len(ws) > 1 else ws[0]
        acc_ref[...] = acc_ref[...] + jnp.dot(w_all, v, preferred_element_type=F32)
        run_ref[...] = jnp.broadcast_to(run, run_ref.shape)

    def masked_step(n, carry):
        tile(n_blocks - 1 - n, True)
        return carry

    lax.fori_loop(0, n_blocks - n_plain, masked_step, 0)

    def alive():
        return jnp.max(run_ref[...]) > F32_EXP2_UNDERFLOW

    def cond(carry):
        kb, go = carry
        return jnp.logical_and(kb >= 0, go)

    def plain_step(carry):
        kb, _ = carry
        tile(kb, False)
        return kb - 1, alive()

    lax.while_loop(cond, plain_step, (n_plain - 1, alive()))
    o_ref[0] = acc_ref[...].astype(o_ref.dtype)


def _sb_attention(q, kt, v, *, heads, hd, off, tq, tk):
    b, s, _ = q.shape
    lp = v.shape[1]
    cw = min(tk, 256)
    return pl.pallas_call(
        functools.partial(_sb_body, tq=tq, tk=tk, cw=cw, off=off, scale=hd ** -0.5),
        grid=(b, heads, s // tq),
        in_specs=[pl.BlockSpec((1, tq, hd), lambda bb, h, i: (bb, i, h)),
                  pl.BlockSpec((1, hd, lp), lambda bb, h, i: (bb, h, 0)),
                  pl.BlockSpec((1, lp, hd), lambda bb, h, i: (bb, 0, h))],
        out_specs=pl.BlockSpec((1, tq, hd), lambda bb, h, i: (bb, i, h)),
        out_shape=jax.ShapeDtypeStruct(q.shape, BF16),
        scratch_shapes=[pltpu.VMEM((tq, LANE), F32), pltpu.VMEM((tq, hd), F32)],
        compiler_params=_cparams(("parallel", "parallel", "arbitrary")),
        name="sb_attention",
    )(q, kt, v)


def _cumsum_body(x_ref, o_ref):
    x = x_ref[0]
    n = x.shape[1]
    lane = lax.broadcasted_iota(I32, x.shape, 1)
    shift = 1
    while shift < n:
        x = x + jnp.where(lane >= shift, pltpu.roll(x, shift, axis=1), 0.0)
        shift *= 2
    o_ref[0] = x


def _cumsum_lanes(x):
    b, h, n = x.shape
    return pl.pallas_call(
        _cumsum_body,
        grid=(b,),
        in_specs=[pl.BlockSpec((1, h, n), lambda i: (i, 0, 0))],
        out_specs=pl.BlockSpec((1, h, n), lambda i: (i, 0, 0)),
        out_shape=jax.ShapeDtypeStruct(x.shape, F32),
        compiler_params=_cparams(("parallel",)),
        name="cumsum_logf",
    )(x)


def _online_softmax_step(z_cols, v, m_ref, l_ref, acc_ref, shift_extra=None):
    m_prev = m_ref[...]
    m_cur = z_cols[0]
    for z in z_cols[1:]:
        m_cur = jnp.maximum(m_cur, z)
    m_cur = jnp.max(m_cur, axis=1, keepdims=True)
    if shift_extra is not None:
        m_cur = m_cur + shift_extra
    m_next = jnp.maximum(m_prev, m_cur)
    shift = m_next if shift_extra is None else m_next - shift_extra
    ps = [jnp.exp2(z - shift) for z in z_cols]
    alpha = jnp.exp2(m_prev - m_next)
    l_new = alpha * l_ref[...]
    for p in ps:
        l_new = l_new + p
    l_ref[...] = l_new
    m_ref[...] = m_next
    p_all = jnp.concatenate([p.astype(BF16) for p in ps], axis=1) if len(ps) > 1 else ps[0].astype(BF16)
    acc_ref[...] = alpha * acc_ref[...] + jnp.dot(p_all, v, preferred_element_type=F32)


def _fox_body(q_ref, kt_ref, v_ref, dq_ref, dk_ref, o_ref, m_ref, l_ref, acc_ref, kmax_ref, *, tq, tk, off, scale):
    i = pl.program_id(2)
    q_lo = off + i * tq
    q = (q_ref[0].astype(F32) * (scale * LOG2E)).astype(BF16)
    dq = dq_ref[0, 0] * LOG2E
    n_blocks = (q_lo + tq - 1) // tk + 1
    n_plain = q_lo // tk
    n_cols = tk // LANE
    lp = kt_ref.shape[2]
    m_ref[...] = jnp.full_like(m_ref, NEG_BIG)
    l_ref[...] = jnp.zeros_like(l_ref)
    acc_ref[...] = jnp.zeros_like(acc_ref)

    @pl.when(i == 0)
    def _():
        kf = kt_ref[0].astype(F32)
        norm2 = jnp.max(jnp.sum(kf * kf, axis=0, keepdims=True), axis=1, keepdims=True)
        kmax_ref[...] = jnp.broadcast_to(jnp.sqrt(norm2), kmax_ref.shape)

    qf = q.astype(F32)
    cap = jnp.sqrt(jnp.sum(qf * qf, axis=1, keepdims=True)) * kmax_ref[0:1, 0:1] * 1.001 + 0.001 + dq

    def step(kb, masked):
        ks = pl.multiple_of(kb * tk, tk)
        kt = kt_ref[0, :, pl.ds(ks, tk)]
        v = v_ref[0, pl.ds(ks, tk), :]
        dk = dk_ref[0, 0, :, pl.ds(ks, tk)] * LOG2E
        z = jnp.dot(q, kt, preferred_element_type=F32) - dk
        if masked:
            row = lax.broadcasted_iota(I32, (tq, tk), 0)
            col = lax.broadcasted_iota(I32, (tq, tk), 1)
            z = jnp.where((ks + col) <= (q_lo + row), z, NEG_BIG)
        _online_softmax_step([z[:, c * LANE:(c + 1) * LANE] for c in range(n_cols)], v, m_ref, l_ref, acc_ref,
                             shift_extra=dq)

    def masked_step(n, carry):
        step(n_blocks - 1 - n, True)
        return carry

    lax.fori_loop(0, n_blocks - n_plain, masked_step, 0)

    def alive(kb):
        lane = lax.broadcasted_iota(I32, (1, lp), 1)
        dk_lo = jnp.min(jnp.where(lane < (kb + 1) * tk, dk_ref[0, 0] * LOG2E, jnp.inf), axis=1, keepdims=True)
        m_now = jnp.max(m_ref[...], axis=1, keepdims=True)
        return jnp.max(cap - dk_lo - m_now) > F32_EXP2_UNDERFLOW

    def cond(carry):
        kb, go = carry
        return jnp.logical_and(kb >= 0, go)

    def plain_step(carry):
        kb, _ = carry
        step(kb, False)
        return kb - 1, alive(kb - 1)

    lax.while_loop(cond, plain_step, (n_plain - 1, alive(n_plain - 1)))
    o_ref[0] = (acc_ref[...] / jnp.sum(l_ref[...], axis=1, keepdims=True)).astype(o_ref.dtype)


def _fox_attention(q, kt, v, d_q, d_k, *, heads, hd, off, tq, tk):
    b, s, _ = q.shape
    lp = v.shape[1]
    return pl.pallas_call(
        functools.partial(_fox_body, tq=tq, tk=tk, off=off, scale=hd ** -0.5),
        grid=(b, heads, s // tq),
        in_specs=[pl.BlockSpec((1, tq, hd), lambda bb, h, i: (bb, i, h)),
                  pl.BlockSpec((1, hd, lp), lambda bb, h, i: (bb, h, 0)),
                  pl.BlockSpec((1, lp, hd), lambda bb, h, i: (bb, 0, h)),
                  pl.BlockSpec((1, 1, tq, 1), lambda bb, h, i: (bb, h, i, 0)),
                  pl.BlockSpec((1, 1, 1, lp), lambda bb, h, i: (bb, h, 0, 0))],
        out_specs=pl.BlockSpec((1, tq, hd), lambda bb, h, i: (bb, i, h)),
        out_shape=jax.ShapeDtypeStruct(q.shape, BF16),
        scratch_shapes=[pltpu.VMEM((tq, LANE), F32), pltpu.VMEM((tq, LANE), F32), pltpu.VMEM((tq, hd), F32),
                        pltpu.VMEM((8, LANE), F32)],
        compiler_params=_cparams(("arbitrary", "arbitrary", "arbitrary")),
        name="fox_attention",
    )(q, kt, v, d_q, d_k)


def _sortable(x):
    u = lax.bitcast_convert_type(x, I32)
    return u ^ (lax.shift_right_arithmetic(u, 31) & 0x7FFFFFFF)


def _indexer_body(qi_ref, wi_ref, ka_ref, kb_ref, sc_ref, thr_ref, key_ref, *, tq, tk, off, topk, idx_heads, w_scale):
    i = pl.program_id(1)
    q_lo = off + i * tq
    lp = ka_ref.shape[1]
    adm_end = ((q_lo + tq - 1) // CHUNK + 1) * CHUNK
    n_adm = (adm_end + tk - 1) // tk
    n_all = lp // tk
    wi = wi_ref[0] * w_scale
    qi = qi_ref[0].astype(BF16)
    pair = ka_ref.shape[2]
    w_cols = [wi[:, j:j + 1] for j in range(idx_heads)]
    q_pairs = [qi[:, p * pair:(p + 1) * pair] for p in range(idx_heads // 2)]
    row = lax.broadcasted_iota(I32, (tq, tk), 0)
    col = lax.broadcasted_iota(I32, (tq, tk), 1)
    q_chunk = (q_lo + row) // CHUNK

    def score_step(kb, _):
        ks = pl.multiple_of(kb * tk, tk)
        k_even = ka_ref[0, pl.ds(ks, tk), :].astype(BF16)
        k_odd = kb_ref[0, pl.ds(ks, tk), :].astype(BF16)
        total = jnp.zeros((tq, tk), F32)
        for p in range(idx_heads // 2):
            s0 = lax.dot_general(q_pairs[p], k_even, (((1,), (1,)), ((), ())), preferred_element_type=F32)
            s1 = lax.dot_general(q_pairs[p], k_odd, (((1,), (1,)), ((), ())), preferred_element_type=F32)
            total = total + jnp.maximum(s0, 0.0) * w_cols[2 * p] + jnp.maximum(s1, 0.0) * w_cols[2 * p + 1]
        admissible = ((ks + col) // CHUNK) <= q_chunk
        total = jnp.where(admissible, total, -jnp.inf)
        sc_ref[0, :, pl.ds(ks, tk)] = total
        key_ref[:, pl.ds(ks, tk)] = _sortable(total)
        return 0

    lax.fori_loop(0, n_adm, score_step, 0)

    def fill_step(kb, _):
        ks = pl.multiple_of(kb * tk, tk)
        sc_ref[0, :, pl.ds(ks, tk)] = jnp.full((tq, tk), -jnp.inf, F32)
        return 0

    lax.fori_loop(n_adm, n_all, fill_step, 0)

    sign = jnp.int32(-2 ** 31)

    def bit_step(n, t_bits):
        bit = lax.shift_left(jnp.int32(1), 31 - n)
        cand_bits = t_bits | bit
        cand = cand_bits ^ sign

        def count_step(kb, cnt):
            ks = pl.multiple_of(kb * tk, tk)
            keys = key_ref[:, pl.ds(ks, tk)]
            hit = jnp.where(keys >= cand, 1.0, 0.0)
            for c in range(tk // LANE):
                cnt = cnt + hit[:, c * LANE:(c + 1) * LANE]
            return cnt

        cnt = lax.fori_loop(0, n_adm, count_step, jnp.zeros((tq, LANE), F32))
        enough = jnp.sum(cnt, axis=1, keepdims=True) >= float(topk)
        return jnp.where(enough, cand_bits, t_bits)

    t_bits = lax.fori_loop(0, 32, bit_step, jnp.zeros((tq, 1), I32))
    t_key = t_bits ^ sign
    t_u = t_key ^ (lax.shift_right_arithmetic(t_key, 31) & 0x7FFFFFFF)
    thr_ref[0] = lax.bitcast_convert_type(t_u, F32)


def _indexer(qi, wi, k_even, k_odd, *, off, topk, idx_heads, tq, tk):
    b, s, _ = qi.shape
    lp = k_even.shape[1]
    idx_dim = qi.shape[2] // idx_heads
    w_scale = (idx_heads ** -0.5) * (idx_dim ** -0.5)
    return pl.pallas_call(
        functools.partial(_indexer_body, tq=tq, tk=tk, off=off, topk=topk, idx_heads=idx_heads, w_scale=w_scale),
        grid=(b, s // tq),
        in_specs=[pl.BlockSpec((1, tq, qi.shape[2]), lambda bb, i: (bb, i, 0)),
                  pl.BlockSpec((1, tq, idx_heads), lambda bb, i: (bb, i, 0)),
                  pl.BlockSpec((1, lp, k_even.shape[2]), lambda bb, i: (bb, 0, 0)),
                  pl.BlockSpec((1, lp, k_odd.shape[2]), lambda bb, i: (bb, 0, 0))],
        out_specs=[pl.BlockSpec((1, tq, lp), lambda bb, i: (bb, i, 0)),
                   pl.BlockSpec((1, tq, 1), lambda bb, i: (bb, i, 0))],
        out_shape=[jax.ShapeDtypeStruct((b, s, lp), F32), jax.ShapeDtypeStruct((b, s, 1), F32)],
        scratch_shapes=[pltpu.VMEM((tq, lp), I32)],
        compiler_params=_cparams(("parallel", "arbitrary")),
        name="dsa_indexer",
    )(qi, wi, k_even, k_odd)


def _bias_tile_body(rb_ref, o_ref, *, n_buckets, heads, n_off):
    half = n_buckets // 2
    exact = half // 2
    row = lax.broadcasted_iota(I32, (LANE, LANE), 0)
    col = lax.broadcasted_iota(I32, (LANE, LANE), 1)
    buckets = []
    for d in range(n_off):
        rel = d * LANE + row - col
        side = jnp.where(rel < 0, half, 0)
        a = jnp.abs(rel)
        far = exact + (jnp.log(jnp.maximum(a, 1).astype(F32) / exact)
                       / math.log(MAX_DISTANCE / exact) * (half - exact)).astype(I32)
        far = jnp.minimum(far, half - 1)
        buckets.append(side + jnp.where(a < exact, a, far))
    for h in range(heads):
        tiles = []
        for bucket in buckets:
            tile = jnp.zeros((LANE, LANE), F32)
            for bkt in range(n_buckets):
                tile = jnp.where(bucket == bkt, rb_ref[bkt, h], tile)
            tiles.append(tile)
        for d in range(n_off):
            o_ref[d, h] = tiles[d] - tiles[n_off - 1]


def _bias_tiles(rel_bias, n_off=3):
    n_buckets, heads = rel_bias.shape
    return pl.pallas_call(
        functools.partial(_bias_tile_body, n_buckets=n_buckets, heads=heads, n_off=n_off),
        in_specs=[pl.BlockSpec(memory_space=pltpu.SMEM)],
        out_specs=pl.BlockSpec(memory_space=pltpu.VMEM),
        out_shape=jax.ShapeDtypeStruct((n_off, heads, LANE, LANE), F32),
        name="dsa_bias_tiles",
    )(rel_bias)


def _dsa_body(q_ref, kt_ref, v_ref, sc_ref, thr_ref, bias_ref, o_ref, m_ref, l_ref, acc_ref, *, tq, tk, off, scale,
              group, hd):
    i = pl.program_id(2)
    q_lo = off + i * tq
    n_off = bias_ref.shape[0]
    adm_end = ((q_lo + tq - 1) // CHUNK + 1) * CHUNK
    n_blocks = (adm_end + tk - 1) // tk
    n_far = jnp.maximum(q_lo - LANE, 0) // tk
    n_cols = tk // LANE
    thr = thr_ref[0]
    qs = [(q_ref[0, :, g * hd:(g + 1) * hd].astype(F32) * (scale * LOG2E)).astype(BF16) for g in range(group)]
    m_ref[...] = jnp.full_like(m_ref, NEG_BIG)
    l_ref[...] = jnp.zeros_like(l_ref)
    acc_ref[...] = jnp.zeros_like(acc_ref)

    def step(kb, near):
        ks = pl.multiple_of(kb * tk, tk)
        kt = kt_ref[0, :, pl.ds(ks, tk)]
        v = v_ref[0, pl.ds(ks, tk), :]
        sc = sc_ref[0, :, pl.ds(ks, tk)]
        sel = jnp.logical_and(sc >= thr, jnp.abs(sc) < jnp.inf)
        neg = jnp.where(sel, 0.0, NEG_BIG)
        for g in range(group):
            z = jnp.dot(qs[g], kt, preferred_element_type=F32) + neg
            z_cols = [z[:, c * LANE:(c + 1) * LANE] for c in range(n_cols)]
            if near:
                for c in range(n_cols):
                    parts = []
                    for r in range(0, tq, LANE):
                        d_idx = jnp.clip((q_lo + r - ks - c * LANE) // LANE, 0, n_off - 1)
                        parts.append(bias_ref[d_idx, g, 0:min(LANE, tq - r), :])
                    bias = jnp.concatenate(parts, axis=0) if len(parts) > 1 else parts[0]
                    z_cols[c] = z_cols[c] + bias * LOG2E
            _online_softmax_step(z_cols, v, m_ref.at[g], l_ref.at[g], acc_ref.at[g])

    def far_step(kb, carry):
        step(kb, False)
        return carry

    def near_step(kb, carry):
        step(kb, True)
        return carry

    lax.fori_loop(0, n_far, far_step, 0)
    lax.fori_loop(n_far, n_blocks, near_step, 0)
    for g in range(group):
        o_ref[0, :, g * hd:(g + 1) * hd] = (acc_ref[g] / jnp.sum(l_ref[g], axis=1, keepdims=True)).astype(o_ref.dtype)


def _dsa_attention(q, kt, v, score, thr, bias_tiles, *, kv_heads, group, hd, off, tq, tk):
    b, s, _ = q.shape
    lp = v.shape[1]
    n_off = bias_tiles.shape[0]
    return pl.pallas_call(
        functools.partial(_dsa_body, tq=tq, tk=tk, off=off, scale=hd ** -0.5, group=group, hd=hd),
        grid=(b, kv_heads, s // tq),
        in_specs=[pl.BlockSpec((1, tq, group * hd), lambda bb, h, i: (bb, i, h)),
                  pl.BlockSpec((1, hd, lp), lambda bb, h, i: (bb, h, 0)),
                  pl.BlockSpec((1, lp, hd), lambda bb, h, i: (bb, 0, h)),
                  pl.BlockSpec((1, tq, lp), lambda bb, h, i: (bb, i, 0)),
                  pl.BlockSpec((1, tq, 1), lambda bb, h, i: (bb, i, 0)),
                  pl.BlockSpec((n_off, group, LANE, LANE), lambda bb, h, i: (0, h, 0, 0))],
        out_specs=pl.BlockSpec((1, tq, group * hd), lambda bb, h, i: (bb, i, h)),
        out_shape=jax.ShapeDtypeStruct(q.shape, BF16),
        scratch_shapes=[pltpu.VMEM((group, tq, LANE), F32), pltpu.VMEM((group, tq, LANE), F32),
                        pltpu.VMEM((group, tq, hd), F32)],
        compiler_params=_cparams(("parallel", "parallel", "arbitrary")),
        name="dsa_attention",
    )(q, kt, v, score, thr, bias_tiles)


def _merge_body(o0_ref, o1_ref, o2_ref, g0_ref, g1_ref, g2_ref, w_ref, out_ref):
    total = None
    for r, (o_ref, g_ref) in enumerate(((o0_ref, g0_ref), (o1_ref, g1_ref), (o2_ref, g2_ref))):
        y = jnp.dot(o_ref[0], w_ref[r], preferred_element_type=F32) * g_ref[0]
        total = y if total is None else total + y
    out_ref[0] = total.astype(out_ref.dtype)


def _merge(branches, gate, w_branch, layer):
    b, s, width = branches[0].shape
    d = w_branch.shape[3]
    tm = _pick(s, 512)
    tn = _pick(d, 512)
    nj = d // tn
    o_spec = pl.BlockSpec((1, tm, width), lambda bb, i, j: (bb, i, 0))
    g_specs = [pl.BlockSpec((1, tm, tn), lambda bb, i, j, r=r: (bb, i, r * nj + j)) for r in range(3)]
    return pl.pallas_call(
        _merge_body,
        grid=(b, s // tm, nj),
        in_specs=[o_spec, o_spec, o_spec, *g_specs,
                  pl.BlockSpec((None, 3, width, tn), lambda bb, i, j: (layer, 0, 0, j))],
        out_specs=pl.BlockSpec((1, tm, tn), lambda bb, i, j: (bb, i, j)),
        out_shape=jax.ShapeDtypeStruct((b, s, d), BF16),
        compiler_params=_cparams(("parallel", "parallel", "parallel")),
        name="branch_merge",
    )(*branches, gate, gate, gate, w_branch)


def _router_body(h_ref, w_ref, b_ref, info_ref, cnt_ref, run_ref, *, n_groups, per_group):
    step = pl.program_id(0)

    @pl.when(step == 0)
    def _():
        run_ref[...] = jnp.zeros_like(run_ref)

    tm = h_ref.shape[0]
    logits = jnp.dot(h_ref[...].astype(BF16), w_ref[...], preferred_element_type=F32) + b_ref[...]
    lane = lax.broadcasted_iota(I32, (tm, LANE), 1).astype(F32)
    far = float(4 * LANE)
    is_grp = lane < n_groups
    g_logit = jnp.where(is_grp, logits, -jnp.inf)
    g_max = jnp.max(g_logit, axis=1, keepdims=True)
    grp = jnp.min(jnp.where(g_logit == g_max, lane, far), axis=1, keepdims=True)
    p_grp = 1.0 / jnp.sum(jnp.where(is_grp, jnp.exp(logits - g_max), 0.0), axis=1, keepdims=True)
    e_lo = n_groups + grp * per_group
    in_grp = jnp.logical_and(lane >= e_lo, lane < e_lo + per_group)
    e_logit = jnp.where(in_grp, logits, -jnp.inf)
    m1 = jnp.max(e_logit, axis=1, keepdims=True)
    i1 = jnp.min(jnp.where(e_logit == m1, lane, far), axis=1, keepdims=True)
    e_logit2 = jnp.where(lane == i1, -jnp.inf, e_logit)
    m2 = jnp.max(e_logit2, axis=1, keepdims=True)
    i2 = jnp.min(jnp.where(jnp.logical_and(e_logit2 == m2, in_grp), lane, far), axis=1, keepdims=True)
    e2 = jnp.exp(m2 - m1)
    w1 = p_grp / (1.0 + e2)
    w2 = p_grp * e2 / (1.0 + e2)
    hit1 = lane == i1
    hit2 = lane == i2
    onehot = jnp.where(jnp.logical_or(hit1, hit2), 1.0, 0.0)
    r_i = lax.broadcasted_iota(I32, (tm, tm), 0)
    c_i = lax.broadcasted_iota(I32, (tm, tm), 1)
    earlier = jnp.where(c_i < r_i, 1.0, 0.0).astype(BF16)
    before = jnp.dot(earlier, onehot.astype(BF16), preferred_element_type=F32) + run_ref[...]
    r1 = jnp.sum(jnp.where(hit1, before, 0.0), axis=1, keepdims=True)
    r2 = jnp.sum(jnp.where(hit2, before, 0.0), axis=1, keepdims=True)
    run_ref[...] = run_ref[...] + jnp.sum(onehot, axis=0, keepdims=True)
    cnt_ref[...] = run_ref[...]
    vals = (i1 - n_groups, i2 - n_groups, w1, w2, r1, r2)
    info = jnp.zeros((tm, LANE), F32)
    for pos, val in enumerate(vals):
        info = jnp.where(lane == pos, val, info)
    info_ref[...] = info


def _router(h, w_route, b_route, *, n_groups, per_group):
    t, d = h.shape
    tm = _pick(t, 256)
    return pl.pallas_call(
        functools.partial(_router_body, n_groups=n_groups, per_group=per_group),
        grid=(t // tm,),
        in_specs=[pl.BlockSpec((tm, d), lambda i: (i, 0)),
                  pl.BlockSpec((d, LANE), lambda i: (0, 0)),
                  pl.BlockSpec((1, LANE), lambda i: (0, 0))],
        out_specs=[pl.BlockSpec((tm, LANE), lambda i: (i, 0)),
                   pl.BlockSpec((1, LANE), lambda i: (0, 0))],
        out_shape=[jax.ShapeDtypeStruct((t, LANE), F32), jax.ShapeDtypeStruct((1, LANE), F32)],
        scratch_shapes=[pltpu.VMEM((1, LANE), F32)],
        compiler_params=_cparams(("arbitrary",)),
        name="moe_router",
    )(h, w_route, b_route)


def _dispatch_body(dest_ref, h_ref, slots_in_ref, slots_ref, sem, *, tm):
    del slots_in_ref
    base = pl.program_id(0) * tm

    def copy(r, kk):
        return pltpu.make_async_copy(h_ref.at[pl.ds(r, 1)], slots_ref.at[pl.ds(dest_ref[(base + r) * TOP_K + kk], 1)], sem)

    def start(r, _):
        for kk in range(TOP_K):
            copy(r, kk).start()
        return 0

    def wait(r, _):
        for kk in range(TOP_K):
            copy(r, kk).wait()
        return 0

    lax.fori_loop(0, tm, start, 0)
    lax.fori_loop(0, tm, wait, 0)


def _dispatch(h, dest, n_slots):
    t, d = h.shape
    tm = _pick(t, 256)
    grid_spec = pltpu.PrefetchScalarGridSpec(
        num_scalar_prefetch=1,
        grid=(t // tm,),
        in_specs=[pl.BlockSpec((tm, d), lambda i, dest_ref: (i, 0)),
                  pl.BlockSpec(memory_space=pl.ANY)],
        out_specs=pl.BlockSpec(memory_space=pl.ANY),
        scratch_shapes=[pltpu.SemaphoreType.DMA(())],
    )
    return pl.pallas_call(
        functools.partial(_dispatch_body, tm=tm),
        grid_spec=grid_spec,
        out_shape=jax.ShapeDtypeStruct((n_slots, d), h.dtype),
        input_output_aliases={2: 0},
        compiler_params=_cparams(("arbitrary",)),
        name="moe_dispatch",
    )(dest, h, jnp.zeros((n_slots, d), h.dtype))


def _experts_body(be_ref, nb_ref, x_ref, wg_ref, wu_ref, wd_ref, o_ref):
    blk = pl.program_id(0)
    f = pl.program_id(1)

    @pl.when(blk < nb_ref[0])
    def _():
        x = x_ref[...].astype(BF16)
        gate = jnp.dot(x, wg_ref[0], preferred_element_type=F32)
        up = jnp.dot(x, wu_ref[0], preferred_element_type=F32)
        act = (gate * jax.nn.sigmoid(gate) * up).astype(BF16)
        part = jnp.dot(act, wd_ref[0], preferred_element_type=F32)

        @pl.when(f == 0)
        def _():
            o_ref[...] = part

        @pl.when(f > 0)
        def _():
            o_ref[...] = o_ref[...] + part

    @pl.when(jnp.logical_and(blk >= nb_ref[0], f == 0))
    def _():
        o_ref[...] = jnp.zeros_like(o_ref)


def _experts(slots, block_expert, n_used, w_gate, w_up, w_down, layer, *, bm):
    n_slots, d = slots.shape
    ff = w_gate.shape[3]
    tf = _pick(ff, 256)
    nblk = n_slots // bm

    def x_map(i, f, be, nb):
        return (jnp.minimum(i, nb[0] - 1), 0)

    def w_in_map(i, f, be, nb):
        live = i < nb[0]
        return (layer, be[jnp.minimum(i, nb[0] - 1)], 0, jnp.where(live, f, ff // tf - 1))

    def w_out_map(i, f, be, nb):
        live = i < nb[0]
        return (layer, be[jnp.minimum(i, nb[0] - 1)], jnp.where(live, f, ff // tf - 1), 0)

    grid_spec = pltpu.PrefetchScalarGridSpec(
        num_scalar_prefetch=2,
        grid=(nblk, ff // tf),
        in_specs=[pl.BlockSpec((bm, d), x_map),
                  pl.BlockSpec((None, 1, d, tf), w_in_map),
                  pl.BlockSpec((None, 1, d, tf), w_in_map),
                  pl.BlockSpec((None, 1, tf, d), w_out_map)],
        out_specs=pl.BlockSpec((bm, d), lambda i, f, be, nb: (i, 0)),
    )
    return pl.pallas_call(
        _experts_body,
        grid_spec=grid_spec,
        out_shape=jax.ShapeDtypeStruct((n_slots, d), F32),
        compiler_params=_cparams(("arbitrary", "arbitrary")),
        name="moe_experts",
    )(block_expert, n_used, slots, w_gate, w_up, w_down)


def _combine_body(dest_ref, y_ref, x_ref, g_ref, wt_ref, o_ref, buf_ref, sem, *, tm, per_batch):
    base = (pl.program_id(0) * per_batch + pl.program_id(1)) * tm

    def copy(r, kk):
        return pltpu.make_async_copy(y_ref.at[pl.ds(dest_ref[(base + r) * TOP_K + kk], 1)],
                                     buf_ref.at[kk, pl.ds(r, 1)], sem)

    def start(r, _):
        for kk in range(TOP_K):
            copy(r, kk).start()
        return 0

    def wait(r, _):
        for kk in range(TOP_K):
            copy(r, kk).wait()
        return 0

    lax.fori_loop(0, tm, start, 0)
    lax.fori_loop(0, tm, wait, 0)
    wt = wt_ref[0]
    f = buf_ref[0] * wt[:, 0:1] + buf_ref[1] * wt[:, 1:2]
    o_ref[0] = x_ref[0] + g_ref[0] * f


def _combine(y_slots, dest, weights, x, gate):
    b, s, d = x.shape
    tm = _pick(s, 256)
    per_batch = s // tm
    grid_spec = pltpu.PrefetchScalarGridSpec(
        num_scalar_prefetch=1,
        grid=(b, per_batch),
        in_specs=[pl.BlockSpec(memory_space=pl.ANY),
                  pl.BlockSpec((1, tm, d), lambda bb, i, dest_ref: (bb, i, 0)),
                  pl.BlockSpec((1, 1, d), lambda bb, i, dest_ref: (bb, 0, 0)),
                  pl.BlockSpec((1, tm, TOP_K), lambda bb, i, dest_ref: (bb, i, 0))],
        out_specs=pl.BlockSpec((1, tm, d), lambda bb, i, dest_ref: (bb, i, 0)),
        scratch_shapes=[pltpu.VMEM((TOP_K, tm, d), F32), pltpu.SemaphoreType.DMA(())],
    )
    return pl.pallas_call(
        functools.partial(_combine_body, tm=tm, per_batch=per_batch),
        grid_spec=grid_spec,
        out_shape=jax.ShapeDtypeStruct(x.shape, F32),
        compiler_params=_cparams(("arbitrary", "arbitrary")),
        name="moe_combine",
    )(dest, y_slots, x, gate, weights)


def _mixer_tiles(s, n_keys):
    k_base = next((t for t in (1024, 512, 256) if n_keys % t == 0), None)
    if k_base is None:
        lp = -(-n_keys // LANE) * LANE
        whole = lp if lp <= 4096 else LANE
        return {"lp": lp, "sb": (_pick(s, 256), LANE), "fox": (_pick(s, 512), whole),
                "dsa": (_pick(s, 256), whole), "idx": (_pick(s, LANE), LANE)}
    return {
        "lp": n_keys,
        "sb": (_pick(s, 256), min(k_base, 512)),
        "fox": (_pick(s, 512), k_base),
        "dsa": (_pick(s, 256), k_base),
        "idx": (_pick(s, LANE), min(k_base, 512)),
    }


def _pad_keys(a, lp):
    pad = lp - a.shape[1]
    if pad == 0:
        return a
    return jnp.concatenate([a, jnp.zeros((a.shape[0], pad) + a.shape[2:], a.dtype)], axis=1)


def _token_mixers(h, past, wts, layer, dims, rows_so_far):
    b, s, d = h.shape
    hd, sbh, dsh, kvh, ixh, ixd, fxh = (dims[k] for k in ("hd", "sb_heads", "dsa_heads", "kv_heads",
                                                          "idx_heads", "idx_dim", "fox_heads"))
    seg = wts["w_in_segments"]
    n_keys = s if past is None else past[0].shape[1] + s
    off = n_keys - s
    tiles = _mixer_tiles(s, n_keys)
    lp = tiles["lp"]
    direct = past is None and lp == s

    h_flat = h if direct else h.reshape(1, b * s, d)

    def proj(name, **kw):
        arr, start, width = seg[name]
        out = _matmul(h_flat, arr, b_cols=(layer, start, width), name="in_" + name, **kw)
        return out if direct else out.reshape(b, s, width)

    def stacked(new, idx):
        if rows_so_far is None:
            return new[None]
        return jnp.concatenate([rows_so_far[idx], new[None]], axis=0)

    def row_proj(name, idx, aux, **kw):
        if not direct:
            out = proj(name, **kw)
            return out, stacked(out, idx), None
        if rows_so_far is None:
            out, operand = proj(name, aux=aux, **kw)
            return out, out[None], operand
        stack, operand = proj(name, aux=aux, stack_on=rows_so_far[idx], **kw)
        return None, stack, operand

    norm = lambda key: [(wts[key][layer].reshape(1, hd), "const")]
    sb_q = proj("sb_q", out_dtype=BF16)
    sb_k, st_sb_k, kt_sb = row_proj("sb_k", 0, "bf16_t")
    sb_v, st_sb_v, vb_sb = row_proj("sb_v", 1, "bf16")
    ds_q = proj("ds_q", epi=_epi_head_rms, extras=norm("q_norm_dsa"), out_dtype=BF16)
    ds_k, st_ds_k, kt_ds = row_proj("ds_k", 2, "bf16_t", epi=_epi_head_rms, extras=norm("k_norm_dsa"))
    ds_v, st_ds_v, vb_ds = row_proj("ds_v", 3, "bf16")
    ix_q = proj("ix_q", out_dtype=BF16)
    small = proj("small", epi=_epi_masked_logsig,
                 extras=[(wts["small_bias"][layer], "row"), (wts["small_mask"], "row")])
    fx_q = proj("fx_q", epi=_epi_head_rms, extras=norm("q_norm_fox"), out_dtype=BF16)
    fx_k, st_fx_k, kt_fx = row_proj("fx_k", 5, "bf16_t", epi=_epi_head_rms, extras=norm("k_norm_fox"))
    fx_v, st_fx_v, vb_fx = row_proj("fx_v", 6, "bf16")
    gate = proj("gate", epi=_epi_sigmoid)
    ix_k = small[:, :, :ixd]
    ix_w = small[:, :, ixd:ixd + ixh]
    log_f = small[:, :, ixd + ixh:ixd + ixh + fxh]
    new_rows = (st_sb_k, st_sb_v, st_ds_k, st_ds_v, stacked(ix_k, 4), st_fx_k, st_fx_v, stacked(log_f, 7))

    if past is None:
        k_ix, lf_all = _pad_keys(ix_k, lp), _pad_keys(log_f, lp)
    else:
        cat = lambda pc, r: _pad_keys(jnp.concatenate([pc.reshape(pc.shape[0], pc.shape[1], -1), r], axis=1), lp)
        k_ix, lf_all = cat(past[4], ix_k), cat(past[7], log_f)
    if not direct:
        rows = (sb_k, sb_v, ds_k, ds_v, None, fx_k, fx_v, None)
        full = [None if r is None else (_pad_keys(r, lp) if past is None else cat(pc, r))
                for pc, r in zip(past or rows, rows)]
        kt_sb, kt_ds, kt_fx = (jnp.transpose(full[i].astype(BF16), (0, 2, 1)) for i in (0, 2, 5))
        vb_sb, vb_ds, vb_fx = (full[i].astype(BF16) for i in (1, 3, 6))

    tq, tk = tiles["sb"]
    o_sb = _sb_attention(sb_q, kt_sb, vb_sb, heads=sbh, hd=hd, off=off, tq=tq, tk=tk)

    d_all = _cumsum_lanes(jnp.transpose(lf_all, (0, 2, 1)))
    d_q = d_all[:, :, off:off + s, None]
    tq, tk = tiles["fox"]
    o_fx = _fox_attention(fx_q, kt_fx, vb_fx, d_q, d_all.reshape(b, fxh, 1, lp),
                          heads=fxh, hd=hd, off=off, tq=tq, tk=tk)

    topk = min(DSA_TOPK_MAX, n_keys // 4)
    zeros = jnp.zeros_like(k_ix)
    k_even = jnp.concatenate([k_ix, zeros], axis=2)
    k_odd = jnp.concatenate([zeros, k_ix], axis=2)
    tq, tk = tiles["idx"]
    score, thr = _indexer(ix_q, ix_w, k_even, k_odd, off=off, topk=topk, idx_heads=ixh, tq=tq, tk=tk)
    tq, tk = tiles["dsa"]
    o_ds = _dsa_attention(ds_q, kt_ds, vb_ds, score, thr, wts["bias_tiles"], kv_heads=kvh,
                          group=dsh // kvh, hd=hd, off=off, tq=tq, tk=tk)

    if direct:
        merged = _merge((o_sb, o_ds, o_fx), gate, wts["w_branch"], layer)
    else:
        flat = lambda a: a.reshape(1, b * s, a.shape[-1])
        merged = _merge(tuple(flat(o) for o in (o_sb, o_ds, o_fx)), flat(gate), wts["w_branch"], layer)
        merged = merged.reshape(b, s, d)
    return merged, new_rows


def _hier_moe(x, h, g2, wts, layer, dims):
    b, s, d = h.shape
    t = b * s
    ng, ne = dims["n_groups"], dims["n_experts"]
    info, counts = _router(h.reshape(t, d), wts["w_route"][layer], wts["b_route"][layer], n_groups=ng,
                           per_group=ne // ng)
    expert = info[:, 0:TOP_K].astype(I32)
    weights = info[:, TOP_K:2 * TOP_K]
    rank = info[:, 2 * TOP_K:3 * TOP_K].astype(I32)
    counts = counts[0, ng:ng + ne].astype(I32)
    bm = 512 if t * TOP_K >= 512 * ne else 256
    padded = (counts + bm - 1) // bm * bm
    pad_end = jnp.cumsum(padded)
    pad_start = pad_end - padded
    dest = (pad_start[expert] + rank).reshape(t * TOP_K)
    n_blocks = -(-(t * TOP_K) // bm) + ne
    block_expert = jnp.minimum(
        jnp.searchsorted(pad_end, jnp.arange(n_blocks, dtype=I32) * bm, side="right"), ne - 1).astype(I32)
    n_used = (pad_end[-1:] // bm).astype(I32)
    slots = _dispatch(h.reshape(t, d), dest, n_blocks * bm)
    y_slots = _experts(slots, block_expert, n_used, wts["w_exp_gate"], wts["w_exp_up"], wts["w_exp_down"], layer,
                       bm=bm)
    return _combine(y_slots, dest, weights.reshape(b, s, TOP_K), x, g2)


def _trunk(x, mod, past, wts, depth, dims):
    b, s, d = x.shape
    rows = None
    for l in range(depth):
        sh1, sc1, g1, sh2, sc2, g2 = mod[l]
        h = _norm_mod(x, wts["g_norm1"][l], sc1, sh1)
        layer_past = None if past is None else tuple(pc[l] for pc in past)
        merged, rows = _token_mixers(h, layer_past, wts, l, dims, rows)
        if s >= MM_TM:
            x = _matmul(merged, wts["w_out"], b_cols=(l, 0, d), epi=_epi_residual,
                        extras=[(x, "tile"), (g1, "batchrow")], name="out_proj")
        else:
            flat = lambda a: jnp.broadcast_to(a, (b, s, d)).reshape(1, b * s, d)
            x = _matmul(flat(merged), wts["w_out"], b_cols=(l, 0, d), epi=_epi_residual,
                        extras=[(flat(x), "tile"), (flat(g1), "tile")], name="out_proj").reshape(b, s, d)
        h = _norm_mod(x, wts["g_norm2"][l], sc2, sh2, out_dtype=F32)
        x = _hier_moe(x, h, g2, wts, l, dims)
    return x, rows


def _prepare_weights(dims, w_in, q_norm_dsa, k_norm_dsa, q_norm_fox, k_norm_fox, b_forget, w_branch, w_out,
                     bias_tiles, w_route_grp, b_route_grp, w_route_exp, b_route_exp, w_exp_gate, w_exp_up,
                     w_exp_down, g_norm1, g_norm2):
    hd, sbh, dsh, kvh, ixh, ixd, fxh, d = (dims[k] for k in ("hd", "sb_heads", "dsa_heads", "kv_heads",
                                                             "idx_heads", "idx_dim", "fox_heads", "d"))
    depth = w_in.shape[0]
    widths = [("sb_q", sbh * hd), ("sb_k", sbh * hd), ("sb_v", sbh * hd), ("ds_q", dsh * hd), ("ds_k", kvh * hd),
              ("ds_v", kvh * hd), ("ix_q", ixh * ixd), ("ix_k", ixd), ("ix_w", ixh), ("fx_q", fxh * hd),
              ("fx_k", fxh * hd), ("fx_v", fxh * hd), ("fx_f", fxh), ("gate", 3 * d)]
    cols = {}
    start = 0
    for name, w in widths:
        cols[name] = (start, start + w)
        start += w
    runs = []
    for name, (a, bnd) in cols.items():
        width = bnd - a
        if width % LANE:
            continue
        if runs and runs[-1][1] == a and (a - runs[-1][0]) % _pick(width, MM_TN) == 0:
            runs[-1][1] = bnd
            runs[-1][2].append((name, a - runs[-1][0], width))
        else:
            runs.append([a, bnd, [(name, 0, width)]])
    seg = {}
    for first, last, members in runs:
        arr = w_in[:, :, first:last].astype(BF16)
        for name, rel, width in members:
            seg[name] = (arr, rel, width)
    n_small = ixd + ixh + fxh
    small_w = -(-n_small // LANE) * LANE
    small = jnp.concatenate([w_in[:, :, cols[name][0]:cols[name][1]] for name in ("ix_k", "ix_w", "fx_f")]
                            + [jnp.zeros((depth, d, small_w - n_small), F32)], axis=2)
    seg["small"] = (small.astype(BF16), 0, small_w)
    lane = jnp.arange(small_w)
    is_f = jnp.logical_and(lane >= ixd + ixh, lane < n_small)
    small_bias = jnp.zeros((depth, 1, small_w), F32).at[:, 0, ixd + ixh:n_small].set(b_forget.astype(F32))
    ng, ne = dims["n_groups"], dims["n_experts"]
    w_route = jnp.concatenate([w_route_grp, w_route_exp, jnp.zeros((depth, d, LANE - ng - ne), F32)], axis=2)
    b_route = jnp.concatenate([b_route_grp, b_route_exp, jnp.zeros((depth, LANE - ng - ne), F32)], axis=1)
    return {
        "w_in_segments": seg,
        "small_bias": small_bias,
        "small_mask": is_f.astype(F32).reshape(1, small_w),
        "q_norm_dsa": q_norm_dsa, "k_norm_dsa": k_norm_dsa, "q_norm_fox": q_norm_fox, "k_norm_fox": k_norm_fox,
        "w_branch": w_branch.astype(BF16), "w_out": w_out.astype(BF16),
        "bias_tiles": bias_tiles,
        "w_route": w_route.astype(BF16), "b_route": b_route.astype(F32).reshape(depth, 1, LANE),
        "w_exp_gate": w_exp_gate.astype(BF16), "w_exp_up": w_exp_up.astype(BF16),
        "w_exp_down": w_exp_down.astype(BF16),
        "g_norm1": g_norm1, "g_norm2": g_norm2,
    }


def kernel(x_prompt, x_sample, c_prompt, c_sample, cache_sb_k, cache_sb_v, cache_dsa_k, cache_dsa_v, cache_dsa_kidx, cache_fox_k, cache_fox_v, cache_fox_logf, w_mod, b_mod, g_norm1, g_norm2, w_in, q_norm_dsa, k_norm_dsa, q_norm_fox, k_norm_fox, b_forget, w_branch, w_out, rel_bias, w_route_grp, b_route_grp, w_route_exp, b_route_exp, w_exp_gate, w_exp_up, w_exp_down):
    depth = w_in.shape[0]
    d = x_prompt.shape[-1]
    hd = q_norm_dsa.shape[-1]
    dims = {
        "d": d, "hd": hd,
        "sb_heads": cache_sb_k.shape[3], "dsa_heads": rel_bias.shape[1], "kv_heads": cache_dsa_k.shape[3],
        "idx_dim": cache_dsa_kidx.shape[-1], "fox_heads": cache_fox_k.shape[3],
        "n_groups": w_route_grp.shape[-1], "n_experts": w_route_exp.shape[-1],
    }
    fixed = (3 * dims["sb_heads"] * hd + dims["dsa_heads"] * hd + 2 * dims["kv_heads"] * hd + dims["idx_dim"]
             + 3 * dims["fox_heads"] * hd + dims["fox_heads"] + 3 * d)
    dims["idx_heads"] = (w_in.shape[2] - fixed) // (dims["idx_dim"] + 1)

    bias_tiles = _bias_tiles(rel_bias)
    wts = _prepare_weights(dims, w_in, q_norm_dsa, k_norm_dsa, q_norm_fox, k_norm_fox, b_forget, w_branch, w_out,
                           bias_tiles, w_route_grp, b_route_grp, w_route_exp, b_route_exp, w_exp_gate, w_exp_up,
                           w_exp_down, g_norm1, g_norm2)

    nb_p, nb_s = c_prompt.shape[0], c_sample.shape[0]
    rows = -(-(nb_p + nb_s) // 8) * 8
    c_all = jnp.concatenate([c_prompt, c_sample, jnp.zeros((rows - nb_p - nb_s, d), F32)], axis=0)[None]
    mods_p, mods_s = [], []
    for l in range(depth):
        mod = _matmul(c_all, w_mod, b_cols=(l, 0, w_mod.shape[2]), epi=_epi_bias,
                      extras=[(b_mod[l].reshape(1, -1), "row")], silu_a=True, name="adaln_mod")[0]
        mods_p.append([m[:, None, :] for m in jnp.split(mod[:nb_p], 6, axis=-1)])
        mods_s.append([m[:, None, :] for m in jnp.split(mod[nb_p:nb_p + nb_s], 6, axis=-1)])

    past = (cache_sb_k, cache_sb_v, cache_dsa_k, cache_dsa_v, cache_dsa_kidx, cache_fox_k, cache_fox_v,
            cache_fox_logf)
    y_p, rows_p = _trunk(x_prompt, mods_p, None, wts, depth, dims)
    y_s, rows_s = _trunk(x_sample, mods_s, past, wts, depth, dims)

    head_counts = (dims["sb_heads"], dims["sb_heads"], dims["kv_heads"], dims["kv_heads"], None,
                   dims["fox_heads"], dims["fox_heads"], None)

    def split_heads(rows):
        return [a if heads is None else a.reshape(a.shape[0], a.shape[1], a.shape[2], heads, hd)
                for a, heads in zip(rows, head_counts)]

    return (y_p, y_s, *split_heads(rows_p), *split_heads(rows_s))
```

```python
import functools
import math

import jax
import jax.numpy as jnp
from jax import lax
from jax.experimental import pallas as pl
from jax.experimental.pallas import tpu as pltpu

F32 = jnp.float32
BF16 = jnp.bfloat16
I32 = jnp.int32

EPS = 1e-6
CHUNK = 64
DSA_TOPK_MAX = 256
MAX_DISTANCE = 128
TOP_K = 2
LANE = 128
NEG_BIG = -1e30
LOG2E = math.log2(math.e)
F32_EXP2_UNDERFLOW = -151.0
VMEM_LIMIT = 56 * 1024 * 1024
MM_TM, MM_TN = 1024, 512


def _cparams(sem):
    return pltpu.CompilerParams(dimension_semantics=sem, vmem_limit_bytes=VMEM_LIMIT)


def _pick(n, pref):
    if n <= pref:
        return n
    t = pref
    while n % t:
        t //= 2
    return t


def _norm_mod_body(x_ref, g_ref, sc_ref, sh_ref, o_ref):
    x = x_ref[0]
    ms = jnp.mean(x * x, axis=-1, keepdims=True)
    y = x * lax.rsqrt(ms + EPS) * g_ref[...]
    o_ref[0] = (y * (1.0 + sc_ref[0]) + sh_ref[0]).astype(o_ref.dtype)


def _norm_mod(x, g, sc, sh, out_dtype=BF16):
    b, s, d = x.shape
    ts = _pick(s, 256)
    return pl.pallas_call(
        _norm_mod_body,
        grid=(b, s // ts),
        in_specs=[pl.BlockSpec((1, ts, d), lambda i, j: (i, j, 0)),
                  pl.BlockSpec((1, d), lambda i, j: (0, 0)),
                  pl.BlockSpec((1, 1, d), lambda i, j: (i, 0, 0)),
                  pl.BlockSpec((1, 1, d), lambda i, j: (i, 0, 0))],
        out_specs=pl.BlockSpec((1, ts, d), lambda i, j: (i, j, 0)),
        out_shape=jax.ShapeDtypeStruct((b, s, d), out_dtype),
        compiler_params=_cparams(("parallel", "parallel")),
        name="norm_mod",
    )(x, g.reshape(1, d), sc, sh)


def _log_sigmoid(x):
    return jnp.minimum(x, 0.0) - jnp.log1p(jnp.exp(-jnp.abs(x)))


def _epi_none(acc):
    return acc


def _epi_bias(acc, bias):
    return acc + bias


def _epi_head_rms(acc, gain):
    hd = gain.shape[-1]
    outs = []
    for c in range(acc.shape[-1] // hd):
        blk = acc[:, c * hd:(c + 1) * hd]
        ms = jnp.mean(blk * blk, axis=-1, keepdims=True)
        outs.append(blk * lax.rsqrt(ms + EPS) * gain)
    return jnp.concatenate(outs, axis=-1) if len(outs) > 1 else outs[0]


def _epi_masked_logsig(acc, bias, mask):
    return jnp.where(mask > 0.0, _log_sigmoid(acc + bias), acc)


def _epi_sigmoid(acc):
    return jax.nn.sigmoid(acc)


def _epi_residual(acc, res, gate):
    return res + gate * acc


def _mm_body(*refs, epi, n_extra, silu_a, aux, stacked):
    a_ref, b_ref = refs[0], refs[1]
    extras = refs[2:2 + n_extra]
    n_in = 2 + n_extra + (1 if stacked else 0)
    o_ref = refs[n_in]
    a = a_ref[0]
    if silu_a:
        a = a.astype(F32)
        a = a * jax.nn.sigmoid(a)
    acc = jnp.dot(a.astype(BF16), b_ref[...].astype(BF16), preferred_element_type=F32)
    vals = []
    for r in extras:
        v = r[...]
        vals.append(v[0] if v.ndim == 3 else v)
    res = epi(acc, *vals)
    if stacked:
        prev_ref = refs[n_in - 1]
        n_prev = prev_ref.shape[0]
        o_ref[0:n_prev, 0] = prev_ref[:, 0]
        o_ref[n_prev, 0] = res.astype(o_ref.dtype)
    else:
        o_ref[0] = res.astype(o_ref.dtype)
    if aux == "bf16":
        refs[n_in + 1][0] = res.astype(BF16)
    elif aux == "bf16_t":
        refs[n_in + 1][0] = res.T.astype(BF16)


def _matmul(a, b, *, epi=_epi_none, extras=(), out_dtype=F32, tm=MM_TM, tn=MM_TN, silu_a=False, aux=None,
            b_cols=None, stack_on=None, name="matmul"):
    bsz, s, k = a.shape
    tm = _pick(s, tm)
    if b_cols is None:
        n = b.shape[1]
        tn = _pick(n, tn)
        b_spec = pl.BlockSpec((k, tn), lambda bb, i, j: (0, j))
    else:
        layer, start, n = b_cols
        tn = _pick(n, tn)
        assert start % tn == 0
        b_spec = pl.BlockSpec((None, k, tn), lambda bb, i, j: (layer, 0, start // tn + j))
    in_specs = [pl.BlockSpec((1, tm, k), lambda bb, i, j: (bb, i, 0)), b_spec]
    args = [a, b]
    for arr, kind in extras:
        if kind == "row":
            in_specs.append(pl.BlockSpec((1, tn), lambda bb, i, j: (0, j)))
        elif kind == "batchrow":
            in_specs.append(pl.BlockSpec((1, 1, tn), lambda bb, i, j: (bb, 0, j)))
        elif kind == "tile":
            in_specs.append(pl.BlockSpec((1, tm, tn), lambda bb, i, j: (bb, i, j)))
        else:
            in_specs.append(pl.BlockSpec(arr.shape, lambda bb, i, j, nd=arr.ndim: (0,) * nd))
        args.append(arr)
    if stack_on is None:
        out_specs = pl.BlockSpec((1, tm, tn), lambda bb, i, j: (bb, i, j))
        out_shape = jax.ShapeDtypeStruct((bsz, s, n), out_dtype)
    else:
        n_prev = stack_on.shape[0]
        in_specs.append(pl.BlockSpec((n_prev, 1, tm, tn), lambda bb, i, j: (0, bb, i, j)))
        args.append(stack_on)
        out_specs = pl.BlockSpec((n_prev + 1, 1, tm, tn), lambda bb, i, j: (0, bb, i, j))
        out_shape = jax.ShapeDtypeStruct((n_prev + 1, bsz, s, n), out_dtype)
    if aux == "bf16":
        out_specs = [out_specs, pl.BlockSpec((1, tm, tn), lambda bb, i, j: (bb, i, j))]
        out_shape = [out_shape, jax.ShapeDtypeStruct((bsz, s, n), BF16)]
    elif aux == "bf16_t":
        out_specs = [out_specs, pl.BlockSpec((1, tn, tm), lambda bb, i, j: (bb, j, i))]
        out_shape = [out_shape, jax.ShapeDtypeStruct((bsz, n, s), BF16)]
    return pl.pallas_call(
        functools.partial(_mm_body, epi=epi, n_extra=len(extras), silu_a=silu_a, aux=aux,
                          stacked=stack_on is not None),
        grid=(bsz, s // tm, n // tn),
        in_specs=in_specs,
        out_specs=out_specs,
        out_shape=out_shape,
        compiler_params=_cparams(("parallel", "parallel", "parallel")),
        name=name,
    )(*args)


def _sb_body(q_ref, kt_ref, v_ref, o_ref, run_ref, acc_ref, *, tq, tk, cw, off, scale):
    i = pl.program_id(2)
    q_lo = off + i * tq
    q = (q_ref[0].astype(F32) * (scale * LOG2E)).astype(BF16)
    tri_r = lax.broadcasted_iota(I32, (2 * cw, cw), 0)
    tri_c = lax.broadcasted_iota(I32, (2 * cw, cw), 1)
    suffix = jnp.where(jnp.where(tri_r >= cw, tri_r - cw, tri_r) >= tri_c, 1.0, 0.0).astype(BF16)
    n_blocks = (q_lo + tq - 2) // tk + 1
    n_plain = q_lo // tk
    run_ref[...] = jnp.zeros_like(run_ref)
    acc_ref[...] = jnp.zeros_like(acc_ref)

    def tile(kb, masked):
        ks = pl.multiple_of(kb * tk, tk)
        kt = kt_ref[0, :, pl.ds(ks, tk)]
        v = v_ref[0, pl.ds(ks, tk), :]
        z_all = jnp.dot(q, kt, preferred_element_type=F32)
        run = jnp.max(run_ref[...], axis=1, keepdims=True)
        ws = [None] * (tk // cw)
        for j in reversed(range(tk // cw)):
            z = z_all[:, j * cw:(j + 1) * cw]
            log_keep = jnp.minimum(-z, 0.0) - jnp.log2(1.0 + jnp.exp2(jnp.minimum(z, -z)))
            if masked:
                row = lax.broadcasted_iota(I32, (tq, cw), 0)
                col = lax.broadcasted_iota(I32, (tq, cw), 1)
                mask = (ks + j * cw + col) < (q_lo + row)
                log_keep = jnp.where(mask, log_keep, 0.0)
            hi = log_keep.astype(BF16)
            lo = (log_keep - hi.astype(F32)).astype(BF16)
            incl = jnp.dot(jnp.concatenate([hi, lo], axis=1), suffix,
                           preferred_element_type=F32)
            w = jnp.exp2(z + incl + run)
            if masked:
                w = jnp.where(mask, w, 0.0)
            ws[j] = w.astype(BF16)
            run = run + jnp.sum(log_keep, axis=1, keepdims=True)
        w_all = jnp.concatenate(ws, axis=1) if len(ws) > 1 else ws[0]
        acc_ref[...] = acc_ref[...] + jnp.dot(w_all, v, preferred_element_type=F32)
        run_ref[...] = jnp.broadcast_to(run, run_ref.shape)

    def masked_step(n, carry):
        tile(n_blocks - 1 - n, True)
        return carry

    lax.fori_loop(0, n_blocks - n_plain, masked_step, 0)

    def alive():
        return jnp.max(run_ref[...]) > F32_EXP2_UNDERFLOW

    def cond(carry):
        kb, go = carry
        return jnp.logical_and(kb >= 0, go)

    def plain_step(carry):
        kb, _ = carry
        tile(kb, False)
        return kb - 1, alive()

    lax.while_loop(cond, plain_step, (n_plain - 1, alive()))
    o_ref[0] = acc_ref[...].astype(o_ref.dtype)


def _sb_attention(q, kt, v, *, heads, hd, off, tq, tk):
    b, s, _ = q.shape
    lp = v.shape[1]
    cw = min(tk, 256)
    return pl.pallas_call(
        functools.partial(_sb_body, tq=tq, tk=tk, cw=cw, off=off, scale=hd ** -0.5),
        grid=(b, heads, s // tq),
        in_specs=[pl.BlockSpec((1, tq, hd), lambda bb, h, i: (bb, i, h)),
                  pl.BlockSpec((1, hd, lp), lambda bb, h, i: (bb, h, 0)),
                  pl.BlockSpec((1, lp, hd), lambda bb, h, i: (bb, 0, h))],
        out_specs=pl.BlockSpec((1, tq, hd), lambda bb, h, i: (bb, i, h)),
        out_shape=jax.ShapeDtypeStruct(q.shape, BF16),
        scratch_shapes=[pltpu.VMEM((tq, LANE), F32), pltpu.VMEM((tq, hd), F32)],
        compiler_params=_cparams(("parallel", "parallel", "arbitrary")),
        name="sb_attention",
    )(q, kt, v)


def _cumsum_body(x_ref, o_ref):
    x = x_ref[0]
    n = x.shape[1]
    lane = lax.broadcasted_iota(I32, x.shape, 1)
    shift = 1
    while shift < n:
        x = x + jnp.where(lane >= shift, pltpu.roll(x, shift, axis=1), 0.0)
        shift *= 2
    o_ref[0] = x


def _cumsum_lanes(x):
    b, h, n = x.shape
    return pl.pallas_call(
        _cumsum_body,
        grid=(b,),
        in_specs=[pl.BlockSpec((1, h, n), lambda i: (i, 0, 0))],
        out_specs=pl.BlockSpec((1, h, n), lambda i: (i, 0, 0)),
        out_shape=jax.ShapeDtypeStruct(x.shape, F32),
        compiler_params=_cparams(("parallel",)),
        name="cumsum_logf",
    )(x)


def _online_softmax_step(z_cols, v, m_ref, l_ref, acc_ref, shift_extra=None):
    m_prev = m_ref[...]
    m_cur = z_cols[0]
    for z in z_cols[1:]:
        m_cur = jnp.maximum(m_cur, z)
    m_cur = jnp.max(m_cur, axis=1, keepdims=True)
    if shift_extra is not None:
        m_cur = m_cur + shift_extra
    m_next = jnp.maximum(m_prev, m_cur)
    shift = m_next if shift_extra is None else m_next - shift_extra
    ps = [jnp.exp2(z - shift) for z in z_cols]
    alpha = jnp.exp2(m_prev - m_next)
    l_new = alpha * l_ref[...]
    for p in ps:
        l_new = l_new + p
    l_ref[...] = l_new
    m_ref[...] = m_next
    p_all = jnp.concatenate([p.astype(BF16) for p in ps], axis=1) if len(ps) > 1 else ps[0].astype(BF16)
    acc_ref[...] = alpha * acc_ref[...] + jnp.dot(p_all, v, preferred_element_type=F32)


def _fox_body(q_ref, kt_ref, v_ref, dq_ref, dk_ref, o_ref, m_ref, l_ref, acc_ref, kmax_ref, *, tq, tk, off, scale):
    i = pl.program_id(2)
    q_lo = off + i * tq
    q = (q_ref[0].astype(F32) * (scale * LOG2E)).astype(BF16)
    dq = dq_ref[0, 0] * LOG2E
    n_blocks = (q_lo + tq - 1) // tk + 1
    n_plain = q_lo // tk
    n_cols = tk // LANE
    lp = kt_ref.shape[2]
    m_ref[...] = jnp.full_like(m_ref, NEG_BIG)
    l_ref[...] = jnp.zeros_like(l_ref)
    acc_ref[...] = jnp.zeros_like(acc_ref)

    @pl.when(i == 0)
    def _():
        kf = kt_ref[0].astype(F32)
        norm2 = jnp.max(jnp.sum(kf * kf, axis=0, keepdims=True), axis=1, keepdims=True)
        kmax_ref[...] = jnp.broadcast_to(jnp.sqrt(norm2), kmax_ref.shape)

    qf = q.astype(F32)
    cap = jnp.sqrt(jnp.sum(qf * qf, axis=1, keepdims=True)) * kmax_ref[0:1, 0:1] * 1.001 + 0.001 + dq

    def step(kb, masked):
        ks = pl.multiple_of(kb * tk, tk)
        kt = kt_ref[0, :, pl.ds(ks, tk)]
        v = v_ref[0, pl.ds(ks, tk), :]
        dk = dk_ref[0, 0, :, pl.ds(ks, tk)] * LOG2E
        z = jnp.dot(q, kt, preferred_element_type=F32) - dk
        if masked:
            row = lax.broadcasted_iota(I32, (tq, tk), 0)
            col = lax.broadcasted_iota(I32, (tq, tk), 1)
            z = jnp.where((ks + col) <= (q_lo + row), z, NEG_BIG)
        _online_softmax_step([z[:, c * LANE:(c + 1) * LANE] for c in range(n_cols)], v, m_ref, l_ref, acc_ref,
                             shift_extra=dq)

    def masked_step(n, carry):
        step(n_blocks - 1 - n, True)
        return carry

    lax.fori_loop(0, n_blocks - n_plain, masked_step, 0)

    def alive(kb):
        lane = lax.broadcasted_iota(I32, (1, lp), 1)
        dk_lo = jnp.min(jnp.where(lane < (kb + 1) * tk, dk_ref[0, 0] * LOG2E, jnp.inf), axis=1, keepdims=True)
        m_now = jnp.max(m_ref[...], axis=1, keepdims=True)
        return jnp.max(cap - dk_lo - m_now) > F32_EXP2_UNDERFLOW

    def cond(carry):
        kb, go = carry
        return jnp.logical_and(kb >= 0, go)

    def plain_step(carry):
        kb, _ = carry
        step(kb, False)
        return kb - 1, alive(kb - 1)

    lax.while_loop(cond, plain_step, (n_plain - 1, alive(n_plain - 1)))
    o_ref[0] = (acc_ref[...] / jnp.sum(l_ref[...], axis=1, keepdims=True)).astype(o_ref.dtype)


def _fox_attention(q, kt, v, d_q, d_k, *, heads, hd, off, tq, tk):
    b, s, _ = q.shape
    lp = v.shape[1]
    return pl.pallas_call(
        functools.partial(_fox_body, tq=tq, tk=tk, off=off, scale=hd ** -0.5),
        grid=(b, heads, s // tq),
        in_specs=[pl.BlockSpec((1, tq, hd), lambda bb, h, i: (bb, i, h)),
                  pl.BlockSpec((1, hd, lp), lambda bb, h, i: (bb, h, 0)),
                  pl.BlockSpec((1, lp, hd), lambda bb, h, i: (bb, 0, h)),
                  pl.BlockSpec((1, 1, tq, 1), lambda bb, h, i: (bb, h, i, 0)),
                  pl.BlockSpec((1, 1, 1, lp), lambda bb, h, i: (bb, h, 0, 0))],
        out_specs=pl.BlockSpec((1, tq, hd), lambda bb, h, i: (bb, i, h)),
        out_shape=jax.ShapeDtypeStruct(q.shape, BF16),
        scratch_shapes=[pltpu.VMEM((tq, LANE), F32), pltpu.VMEM((tq, LANE), F32), pltpu.VMEM((tq, hd), F32),
                        pltpu.VMEM((8, LANE), F32)],
        compiler_params=_cparams(("arbitrary", "arbitrary", "arbitrary")),
        name="fox_attention",
    )(q, kt, v, d_q, d_k)


def _sortable(x):
    u = lax.bitcast_convert_type(x, I32)
    return u ^ (lax.shift_right_arithmetic(u, 31) & 0x7FFFFFFF)


def _indexer_body(qi_ref, wi_ref, ka_ref, kb_ref, sc_ref, thr_ref, key_ref, *, tq, tk, off, topk, idx_heads, w_scale):
    i = pl.program_id(1)
    q_lo = off + i * tq
    lp = ka_ref.shape[1]
    adm_end = ((q_lo + tq - 1) // CHUNK + 1) * CHUNK
    n_adm = (adm_end + tk - 1) // tk
    n_all = lp // tk
    wi = wi_ref[0] * w_scale
    qi = qi_ref[0].astype(BF16)
    pair = ka_ref.shape[2]
    w_cols = [wi[:, j:j + 1] for j in range(idx_heads)]
    q_pairs = [qi[:, p * pair:(p + 1) * pair] for p in range(idx_heads // 2)]
    row = lax.broadcasted_iota(I32, (tq, tk), 0)
    col = lax.broadcasted_iota(I32, (tq, tk), 1)
    q_chunk = (q_lo + row) // CHUNK

    def score_step(kb, _):
        ks = pl.multiple_of(kb * tk, tk)
        k_even = ka_ref[0, pl.ds(ks, tk), :].astype(BF16)
        k_odd = kb_ref[0, pl.ds(ks, tk), :].astype(BF16)
        total = jnp.zeros((tq, tk), F32)
        for p in range(idx_heads // 2):
            s0 = lax.dot_general(q_pairs[p], k_even, (((1,), (1,)), ((), ())), preferred_element_type=F32)
            s1 = lax.dot_general(q_pairs[p], k_odd, (((1,), (1,)), ((), ())), preferred_element_type=F32)
            total = total + jnp.maximum(s0, 0.0) * w_cols[2 * p] + jnp.maximum(s1, 0.0) * w_cols[2 * p + 1]
        admissible = ((ks + col) // CHUNK) <= q_chunk
        total = jnp.where(admissible, total, -jnp.inf)
        sc_ref[0, :, pl.ds(ks, tk)] = total
        key_ref[:, pl.ds(ks, tk)] = _sortable(total)
        return 0

    lax.fori_loop(0, n_adm, score_step, 0)

    def fill_step(kb, _):
        ks = pl.multiple_of(kb * tk, tk)
        sc_ref[0, :, pl.ds(ks, tk)] = jnp.full((tq, tk), -jnp.inf, F32)
        return 0

    lax.fori_loop(n_adm, n_all, fill_step, 0)

    sign = jnp.int32(-2 ** 31)

    def count(hit_fn):
        def count_step(kb, cnt):
            ks = pl.multiple_of(kb * tk, tk)
            hit = jnp.where(hit_fn(key_ref[:, pl.ds(ks, tk)]), 1.0, 0.0)
            for c in range(tk // LANE):
                cnt = cnt + hit[:, c * LANE:(c + 1) * LANE]
            return cnt

        cnt = lax.fori_loop(0, n_adm, count_step, jnp.zeros((tq, LANE), F32))
        return jnp.sum(cnt, axis=1, keepdims=True)

    def bit_step(carry):
        n, t_bits, settled = carry
        bit = lax.shift_left(jnp.int32(1), 31 - n)
        cand_bits = t_bits | bit
        cand = cand_bits ^ sign
        cnt = count(lambda keys: keys >= cand)
        settled = jnp.maximum(settled, jnp.where(cnt == float(topk), 1.0, 0.0))
        return n + 1, jnp.where(cnt >= float(topk), cand_bits, t_bits), settled

    def bits_left(carry):
        n, _, settled = carry
        return jnp.logical_and(n < 32, jnp.min(settled) < 1.0)

    _, t_bits, _ = lax.while_loop(bits_left, bit_step,
                                  (jnp.int32(0), jnp.zeros((tq, 1), I32), jnp.zeros((tq, 1), F32)))
    t_key = t_bits ^ sign
    t_u = t_key ^ (lax.shift_right_arithmetic(t_key, 31) & 0x7FFFFFFF)
    thr = lax.bitcast_convert_type(t_u, F32)
    need = float(topk) - count(lambda keys: keys > t_key)
    surplus = jnp.where(thr > -jnp.inf, count(lambda keys: keys >= t_key) - float(topk), 0.0)
    lane = lax.broadcasted_iota(I32, (tq, LANE), 1)
    thr_ref[0] = jnp.where(lane == 0, thr, jnp.where(lane == 1, need, jnp.where(lane == 2, surplus, 0.0)))


def _indexer(qi, wi, k_even, k_odd, *, off, topk, idx_heads, tq, tk):
    b, s, _ = qi.shape
    lp = k_even.shape[1]
    idx_dim = qi.shape[2] // idx_heads
    w_scale = (idx_heads ** -0.5) * (idx_dim ** -0.5)
    return pl.pallas_call(
        functools.partial(_indexer_body, tq=tq, tk=tk, off=off, topk=topk, idx_heads=idx_heads, w_scale=w_scale),
        grid=(b, s // tq),
        in_specs=[pl.BlockSpec((1, tq, qi.shape[2]), lambda bb, i: (bb, i, 0)),
                  pl.BlockSpec((1, tq, idx_heads), lambda bb, i: (bb, i, 0)),
                  pl.BlockSpec((1, lp, k_even.shape[2]), lambda bb, i: (bb, 0, 0)),
                  pl.BlockSpec((1, lp, k_odd.shape[2]), lambda bb, i: (bb, 0, 0))],
        out_specs=[pl.BlockSpec((1, tq, lp), lambda bb, i: (bb, i, 0)),
                   pl.BlockSpec((1, tq, LANE), lambda bb, i: (bb, i, 0))],
        out_shape=[jax.ShapeDtypeStruct((b, s, lp), F32), jax.ShapeDtypeStruct((b, s, LANE), F32)],
        scratch_shapes=[pltpu.VMEM((tq, lp), I32)],
        compiler_params=_cparams(("parallel", "arbitrary")),
        name="dsa_indexer",
    )(qi, wi, k_even, k_odd)


def _bias_tile_body(rb_ref, o_ref, *, n_buckets, heads, n_off):
    half = n_buckets // 2
    exact = half // 2
    row = lax.broadcasted_iota(I32, (LANE, LANE), 0)
    col = lax.broadcasted_iota(I32, (LANE, LANE), 1)
    buckets = []
    for d in range(n_off):
        rel = d * LANE + row - col
        side = jnp.where(rel < 0, half, 0)
        a = jnp.abs(rel)
        far = exact + (jnp.log(jnp.maximum(a, 1).astype(F32) / exact)
                       / math.log(MAX_DISTANCE / exact) * (half - exact)).astype(I32)
        far = jnp.minimum(far, half - 1)
        buckets.append(side + jnp.where(a < exact, a, far))
    for h in range(heads):
        tiles = []
        for bucket in buckets:
            tile = jnp.zeros((LANE, LANE), F32)
            for bkt in range(n_buckets):
                tile = jnp.where(bucket == bkt, rb_ref[bkt, h], tile)
            tiles.append(tile)
        for d in range(n_off):
            o_ref[d, h] = tiles[d] - tiles[n_off - 1]


def _bias_tiles(rel_bias, n_off=3):
    n_buckets, heads = rel_bias.shape
    return pl.pallas_call(
        functools.partial(_bias_tile_body, n_buckets=n_buckets, heads=heads, n_off=n_off),
        in_specs=[pl.BlockSpec(memory_space=pltpu.SMEM)],
        out_specs=pl.BlockSpec(memory_space=pltpu.VMEM),
        out_shape=jax.ShapeDtypeStruct((n_off, heads, LANE, LANE), F32),
        name="dsa_bias_tiles",
    )(rel_bias)


def _dsa_body(q_ref, kt_ref, v_ref, sc_ref, sel_ref, bias_ref, o_ref, m_ref, l_ref, acc_ref, tied_ref, *, tq, tk,
              off, scale, group, hd):
    i = pl.program_id(2)
    q_lo = off + i * tq
    n_off = bias_ref.shape[0]
    adm_end = ((q_lo + tq - 1) // CHUNK + 1) * CHUNK
    n_blocks = (adm_end + tk - 1) // tk
    n_far = jnp.maximum(q_lo - LANE, 0) // tk
    n_cols = tk // LANE
    thr = sel_ref[0, :, 0:1]
    need = sel_ref[0, :, 1:2]
    any_surplus = jnp.max(sel_ref[0, :, 2:3]) > 0.0
    qs = [(q_ref[0, :, g * hd:(g + 1) * hd].astype(F32) * (scale * LOG2E)).astype(BF16) for g in range(group)]
    m_ref[...] = jnp.full_like(m_ref, NEG_BIG)
    l_ref[...] = jnp.zeros_like(l_ref)
    acc_ref[...] = jnp.zeros_like(acc_ref)
    tied_ref[...] = jnp.zeros_like(tied_ref)

    def selected(sc, ties):
        finite = jnp.abs(sc) < jnp.inf
        if not ties:
            return jnp.logical_and(sc >= thr, finite)
        tri_r = lax.broadcasted_iota(I32, (LANE, LANE), 0)
        tri_c = lax.broadcasted_iota(I32, (LANE, LANE), 1)
        before = jnp.where(tri_r < tri_c, 1.0, 0.0).astype(BF16)
        seen = jnp.max(tied_ref[...], axis=1, keepdims=True)
        cols = []
        for c in range(n_cols):
            sc_c = sc[:, c * LANE:(c + 1) * LANE]
            fin_c = finite[:, c * LANE:(c + 1) * LANE]
            tie = jnp.where(jnp.logical_and(sc_c == thr, fin_c), 1.0, 0.0)
            rank = seen + jnp.dot(tie.astype(BF16), before, preferred_element_type=F32)
            take = jnp.logical_and(tie > 0.0, rank < need)
            cols.append(jnp.logical_or(jnp.logical_and(sc_c > thr, fin_c), take))
            seen = seen + jnp.sum(tie, axis=1, keepdims=True)
        tied_ref[...] = jnp.broadcast_to(seen, tied_ref.shape)
        return jnp.concatenate(cols, axis=1) if n_cols > 1 else cols[0]

    def step(kb, near, ties):
        ks = pl.multiple_of(kb * tk, tk)
        kt = kt_ref[0, :, pl.ds(ks, tk)]
        v = v_ref[0, pl.ds(ks, tk), :]
        sc = sc_ref[0, :, pl.ds(ks, tk)]
        neg = jnp.where(selected(sc, ties), 0.0, NEG_BIG)
        for g in range(group):
            z = jnp.dot(qs[g], kt, preferred_element_type=F32) + neg
            z_cols = [z[:, c * LANE:(c + 1) * LANE] for c in range(n_cols)]
            if near:
                for c in range(n_cols):
                    parts = []
                    for r in range(0, tq, LANE):
                        d_idx = jnp.clip((q_lo + r - ks - c * LANE) // LANE, 0, n_off - 1)
                        parts.append(bias_ref[d_idx, g, 0:min(LANE, tq - r), :])
                    bias = jnp.concatenate(parts, axis=0) if len(parts) > 1 else parts[0]
                    z_cols[c] = z_cols[c] + bias * LOG2E
            _online_softmax_step(z_cols, v, m_ref.at[g], l_ref.at[g], acc_ref.at[g])

    def walk(ties):
        def far_step(kb, carry):
            step(kb, False, ties)
            return carry

        def near_step(kb, carry):
            step(kb, True, ties)
            return carry

        lax.fori_loop(0, n_far, far_step, 0)
        lax.fori_loop(n_far, n_blocks, near_step, 0)

    @pl.when(any_surplus)
    def _():
        walk(True)

    @pl.when(jnp.logical_not(any_surplus))
    def _():
        walk(False)

    for g in range(group):
        o_ref[0, :, g * hd:(g + 1) * hd] = (acc_ref[g] / jnp.sum(l_ref[g], axis=1, keepdims=True)).astype(o_ref.dtype)


def _dsa_attention(q, kt, v, score, sel_info, bias_tiles, *, kv_heads, group, hd, off, tq, tk):
    b, s, _ = q.shape
    lp = v.shape[1]
    n_off = bias_tiles.shape[0]
    return pl.pallas_call(
        functools.partial(_dsa_body, tq=tq, tk=tk, off=off, scale=hd ** -0.5, group=group, hd=hd),
        grid=(b, kv_heads, s // tq),
        in_specs=[pl.BlockSpec((1, tq, group * hd), lambda bb, h, i: (bb, i, h)),
                  pl.BlockSpec((1, hd, lp), lambda bb, h, i: (bb, h, 0)),
                  pl.BlockSpec((1, lp, hd), lambda bb, h, i: (bb, 0, h)),
                  pl.BlockSpec((1, tq, lp), lambda bb, h, i: (bb, i, 0)),
                  pl.BlockSpec((1, tq, LANE), lambda bb, h, i: (bb, i, 0)),
                  pl.BlockSpec((n_off, group, LANE, LANE), lambda bb, h, i: (0, h, 0, 0))],
        out_specs=pl.BlockSpec((1, tq, group * hd), lambda bb, h, i: (bb, i, h)),
        out_shape=jax.ShapeDtypeStruct(q.shape, BF16),
        scratch_shapes=[pltpu.VMEM((group, tq, LANE), F32), pltpu.VMEM((group, tq, LANE), F32),
                        pltpu.VMEM((group, tq, hd), F32), pltpu.VMEM((tq, LANE), F32)],
        compiler_params=_cparams(("parallel", "parallel", "arbitrary")),
        name="dsa_attention",
    )(q, kt, v, score, sel_info, bias_tiles)


def _merge_body(o0_ref, o1_ref, o2_ref, g0_ref, g1_ref, g2_ref, w_ref, out_ref):
    total = None
    for r, (o_ref, g_ref) in enumerate(((o0_ref, g0_ref), (o1_ref, g1_ref), (o2_ref, g2_ref))):
        y = jnp.dot(o_ref[0], w_ref[r], preferred_element_type=F32) * g_ref[0]
        total = y if total is None else total + y
    out_ref[0] = total.astype(out_ref.dtype)


def _merge(branches, gate, w_branch, layer):
    b, s, width = branches[0].shape
    d = w_branch.shape[3]
    tm = _pick(s, 512)
    tn = _pick(d, 512)
    nj = d // tn
    o_spec = pl.BlockSpec((1, tm, width), lambda bb, i, j: (bb, i, 0))
    g_specs = [pl.BlockSpec((1, tm, tn), lambda bb, i, j, r=r: (bb, i, r * nj + j)) for r in range(3)]
    return pl.pallas_call(
        _merge_body,
        grid=(b, s // tm, nj),
        in_specs=[o_spec, o_spec, o_spec, *g_specs,
                  pl.BlockSpec((None, 3, width, tn), lambda bb, i, j: (layer, 0, 0, j))],
        out_specs=pl.BlockSpec((1, tm, tn), lambda bb, i, j: (bb, i, j)),
        out_shape=jax.ShapeDtypeStruct((b, s, d), BF16),
        compiler_params=_cparams(("parallel", "parallel", "parallel")),
        name="branch_merge",
    )(*branches, gate, gate, gate, w_branch)


def _router_body(h_ref, w_ref, b_ref, info_ref, cnt_ref, run_ref, *, n_groups, per_group):
    step = pl.program_id(0)

    @pl.when(step == 0)
    def _():
        run_ref[...] = jnp.zeros_like(run_ref)

    tm = h_ref.shape[0]
    logits = jnp.dot(h_ref[...].astype(BF16), w_ref[...], preferred_element_type=F32) + b_ref[...]
    lane = lax.broadcasted_iota(I32, (tm, LANE), 1).astype(F32)
    far = float(4 * LANE)
    is_grp = lane < n_groups
    g_logit = jnp.where(is_grp, logits, -jnp.inf)
    g_max = jnp.max(g_logit, axis=1, keepdims=True)
    grp = jnp.min(jnp.where(g_logit == g_max, lane, far), axis=1, keepdims=True)
    p_grp = 1.0 / jnp.sum(jnp.where(is_grp, jnp.exp(logits - g_max), 0.0), axis=1, keepdims=True)
    e_lo = n_groups + grp * per_group
    in_grp = jnp.logical_and(lane >= e_lo, lane < e_lo + per_group)
    e_logit = jnp.where(in_grp, logits, -jnp.inf)
    m1 = jnp.max(e_logit, axis=1, keepdims=True)
    i1 = jnp.min(jnp.where(e_logit == m1, lane, far), axis=1, keepdims=True)
    e_logit2 = jnp.where(lane == i1, -jnp.inf, e_logit)
    m2 = jnp.max(e_logit2, axis=1, keepdims=True)
    i2 = jnp.min(jnp.where(jnp.logical_and(e_logit2 == m2, in_grp), lane, far), axis=1, keepdims=True)
    e2 = jnp.exp(m2 - m1)
    w1 = p_grp / (1.0 + e2)
    w2 = p_grp * e2 / (1.0 + e2)
    hit1 = lane == i1
    hit2 = lane == i2
    onehot = jnp.where(jnp.logical_or(hit1, hit2), 1.0, 0.0)
    r_i = lax.broadcasted_iota(I32, (tm, tm), 0)
    c_i = lax.broadcasted_iota(I32, (tm, tm), 1)
    earlier = jnp.where(c_i < r_i, 1.0, 0.0).astype(BF16)
    before = jnp.dot(earlier, onehot.astype(BF16), preferred_element_type=F32) + run_ref[...]
    r1 = jnp.sum(jnp.where(hit1, before, 0.0), axis=1, keepdims=True)
    r2 = jnp.sum(jnp.where(hit2, before, 0.0), axis=1, keepdims=True)
    run_ref[...] = run_ref[...] + jnp.sum(onehot, axis=0, keepdims=True)
    cnt_ref[...] = run_ref[...]
    vals = (i1 - n_groups, i2 - n_groups, w1, w2, r1, r2)
    info = jnp.zeros((tm, LANE), F32)
    for pos, val in enumerate(vals):
        info = jnp.where(lane == pos, val, info)
    info_ref[...] = info


def _router(h, w_route, b_route, *, n_groups, per_group):
    t, d = h.shape
    tm = _pick(t, 256)
    return pl.pallas_call(
        functools.partial(_router_body, n_groups=n_groups, per_group=per_group),
        grid=(t // tm,),
        in_specs=[pl.BlockSpec((tm, d), lambda i: (i, 0)),
                  pl.BlockSpec((d, LANE), lambda i: (0, 0)),
                  pl.BlockSpec((1, LANE), lambda i: (0, 0))],
        out_specs=[pl.BlockSpec((tm, LANE), lambda i: (i, 0)),
                   pl.BlockSpec((1, LANE), lambda i: (0, 0))],
        out_shape=[jax.ShapeDtypeStruct((t, LANE), F32), jax.ShapeDtypeStruct((1, LANE), F32)],
        scratch_shapes=[pltpu.VMEM((1, LANE), F32)],
        compiler_params=_cparams(("arbitrary",)),
        name="moe_router",
    )(h, w_route, b_route)


def _dispatch_body(dest_ref, h_ref, slots_in_ref, slots_ref, sem, *, tm):
    del slots_in_ref
    base = pl.program_id(0) * tm

    def copy(r, kk):
        return pltpu.make_async_copy(h_ref.at[pl.ds(r, 1)], slots_ref.at[pl.ds(dest_ref[(base + r) * TOP_K + kk], 1)], sem)

    def start(r, _):
        for kk in range(TOP_K):
            copy(r, kk).start()
        return 0

    def wait(r, _):
        for kk in range(TOP_K):
            copy(r, kk).wait()
        return 0

    lax.fori_loop(0, tm, start, 0)
    lax.fori_loop(0, tm, wait, 0)


def _dispatch(h, dest, n_slots):
    t, d = h.shape
    tm = _pick(t, 256)
    grid_spec = pltpu.PrefetchScalarGridSpec(
        num_scalar_prefetch=1,
        grid=(t // tm,),
        in_specs=[pl.BlockSpec((tm, d), lambda i, dest_ref: (i, 0)),
                  pl.BlockSpec(memory_space=pl.ANY)],
        out_specs=pl.BlockSpec(memory_space=pl.ANY),
        scratch_shapes=[pltpu.SemaphoreType.DMA(())],
    )
    return pl.pallas_call(
        functools.partial(_dispatch_body, tm=tm),
        grid_spec=grid_spec,
        out_shape=jax.ShapeDtypeStruct((n_slots, d), h.dtype),
        input_output_aliases={2: 0},
        compiler_params=_cparams(("arbitrary",)),
        name="moe_dispatch",
    )(dest, h, jnp.zeros((n_slots, d), h.dtype))


def _experts_body(be_ref, nb_ref, x_ref, wg_ref, wu_ref, wd_ref, o_ref):
    blk = pl.program_id(0)
    f = pl.program_id(1)

    @pl.when(blk < nb_ref[0])
    def _():
        x = x_ref[...].astype(BF16)
        gate = jnp.dot(x, wg_ref[0], preferred_element_type=F32)
        up = jnp.dot(x, wu_ref[0], preferred_element_type=F32)
        act = (gate * jax.nn.sigmoid(gate) * up).astype(BF16)
        part = jnp.dot(act, wd_ref[0], preferred_element_type=F32)

        @pl.when(f == 0)
        def _():
            o_ref[...] = part

        @pl.when(f > 0)
        def _():
            o_ref[...] = o_ref[...] + part

    @pl.when(jnp.logical_and(blk >= nb_ref[0], f == 0))
    def _():
        o_ref[...] = jnp.zeros_like(o_ref)


def _experts(slots, block_expert, n_used, w_gate, w_up, w_down, layer, *, bm):
    n_slots, d = slots.shape
    ff = w_gate.shape[3]
    tf = _pick(ff, 256)
    nblk = n_slots // bm

    def x_map(i, f, be, nb):
        return (jnp.minimum(i, nb[0] - 1), 0)

    def w_in_map(i, f, be, nb):
        live = i < nb[0]
        return (layer, be[jnp.minimum(i, nb[0] - 1)], 0, jnp.where(live, f, ff // tf - 1))

    def w_out_map(i, f, be, nb):
        live = i < nb[0]
        return (layer, be[jnp.minimum(i, nb[0] - 1)], jnp.where(live, f, ff // tf - 1), 0)

    grid_spec = pltpu.PrefetchScalarGridSpec(
        num_scalar_prefetch=2,
        grid=(nblk, ff // tf),
        in_specs=[pl.BlockSpec((bm, d), x_map),
                  pl.BlockSpec((None, 1, d, tf), w_in_map),
                  pl.BlockSpec((None, 1, d, tf), w_in_map),
                  pl.BlockSpec((None, 1, tf, d), w_out_map)],
        out_specs=pl.BlockSpec((bm, d), lambda i, f, be, nb: (i, 0)),
    )
    return pl.pallas_call(
        _experts_body,
        grid_spec=grid_spec,
        out_shape=jax.ShapeDtypeStruct((n_slots, d), F32),
        compiler_params=_cparams(("arbitrary", "arbitrary")),
        name="moe_experts",
    )(block_expert, n_used, slots, w_gate, w_up, w_down)


def _combine_body(dest_ref, y_ref, x_ref, g_ref, wt_ref, o_ref, buf_ref, sem, *, tm, per_batch):
    base = (pl.program_id(0) * per_batch + pl.program_id(1)) * tm

    def copy(r, kk):
        return pltpu.make_async_copy(y_ref.at[pl.ds(dest_ref[(base + r) * TOP_K + kk], 1)],
                                     buf_ref.at[kk, pl.ds(r, 1)], sem)

    def start(r, _):
        for kk in range(TOP_K):
            copy(r, kk).start()
        return 0

    def wait(r, _):
        for kk in range(TOP_K):
            copy(r, kk).wait()
        return 0

    lax.fori_loop(0, tm, start, 0)
    lax.fori_loop(0, tm, wait, 0)
    wt = wt_ref[0]
    f = buf_ref[0] * wt[:, 0:1] + buf_ref[1] * wt[:, 1:2]
    o_ref[0] = x_ref[0] + g_ref[0] * f


def _combine(y_slots, dest, weights, x, gate):
    b, s, d = x.shape
    tm = _pick(s, 256)
    per_batch = s // tm
    grid_spec = pltpu.PrefetchScalarGridSpec(
        num_scalar_prefetch=1,
        grid=(b, per_batch),
        in_specs=[pl.BlockSpec(memory_space=pl.ANY),
                  pl.BlockSpec((1, tm, d), lambda bb, i, dest_ref: (bb, i, 0)),
                  pl.BlockSpec((1, 1, d), lambda bb, i, dest_ref: (bb, 0, 0)),
                  pl.BlockSpec((1, tm, TOP_K), lambda bb, i, dest_ref: (bb, i, 0))],
        out_specs=pl.BlockSpec((1, tm, d), lambda bb, i, dest_ref: (bb, i, 0)),
        scratch_shapes=[pltpu.VMEM((TOP_K, tm, d), F32), pltpu.SemaphoreType.DMA(())],
    )
    return pl.pallas_call(
        functools.partial(_combine_body, tm=tm, per_batch=per_batch),
        grid_spec=grid_spec,
        out_shape=jax.ShapeDtypeStruct(x.shape, F32),
        compiler_params=_cparams(("arbitrary", "arbitrary")),
        name="moe_combine",
    )(dest, y_slots, x, gate, weights)


def _mixer_tiles(s, n_keys):
    k_base = next((t for t in (1024, 512, 256) if n_keys % t == 0), None)
    if k_base is None:
        lp = -(-n_keys // LANE) * LANE
        whole = lp if lp <= 4096 else LANE
        return {"lp": lp, "sb": (_pick(s, 256), LANE), "fox": (_pick(s, 512), whole),
                "dsa": (_pick(s, 256), whole), "idx": (_pick(s, LANE), LANE)}
    return {
        "lp": n_keys,
        "sb": (_pick(s, 256), min(k_base, 512)),
        "fox": (_pick(s, 512), k_base),
        "dsa": (_pick(s, 256), k_base),
        "idx": (_pick(s, LANE), min(k_base, 512)),
    }


def _pad_keys(a, lp):
    pad = lp - a.shape[1]
    if pad == 0:
        return a
    return jnp.concatenate([a, jnp.zeros((a.shape[0], pad) + a.shape[2:], a.dtype)], axis=1)


def _token_mixers(h, past, wts, layer, dims, rows_so_far):
    b, s, d = h.shape
    hd, sbh, dsh, kvh, ixh, ixd, fxh = (dims[k] for k in ("hd", "sb_heads", "dsa_heads", "kv_heads",
                                                          "idx_heads", "idx_dim", "fox_heads"))
    seg = wts["w_in_segments"]
    n_keys = s if past is None else past[0].shape[1] + s
    off = n_keys - s
    tiles = _mixer_tiles(s, n_keys)
    lp = tiles["lp"]
    direct = past is None and lp == s

    h_flat = h if direct else h.reshape(1, b * s, d)

    def proj(name, **kw):
        arr, start, width = seg[name]
        out = _matmul(h_flat, arr, b_cols=(layer, start, width), name="in_" + name, **kw)
        return out if direct else out.reshape(b, s, width)

    def stacked(new, idx):
        if rows_so_far is None:
            return new[None]
        return jnp.concatenate([rows_so_far[idx], new[None]], axis=0)

    def row_proj(name, idx, aux, **kw):
        if not direct:
            out = proj(name, **kw)
            return out, stacked(out, idx), None
        if rows_so_far is None:
            out, operand = proj(name, aux=aux, **kw)
            return out, out[None], operand
        stack, operand = proj(name, aux=aux, stack_on=rows_so_far[idx], **kw)
        return None, stack, operand

    norm = lambda key: [(wts[key][layer].reshape(1, hd), "const")]
    sb_q = proj("sb_q", out_dtype=BF16)
    sb_k, st_sb_k, kt_sb = row_proj("sb_k", 0, "bf16_t")
    sb_v, st_sb_v, vb_sb = row_proj("sb_v", 1, "bf16")
    ds_q = proj("ds_q", epi=_epi_head_rms, extras=norm("q_norm_dsa"), out_dtype=BF16)
    ds_k, st_ds_k, kt_ds = row_proj("ds_k", 2, "bf16_t", epi=_epi_head_rms, extras=norm("k_norm_dsa"))
    ds_v, st_ds_v, vb_ds = row_proj("ds_v", 3, "bf16")
    ix_q = proj("ix_q", out_dtype=BF16)
    small = proj("small", epi=_epi_masked_logsig,
                 extras=[(wts["small_bias"][layer], "row"), (wts["small_mask"], "row")])
    fx_q = proj("fx_q", epi=_epi_head_rms, extras=norm("q_norm_fox"), out_dtype=BF16)
    fx_k, st_fx_k, kt_fx = row_proj("fx_k", 5, "bf16_t", epi=_epi_head_rms, extras=norm("k_norm_fox"))
    fx_v, st_fx_v, vb_fx = row_proj("fx_v", 6, "bf16")
    gate = proj("gate", epi=_epi_sigmoid)
    ix_k = small[:, :, :ixd]
    ix_w = small[:, :, ixd:ixd + ixh]
    log_f = small[:, :, ixd + ixh:ixd + ixh + fxh]
    new_rows = (st_sb_k, st_sb_v, st_ds_k, st_ds_v, stacked(ix_k, 4), st_fx_k, st_fx_v, stacked(log_f, 7))

    if past is None:
        k_ix, lf_all = _pad_keys(ix_k, lp), _pad_keys(log_f, lp)
    else:
        cat = lambda pc, r: _pad_keys(jnp.concatenate([pc.reshape(pc.shape[0], pc.shape[1], -1), r], axis=1), lp)
        k_ix, lf_all = cat(past[4], ix_k), cat(past[7], log_f)
    if not direct:
        rows = (sb_k, sb_v, ds_k, ds_v, None, fx_k, fx_v, None)
        full = [None if r is None else (_pad_keys(r, lp) if past is None else cat(pc, r))
                for pc, r in zip(past or rows, rows)]
        kt_sb, kt_ds, kt_fx = (jnp.transpose(full[i].astype(BF16), (0, 2, 1)) for i in (0, 2, 5))
        vb_sb, vb_ds, vb_fx = (full[i].astype(BF16) for i in (1, 3, 6))

    tq, tk = tiles["sb"]
    o_sb = _sb_attention(sb_q, kt_sb, vb_sb, heads=sbh, hd=hd, off=off, tq=tq, tk=tk)

    d_all = _cumsum_lanes(jnp.transpose(lf_all, (0, 2, 1)))
    d_q = d_all[:, :, off:off + s, None]
    tq, tk = tiles["fox"]
    o_fx = _fox_attention(fx_q, kt_fx, vb_fx, d_q, d_all.reshape(b, fxh, 1, lp),
                          heads=fxh, hd=hd, off=off, tq=tq, tk=tk)

    topk = min(DSA_TOPK_MAX, n_keys // 4)
    zeros = jnp.zeros_like(k_ix)
    k_even = jnp.concatenate([k_ix, zeros], axis=2)
    k_odd = jnp.concatenate([zeros, k_ix], axis=2)
    tq, tk = tiles["idx"]
    score, thr = _indexer(ix_q, ix_w, k_even, k_odd, off=off, topk=topk, idx_heads=ixh, tq=tq, tk=tk)
    tq, tk = tiles["dsa"]
    o_ds = _dsa_attention(ds_q, kt_ds, vb_ds, score, thr, wts["bias_tiles"], kv_heads=kvh,
                          group=dsh // kvh, hd=hd, off=off, tq=tq, tk=tk)

    if direct:
        merged = _merge((o_sb, o_ds, o_fx), gate, wts["w_branch"], layer)
    else:
        flat = lambda a: a.reshape(1, b * s, a.shape[-1])
        merged = _merge(tuple(flat(o) for o in (o_sb, o_ds, o_fx)), flat(gate), wts["w_branch"], layer)
        merged = merged.reshape(b, s, d)
    return merged, new_rows


def _hier_moe(x, h, g2, wts, layer, dims):
    b, s, d = h.shape
    t = b * s
    ng, ne = dims["n_groups"], dims["n_experts"]
    info, counts = _router(h.reshape(t, d), wts["w_route"][layer], wts["b_route"][layer], n_groups=ng,
                           per_group=ne // ng)
    expert = info[:, 0:TOP_K].astype(I32)
    weights = info[:, TOP_K:2 * TOP_K]
    rank = info[:, 2 * TOP_K:3 * TOP_K].astype(I32)
    counts = counts[0, ng:ng + ne].astype(I32)
    bm = 512 if t * TOP_K >= 512 * ne else 256
    padded = (counts + bm - 1) // bm * bm
    pad_end = jnp.cumsum(padded)
    pad_start = pad_end - padded
    dest = (pad_start[expert] + rank).reshape(t * TOP_K)
    n_blocks = -(-(t * TOP_K) // bm) + ne
    block_expert = jnp.minimum(
        jnp.searchsorted(pad_end, jnp.arange(n_blocks, dtype=I32) * bm, side="right"), ne - 1).astype(I32)
    n_used = (pad_end[-1:] // bm).astype(I32)
    slots = _dispatch(h.reshape(t, d), dest, n_blocks * bm)
    y_slots = _experts(slots, block_expert, n_used, wts["w_exp_gate"], wts["w_exp_up"], wts["w_exp_down"], layer,
                       bm=bm)
    return _combine(y_slots, dest, weights.reshape(b, s, TOP_K), x, g2)


def _trunk(x, mod, past, wts, depth, dims):
    b, s, d = x.shape
    rows = None
    for l in range(depth):
        sh1, sc1, g1, sh2, sc2, g2 = mod[l]
        h = _norm_mod(x, wts["g_norm1"][l], sc1, sh1)
        layer_past = None if past is None else tuple(pc[l] for pc in past)
        merged, rows = _token_mixers(h, layer_past, wts, l, dims, rows)
        if s >= MM_TM:
            x = _matmul(merged, wts["w_out"], b_cols=(l, 0, d), epi=_epi_residual,
                        extras=[(x, "tile"), (g1, "batchrow")], name="out_proj")
        else:
            flat = lambda a: jnp.broadcast_to(a, (b, s, d)).reshape(1, b * s, d)
            x = _matmul(flat(merged), wts["w_out"], b_cols=(l, 0, d), epi=_epi_residual,
                        extras=[(flat(x), "tile"), (flat(g1), "tile")], name="out_proj").reshape(b, s, d)
        h = _norm_mod(x, wts["g_norm2"][l], sc2, sh2, out_dtype=F32)
        x = _hier_moe(x, h, g2, wts, l, dims)
    return x, rows


def _prepare_weights(dims, w_in, q_norm_dsa, k_norm_dsa, q_norm_fox, k_norm_fox, b_forget, w_branch, w_out,
                     bias_tiles, w_route_grp, b_route_grp, w_route_exp, b_route_exp, w_exp_gate, w_exp_up,
                     w_exp_down, g_norm1, g_norm2):
    hd, sbh, dsh, kvh, ixh, ixd, fxh, d = (dims[k] for k in ("hd", "sb_heads", "dsa_heads", "kv_heads",
                                                             "idx_heads", "idx_dim", "fox_heads", "d"))
    depth = w_in.shape[0]
    widths = [("sb_q", sbh * hd), ("sb_k", sbh * hd), ("sb_v", sbh * hd), ("ds_q", dsh * hd), ("ds_k", kvh * hd),
              ("ds_v", kvh * hd), ("ix_q", ixh * ixd), ("ix_k", ixd), ("ix_w", ixh), ("fx_q", fxh * hd),
              ("fx_k", fxh * hd), ("fx_v", fxh * hd), ("fx_f", fxh), ("gate", 3 * d)]
    cols = {}
    start = 0
    for name, w in widths:
        cols[name] = (start, start + w)
        start += w
    runs = []
    for name, (a, bnd) in cols.items():
        width = bnd - a
        if width % LANE:
            continue
        if runs and runs[-1][1] == a and (a - runs[-1][0]) % _pick(width, MM_TN) == 0:
            runs[-1][1] = bnd
            runs[-1][2].append((name, a - runs[-1][0], width))
        else:
            runs.append([a, bnd, [(name, 0, width)]])
    seg = {}
    for first, last, members in runs:
        arr = w_in[:, :, first:last].astype(BF16)
        for name, rel, width in members:
            seg[name] = (arr, rel, width)
    n_small = ixd + ixh + fxh
    small_w = -(-n_small // LANE) * LANE
    small = jnp.concatenate([w_in[:, :, cols[name][0]:cols[name][1]] for name in ("ix_k", "ix_w", "fx_f")]
                            + [jnp.zeros((depth, d, small_w - n_small), F32)], axis=2)
    seg["small"] = (small.astype(BF16), 0, small_w)
    lane = jnp.arange(small_w)
    is_f = jnp.logical_and(lane >= ixd + ixh, lane < n_small)
    small_bias = jnp.zeros((depth, 1, small_w), F32).at[:, 0, ixd + ixh:n_small].set(b_forget.astype(F32))
    ng, ne = dims["n_groups"], dims["n_experts"]
    w_route = jnp.concatenate([w_route_grp, w_route_exp, jnp.zeros((depth, d, LANE - ng - ne), F32)], axis=2)
    b_route = jnp.concatenate([b_route_grp, b_route_exp, jnp.zeros((depth, LANE - ng - ne), F32)], axis=1)
    return {
        "w_in_segments": seg,
        "small_bias": small_bias,
        "small_mask": is_f.astype(F32).reshape(1, small_w),
        "q_norm_dsa": q_norm_dsa, "k_norm_dsa": k_norm_dsa, "q_norm_fox": q_norm_fox, "k_norm_fox": k_norm_fox,
        "w_branch": w_branch.astype(BF16), "w_out": w_out.astype(BF16),
        "bias_tiles": bias_tiles,
        "w_route": w_route.astype(BF16), "b_route": b_route.astype(F32).reshape(depth, 1, LANE),
        "w_exp_gate": w_exp_gate.astype(BF16), "w_exp_up": w_exp_up.astype(BF16),
        "w_exp_down": w_exp_down.astype(BF16),
        "g_norm1": g_norm1, "g_norm2": g_norm2,
    }


def kernel(x_prompt, x_sample, c_prompt, c_sample, cache_sb_k, cache_sb_v, cache_dsa_k, cache_dsa_v, cache_dsa_kidx, cache_fox_k, cache_fox_v, cache_fox_logf, w_mod, b_mod, g_norm1, g_norm2, w_in, q_norm_dsa, k_norm_dsa, q_norm_fox, k_norm_fox, b_forget, w_branch, w_out, rel_bias, w_route_grp, b_route_grp, w_route_exp, b_route_exp, w_exp_gate, w_exp_up, w_exp_down):
    depth = w_in.shape[0]
    d = x_prompt.shape[-1]
    hd = q_norm_dsa.shape[-1]
    dims = {
        "d": d, "hd": hd,
        "sb_heads": cache_sb_k.shape[3], "dsa_heads": rel_bias.shape[1], "kv_heads": cache_dsa_k.shape[3],
        "idx_dim": cache_dsa_kidx.shape[-1], "fox_heads": cache_fox_k.shape[3],
        "n_groups": w_route_grp.shape[-1], "n_experts": w_route_exp.shape[-1],
    }
    fixed = (3 * dims["sb_heads"] * hd + dims["dsa_heads"] * hd + 2 * dims["kv_heads"] * hd + dims["idx_dim"]
             + 3 * dims["fox_heads"] * hd + dims["fox_heads"] + 3 * d)
    dims["idx_heads"] = (w_in.shape[2] - fixed) // (dims["idx_dim"] + 1)

    bias_tiles = _bias_tiles(rel_bias)
    wts = _prepare_weights(dims, w_in, q_norm_dsa, k_norm_dsa, q_norm_fox, k_norm_fox, b_forget, w_branch, w_out,
                           bias_tiles, w_route_grp, b_route_grp, w_route_exp, b_route_exp, w_exp_gate, w_exp_up,
                           w_exp_down, g_norm1, g_norm2)

    nb_p, nb_s = c_prompt.shape[0], c_sample.shape[0]
    rows = -(-(nb_p + nb_s) // 8) * 8
    c_all = jnp.concatenate([c_prompt, c_sample, jnp.zeros((rows - nb_p - nb_s, d), F32)], axis=0)[None]
    mods_p, mods_s = [], []
    for l in range(depth):
        mod = _matmul(c_all, w_mod, b_cols=(l, 0, w_mod.shape[2]), epi=_epi_bias,
                      extras=[(b_mod[l].reshape(1, -1), "row")], silu_a=True, name="adaln_mod")[0]
        mods_p.append([m[:, None, :] for m in jnp.split(mod[:nb_p], 6, axis=-1)])
        mods_s.append([m[:, None, :] for m in jnp.split(mod[nb_p:nb_p + nb_s], 6, axis=-1)])

    past = (cache_sb_k, cache_sb_v, cache_dsa_k, cache_dsa_v, cache_dsa_kidx, cache_fox_k, cache_fox_v,
            cache_fox_logf)
    y_p, rows_p = _trunk(x_prompt, mods_p, None, wts, depth, dims)
    y_s, rows_s = _trunk(x_sample, mods_s, past, wts, depth, dims)

    head_counts = (dims["sb_heads"], dims["sb_heads"], dims["kv_heads"], dims["kv_heads"], None,
                   dims["fox_heads"], dims["fox_heads"], None)

    def split_heads(rows):
        return [a if heads is None else a.reshape(a.shape[0], a.shape[1], a.shape[2], heads, hd)
                for a, heads in zip(rows, head_counts)]

    return (y_p, y_s, *split_heads(rows_p), *split_heads(rows_s))
```

```python
import functools
import math

import jax
import jax.numpy as jnp
from jax import lax
from jax.experimental import pallas as pl
from jax.experimental.pallas import tpu as pltpu

F32 = jnp.float32
BF16 = jnp.bfloat16
I32 = jnp.int32

EPS = 1e-6
CHUNK = 64
DSA_TOPK_MAX = 256
MAX_DISTANCE = 128
TOP_K = 2
LANE = 128
NEG_BIG = -1e30
LOG2E = math.log2(math.e)
F32_EXP2_UNDERFLOW = -151.0
VMEM_LIMIT = 56 * 1024 * 1024
MM_TM, MM_TN = 1024, 512


def _cparams(sem):
    return pltpu.CompilerParams(dimension_semantics=sem, vmem_limit_bytes=VMEM_LIMIT)


def _pick(n, pref):
    if n <= pref:
        return n
    t = pref
    while n % t:
        t //= 2
    return t


def _norm_mod_body(x_ref, g_ref, sc_ref, sh_ref, o_ref):
    x = x_ref[0]
    ms = jnp.mean(x * x, axis=-1, keepdims=True)
    y = x * lax.rsqrt(ms + EPS) * g_ref[...]
    o_ref[0] = (y * (1.0 + sc_ref[0]) + sh_ref[0]).astype(o_ref.dtype)


def _norm_mod(x, g, sc, sh, out_dtype=BF16):
    b, s, d = x.shape
    ts = _pick(s, 256)
    return pl.pallas_call(
        _norm_mod_body,
        grid=(b, s // ts),
        in_specs=[pl.BlockSpec((1, ts, d), lambda i, j: (i, j, 0)),
                  pl.BlockSpec((1, d), lambda i, j: (0, 0)),
                  pl.BlockSpec((1, 1, d), lambda i, j: (i, 0, 0)),
                  pl.BlockSpec((1, 1, d), lambda i, j: (i, 0, 0))],
        out_specs=pl.BlockSpec((1, ts, d), lambda i, j: (i, j, 0)),
        out_shape=jax.ShapeDtypeStruct((b, s, d), out_dtype),
        compiler_params=_cparams(("parallel", "parallel")),
        name="norm_mod",
    )(x, g.reshape(1, d), sc, sh)


def _log_sigmoid(x):
    return jnp.minimum(x, 0.0) - jnp.log1p(jnp.exp(-jnp.abs(x)))


def _epi_none(acc):
    return acc


def _epi_bias(acc, bias):
    return acc + bias


def _epi_head_rms(acc, gain):
    hd = gain.shape[-1]
    outs = []
    for c in range(acc.shape[-1] // hd):
        blk = acc[:, c * hd:(c + 1) * hd]
        ms = jnp.mean(blk * blk, axis=-1, keepdims=True)
        outs.append(blk * lax.rsqrt(ms + EPS) * gain)
    return jnp.concatenate(outs, axis=-1) if len(outs) > 1 else outs[0]


def _epi_masked_logsig(acc, bias, mask):
    return jnp.where(mask > 0.0, _log_sigmoid(acc + bias), acc)


def _epi_sigmoid(acc):
    return jax.nn.sigmoid(acc)


def _epi_residual(acc, res, gate):
    return res + gate * acc


def _mm_body(*refs, epi, n_extra, silu_a, aux, stacked):
    a_ref, b_ref = refs[0], refs[1]
    extras = refs[2:2 + n_extra]
    n_in = 2 + n_extra + (1 if stacked else 0)
    o_ref = refs[n_in]
    a = a_ref[0]
    if silu_a:
        a = a.astype(F32)
        a = a * jax.nn.sigmoid(a)
    acc = jnp.dot(a.astype(BF16), b_ref[...].astype(BF16), preferred_element_type=F32)
    vals = []
    for r in extras:
        v = r[...]
        vals.append(v[0] if v.ndim == 3 else v)
    res = epi(acc, *vals)
    if stacked:
        prev_ref = refs[n_in - 1]
        n_prev = prev_ref.shape[0]
        o_ref[0:n_prev, 0] = prev_ref[:, 0]
        o_ref[n_prev, 0] = res.astype(o_ref.dtype)
    else:
        o_ref[0] = res.astype(o_ref.dtype)
    if aux == "bf16":
        refs[n_in + 1][0] = res.astype(BF16)
    elif aux == "bf16_t":
        refs[n_in + 1][0] = res.T.astype(BF16)


def _matmul(a, b, *, epi=_epi_none, extras=(), out_dtype=F32, tm=MM_TM, tn=MM_TN, silu_a=False, aux=None,
            b_cols=None, stack_on=None, name="matmul"):
    bsz, s, k = a.shape
    tm = _pick(s, tm)
    if b_cols is None:
        n = b.shape[1]
        tn = _pick(n, tn)
        b_spec = pl.BlockSpec((k, tn), lambda bb, i, j: (0, j))
    else:
        layer, start, n = b_cols
        tn = _pick(n, tn)
        assert start % tn == 0
        b_spec = pl.BlockSpec((None, k, tn), lambda bb, i, j: (layer, 0, start // tn + j))
    in_specs = [pl.BlockSpec((1, tm, k), lambda bb, i, j: (bb, i, 0)), b_spec]
    args = [a, b]
    for arr, kind in extras:
        if kind == "row":
            in_specs.append(pl.BlockSpec((1, tn), lambda bb, i, j: (0, j)))
        elif kind == "batchrow":
            in_specs.append(pl.BlockSpec((1, 1, tn), lambda bb, i, j: (bb, 0, j)))
        elif kind == "tile":
            in_specs.append(pl.BlockSpec((1, tm, tn), lambda bb, i, j: (bb, i, j)))
        else:
            in_specs.append(pl.BlockSpec(arr.shape, lambda bb, i, j, nd=arr.ndim: (0,) * nd))
        args.append(arr)
    if stack_on is None:
        out_specs = pl.BlockSpec((1, tm, tn), lambda bb, i, j: (bb, i, j))
        out_shape = jax.ShapeDtypeStruct((bsz, s, n), out_dtype)
    else:
        n_prev = stack_on.shape[0]
        in_specs.append(pl.BlockSpec((n_prev, 1, tm, tn), lambda bb, i, j: (0, bb, i, j)))
        args.append(stack_on)
        out_specs = pl.BlockSpec((n_prev + 1, 1, tm, tn), lambda bb, i, j: (0, bb, i, j))
        out_shape = jax.ShapeDtypeStruct((n_prev + 1, bsz, s, n), out_dtype)
    if aux == "bf16":
        out_specs = [out_specs, pl.BlockSpec((1, tm, tn), lambda bb, i, j: (bb, i, j))]
        out_shape = [out_shape, jax.ShapeDtypeStruct((bsz, s, n), BF16)]
    elif aux == "bf16_t":
        out_specs = [out_specs, pl.BlockSpec((1, tn, tm), lambda bb, i, j: (bb, j, i))]
        out_shape = [out_shape, jax.ShapeDtypeStruct((bsz, n, s), BF16)]
    return pl.pallas_call(
        functools.partial(_mm_body, epi=epi, n_extra=len(extras), silu_a=silu_a, aux=aux,
                          stacked=stack_on is not None),
        grid=(bsz, s // tm, n // tn),
        in_specs=in_specs,
        out_specs=out_specs,
        out_shape=out_shape,
        compiler_params=_cparams(("parallel", "parallel", "parallel")),
        name=name,
    )(*args)


def _sb_body(q_ref, kt_ref, v_ref, o_ref, run_ref, acc_ref, *, tq, tk, cw, off, scale):
    i = pl.program_id(2)
    q_lo = off + i * tq
    q = (q_ref[0].astype(F32) * (scale * LOG2E)).astype(BF16)
    tri_r = lax.broadcasted_iota(I32, (2 * cw, cw), 0)
    tri_c = lax.broadcasted_iota(I32, (2 * cw, cw), 1)
    suffix = jnp.where(jnp.where(tri_r >= cw, tri_r - cw, tri_r) >= tri_c, 1.0, 0.0).astype(BF16)
    n_blocks = (q_lo + tq - 2) // tk + 1
    n_plain = q_lo // tk
    run_ref[...] = jnp.zeros_like(run_ref)
    acc_ref[...] = jnp.zeros_like(acc_ref)

    def tile(kb, masked):
        ks = pl.multiple_of(kb * tk, tk)
        kt = kt_ref[0, :, pl.ds(ks, tk)]
        v = v_ref[0, pl.ds(ks, tk), :]
        z_all = jnp.dot(q, kt, preferred_element_type=F32)
        run = jnp.max(run_ref[...], axis=1, keepdims=True)
        ws = [None] * (tk // cw)
        for j in reversed(range(tk // cw)):
            z = z_all[:, j * cw:(j + 1) * cw]
            log_keep = jnp.minimum(-z, 0.0) - jnp.log2(1.0 + jnp.exp2(jnp.minimum(z, -z)))
            if masked:
                row = lax.broadcasted_iota(I32, (tq, cw), 0)
                col = lax.broadcasted_iota(I32, (tq, cw), 1)
                mask = (ks + j * cw + col) < (q_lo + row)
                log_keep = jnp.where(mask, log_keep, 0.0)
            hi = log_keep.astype(BF16)
            lo = (log_keep - hi.astype(F32)).astype(BF16)
            incl = jnp.dot(jnp.concatenate([hi, lo], axis=1), suffix,
                           preferred_element_type=F32)
            w = jnp.exp2(z + incl + run)
            if masked:
                w = jnp.where(mask, w, 0.0)
            ws[j] = w.astype(BF16)
            run = run + jnp.sum(log_keep, axis=1, keepdims=True)
        w_all = jnp.concatenate(ws, axis=1) if len(ws) > 1 else ws[0]
        acc_ref[...] = acc_ref[...] + jnp.dot(w_all, v, preferred_element_type=F32)
        run_ref[...] = jnp.broadcast_to(run, run_ref.shape)

    def masked_step(n, carry):
        tile(n_blocks - 1 - n, True)
        return carry

    lax.fori_loop(0, n_blocks - n_plain, masked_step, 0)

    def alive():
        return jnp.max(run_ref[...]) > F32_EXP2_UNDERFLOW

    def cond(carry):
        kb, go = carry
        return jnp.logical_and(kb >= 0, go)

    def plain_step(carry):
        kb, _ = carry
        tile(kb, False)
        return kb - 1, alive()

    lax.while_loop(cond, plain_step, (n_plain - 1, alive()))
    o_ref[0] = acc_ref[...].astype(o_ref.dtype)


def _sb_attention(q, kt, v, *, heads, hd, off, tq, tk):
    b, s, _ = q.shape
    lp = v.shape[1]
    cw = min(tk, 256)
    return pl.pallas_call(
        functools.partial(_sb_body, tq=tq, tk=tk, cw=cw, off=off, scale=hd ** -0.5),
        grid=(b, heads, s // tq),
        in_specs=[pl.BlockSpec((1, tq, hd), lambda bb, h, i: (bb, i, h)),
                  pl.BlockSpec((1, hd, lp), lambda bb, h, i: (bb, h, 0)),
                  pl.BlockSpec((1, lp, hd), lambda bb, h, i: (bb, 0, h))],
        out_specs=pl.BlockSpec((1, tq, hd), lambda bb, h, i: (bb, i, h)),
        out_shape=jax.ShapeDtypeStruct(q.shape, BF16),
        scratch_shapes=[pltpu.VMEM((tq, LANE), F32), pltpu.VMEM((tq, hd), F32)],
        compiler_params=_cparams(("parallel", "parallel", "arbitrary")),
        name="sb_attention",
    )(q, kt, v)


def _cumsum_body(x_ref, o_ref):
    x = x_ref[0]
    n = x.shape[1]
    lane = lax.broadcasted_iota(I32, x.shape, 1)
    shift = 1
    while shift < n:
        x = x + jnp.where(lane >= shift, pltpu.roll(x, shift, axis=1), 0.0)
        shift *= 2
    o_ref[0] = x


def _cumsum_lanes(x):
    b, h, n = x.shape
    return pl.pallas_call(
        _cumsum_body,
        grid=(b,),
        in_specs=[pl.BlockSpec((1, h, n), lambda i: (i, 0, 0))],
        out_specs=pl.BlockSpec((1, h, n), lambda i: (i, 0, 0)),
        out_shape=jax.ShapeDtypeStruct(x.shape, F32),
        compiler_params=_cparams(("parallel",)),
        name="cumsum_logf",
    )(x)


def _online_softmax_step(z_cols, v, m_ref, l_ref, acc_ref, shift_extra=None):
    m_prev = m_ref[...]
    m_cur = z_cols[0]
    for z in z_cols[1:]:
        m_cur = jnp.maximum(m_cur, z)
    m_cur = jnp.max(m_cur, axis=1, keepdims=True)
    if shift_extra is not None:
        m_cur = m_cur + shift_extra
    m_next = jnp.maximum(m_prev, m_cur)
    shift = m_next if shift_extra is None else m_next - shift_extra
    ps = [jnp.exp2(z - shift) for z in z_cols]
    alpha = jnp.exp2(m_prev - m_next)
    l_new = alpha * l_ref[...]
    for p in ps:
        l_new = l_new + p
    l_ref[...] = l_new
    m_ref[...] = m_next
    p_all = jnp.concatenate([p.astype(BF16) for p in ps], axis=1) if len(ps) > 1 else ps[0].astype(BF16)
    acc_ref[...] = alpha * acc_ref[...] + jnp.dot(p_all, v, preferred_element_type=F32)


def _fox_body(q_ref, kt_ref, v_ref, dq_ref, dk_ref, o_ref, m_ref, l_ref, acc_ref, kmax_ref, *, tq, tk, off, scale):
    i = pl.program_id(2)
    q_lo = off + i * tq
    q = (q_ref[0].astype(F32) * (scale * LOG2E)).astype(BF16)
    dq = dq_ref[0, 0] * LOG2E
    n_blocks = (q_lo + tq - 1) // tk + 1
    n_plain = q_lo // tk
    n_cols = tk // LANE
    lp = kt_ref.shape[2]
    m_ref[...] = jnp.full_like(m_ref, NEG_BIG)
    l_ref[...] = jnp.zeros_like(l_ref)
    acc_ref[...] = jnp.zeros_like(acc_ref)

    @pl.when(i == 0)
    def _():
        kf = kt_ref[0].astype(F32)
        norm2 = jnp.max(jnp.sum(kf * kf, axis=0, keepdims=True), axis=1, keepdims=True)
        kmax_ref[...] = jnp.broadcast_to(jnp.sqrt(norm2), kmax_ref.shape)

    qf = q.astype(F32)
    cap = jnp.sqrt(jnp.sum(qf * qf, axis=1, keepdims=True)) * kmax_ref[0:1, 0:1] * 1.001 + 0.001 + dq

    def step(kb, masked):
        ks = pl.multiple_of(kb * tk, tk)
        kt = kt_ref[0, :, pl.ds(ks, tk)]
        v = v_ref[0, pl.ds(ks, tk), :]
        dk = dk_ref[0, 0, :, pl.ds(ks, tk)] * LOG2E
        z = jnp.dot(q, kt, preferred_element_type=F32) - dk
        if masked:
            row = lax.broadcasted_iota(I32, (tq, tk), 0)
            col = lax.broadcasted_iota(I32, (tq, tk), 1)
            z = jnp.where((ks + col) <= (q_lo + row), z, NEG_BIG)
        _online_softmax_step([z[:, c * LANE:(c + 1) * LANE] for c in range(n_cols)], v, m_ref, l_ref, acc_ref,
                             shift_extra=dq)

    def masked_step(n, carry):
        step(n_blocks - 1 - n, True)
        return carry

    lax.fori_loop(0, n_blocks - n_plain, masked_step, 0)

    def alive(kb):
        lane = lax.broadcasted_iota(I32, (1, lp), 1)
        dk_lo = jnp.min(jnp.where(lane < (kb + 1) * tk, dk_ref[0, 0] * LOG2E, jnp.inf), axis=1, keepdims=True)
        m_now = jnp.max(m_ref[...], axis=1, keepdims=True)
        return jnp.max(cap - dk_lo - m_now) > F32_EXP2_UNDERFLOW

    def cond(carry):
        kb, go = carry
        return jnp.logical_and(kb >= 0, go)

    def plain_step(carry):
        kb, _ = carry
        step(kb, False)
        return kb - 1, alive(kb - 1)

    lax.while_loop(cond, plain_step, (n_plain - 1, alive(n_plain - 1)))
    o_ref[0] = (acc_ref[...] / jnp.sum(l_ref[...], axis=1, keepdims=True)).astype(o_ref.dtype)


def _fox_attention(q, kt, v, d_q, d_k, *, heads, hd, off, tq, tk):
    b, s, _ = q.shape
    lp = v.shape[1]
    return pl.pallas_call(
        functools.partial(_fox_body, tq=tq, tk=tk, off=off, scale=hd ** -0.5),
        grid=(b, heads, s // tq),
        in_specs=[pl.BlockSpec((1, tq, hd), lambda bb, h, i: (bb, i, h)),
                  pl.BlockSpec((1, hd, lp), lambda bb, h, i: (bb, h, 0)),
                  pl.BlockSpec((1, lp, hd), lambda bb, h, i: (bb, 0, h)),
                  pl.BlockSpec((1, 1, tq, 1), lambda bb, h, i: (bb, h, i, 0)),
                  pl.BlockSpec((1, 1, 1, lp), lambda bb, h, i: (bb, h, 0, 0))],
        out_specs=pl.BlockSpec((1, tq, hd), lambda bb, h, i: (bb, i, h)),
        out_shape=jax.ShapeDtypeStruct(q.shape, BF16),
        scratch_shapes=[pltpu.VMEM((tq, LANE), F32), pltpu.VMEM((tq, LANE), F32), pltpu.VMEM((tq, hd), F32),
                        pltpu.VMEM((8, LANE), F32)],
        compiler_params=_cparams(("arbitrary", "arbitrary", "arbitrary")),
        name="fox_attention",
    )(q, kt, v, d_q, d_k)


def _sortable(x):
    u = lax.bitcast_convert_type(x, I32)
    return u ^ (lax.shift_right_arithmetic(u, 31) & 0x7FFFFFFF)


def _indexer_body(qi_ref, wi_ref, ka_ref, kb_ref, sc_ref, thr_ref, key_ref, *, tq, tk, off, topk, idx_heads, w_scale):
    i = pl.program_id(1)
    q_lo = off + i * tq
    lp = ka_ref.shape[1]
    adm_end = ((q_lo + tq - 1) // CHUNK + 1) * CHUNK
    n_adm = (adm_end + tk - 1) // tk
    n_all = lp // tk
    wi = wi_ref[0] * w_scale
    qi = qi_ref[0].astype(BF16)
    pair = ka_ref.shape[2]
    w_cols = [wi[:, j:j + 1] for j in range(idx_heads)]
    q_pairs = [qi[:, p * pair:(p + 1) * pair] for p in range(idx_heads // 2)]
    row = lax.broadcasted_iota(I32, (tq, tk), 0)
    col = lax.broadcasted_iota(I32, (tq, tk), 1)
    q_chunk = (q_lo + row) // CHUNK

    def score_step(kb, _):
        ks = pl.multiple_of(kb * tk, tk)
        k_even = ka_ref[0, pl.ds(ks, tk), :].astype(BF16)
        k_odd = kb_ref[0, pl.ds(ks, tk), :].astype(BF16)
        total = jnp.zeros((tq, tk), F32)
        for p in range(idx_heads // 2):
            s0 = lax.dot_general(q_pairs[p], k_even, (((1,), (1,)), ((), ())), preferred_element_type=F32)
            s1 = lax.dot_general(q_pairs[p], k_odd, (((1,), (1,)), ((), ())), preferred_element_type=F32)
            total = total + jnp.maximum(s0, 0.0) * w_cols[2 * p] + jnp.maximum(s1, 0.0) * w_cols[2 * p + 1]
        admissible = ((ks + col) // CHUNK) <= q_chunk
        total = jnp.where(admissible, total, -jnp.inf)
        sc_ref[0, :, pl.ds(ks, tk)] = total
        key_ref[:, pl.ds(ks, tk)] = _sortable(total)
        return 0

    lax.fori_loop(0, n_adm, score_step, 0)

    def fill_step(kb, _):
        ks = pl.multiple_of(kb * tk, tk)
        sc_ref[0, :, pl.ds(ks, tk)] = jnp.full((tq, tk), -jnp.inf, F32)
        return 0

    lax.fori_loop(n_adm, n_all, fill_step, 0)

    sign = jnp.int32(-2 ** 31)

    def count(hit_fn):
        def count_step(kb, cnt):
            ks = pl.multiple_of(kb * tk, tk)
            hit = jnp.where(hit_fn(key_ref[:, pl.ds(ks, tk)]), 1.0, 0.0)
            for c in range(tk // LANE):
                cnt = cnt + hit[:, c * LANE:(c + 1) * LANE]
            return cnt

        cnt = lax.fori_loop(0, n_adm, count_step, jnp.zeros((tq, LANE), F32))
        return jnp.sum(cnt, axis=1, keepdims=True)

    def bit_step(carry):
        n, t_bits, settled = carry
        bit = lax.shift_left(jnp.int32(1), 31 - n)
        cand_bits = t_bits | bit
        cand = cand_bits ^ sign
        cnt = count(lambda keys: keys >= cand)
        settled = jnp.maximum(settled, jnp.where(cnt == float(topk), 1.0, 0.0))
        return n + 1, jnp.where(cnt >= float(topk), cand_bits, t_bits), settled

    def bits_left(carry):
        n, _, settled = carry
        return jnp.logical_and(n < 32, jnp.min(settled) < 1.0)

    _, t_bits, _ = lax.while_loop(bits_left, bit_step,
                                  (jnp.int32(0), jnp.zeros((tq, 1), I32), jnp.zeros((tq, 1), F32)))
    t_key = t_bits ^ sign
    t_u = t_key ^ (lax.shift_right_arithmetic(t_key, 31) & 0x7FFFFFFF)
    thr = lax.bitcast_convert_type(t_u, F32)
    need = float(topk) - count(lambda keys: keys > t_key)
    surplus = jnp.where(thr > -jnp.inf, count(lambda keys: keys >= t_key) - float(topk), 0.0)
    lane = lax.broadcasted_iota(I32, (tq, LANE), 1)
    thr_ref[0] = jnp.where(lane == 0, thr, jnp.where(lane == 1, need, jnp.where(lane == 2, surplus, 0.0)))


def _indexer(qi, wi, k_even, k_odd, *, off, topk, idx_heads, tq, tk):
    b, s, _ = qi.shape
    lp = k_even.shape[1]
    idx_dim = qi.shape[2] // idx_heads
    w_scale = (idx_heads ** -0.5) * (idx_dim ** -0.5)
    return pl.pallas_call(
        functools.partial(_indexer_body, tq=tq, tk=tk, off=off, topk=topk, idx_heads=idx_heads, w_scale=w_scale),
        grid=(b, s // tq),
        in_specs=[pl.BlockSpec((1, tq, qi.shape[2]), lambda bb, i: (bb, i, 0)),
                  pl.BlockSpec((1, tq, idx_heads), lambda bb, i: (bb, i, 0)),
                  pl.BlockSpec((1, lp, k_even.shape[2]), lambda bb, i: (bb, 0, 0)),
                  pl.BlockSpec((1, lp, k_odd.shape[2]), lambda bb, i: (bb, 0, 0))],
        out_specs=[pl.BlockSpec((1, tq, lp), lambda bb, i: (bb, i, 0)),
                   pl.BlockSpec((1, tq, LANE), lambda bb, i: (bb, i, 0))],
        out_shape=[jax.ShapeDtypeStruct((b, s, lp), F32), jax.ShapeDtypeStruct((b, s, LANE), F32)],
        scratch_shapes=[pltpu.VMEM((tq, lp), I32)],
        compiler_params=_cparams(("parallel", "arbitrary")),
        name="dsa_indexer",
    )(qi, wi, k_even, k_odd)


def _bias_tile_body(rb_ref, o_ref, *, n_buckets, heads, n_off):
    half = n_buckets // 2
    exact = half // 2
    row = lax.broadcasted_iota(I32, (LANE, LANE), 0)
    col = lax.broadcasted_iota(I32, (LANE, LANE), 1)
    buckets = []
    for d in range(n_off):
        rel = d * LANE + row - col
        side = jnp.where(rel < 0, half, 0)
        a = jnp.abs(rel)
        far = exact + (jnp.log(jnp.maximum(a, 1).astype(F32) / exact)
                       / math.log(MAX_DISTANCE / exact) * (half - exact)).astype(I32)
        far = jnp.minimum(far, half - 1)
        buckets.append(side + jnp.where(a < exact, a, far))
    for h in range(heads):
        tiles = []
        for bucket in buckets:
            tile = jnp.zeros((LANE, LANE), F32)
            for bkt in range(n_buckets):
                tile = jnp.where(bucket == bkt, rb_ref[bkt, h], tile)
            tiles.append(tile)
        for d in range(n_off):
            o_ref[d, h] = tiles[d] - tiles[n_off - 1]


def _bias_tiles(rel_bias, n_off=3):
    n_buckets, heads = rel_bias.shape
    return pl.pallas_call(
        functools.partial(_bias_tile_body, n_buckets=n_buckets, heads=heads, n_off=n_off),
        in_specs=[pl.BlockSpec(memory_space=pltpu.SMEM)],
        out_specs=pl.BlockSpec(memory_space=pltpu.VMEM),
        out_shape=jax.ShapeDtypeStruct((n_off, heads, LANE, LANE), F32),
        name="dsa_bias_tiles",
    )(rel_bias)


def _dsa_body(q_ref, kt_ref, v_ref, sc_ref, sel_ref, bias_ref, o_ref, m_ref, l_ref, acc_ref, tied_ref, *, tq, tk,
              off, scale, group, hd):
    i = pl.program_id(2)
    q_lo = off + i * tq
    n_off = bias_ref.shape[0]
    adm_end = ((q_lo + tq - 1) // CHUNK + 1) * CHUNK
    n_blocks = (adm_end + tk - 1) // tk
    n_far = jnp.maximum(q_lo - LANE, 0) // tk
    n_cols = tk // LANE
    thr = sel_ref[0, :, 0:1]
    need = sel_ref[0, :, 1:2]
    any_surplus = jnp.max(sel_ref[0, :, 2:3]) > 0.0
    qs = [(q_ref[0, :, g * hd:(g + 1) * hd].astype(F32) * (scale * LOG2E)).astype(BF16) for g in range(group)]
    m_ref[...] = jnp.full_like(m_ref, NEG_BIG)
    l_ref[...] = jnp.zeros_like(l_ref)
    acc_ref[...] = jnp.zeros_like(acc_ref)
    tied_ref[...] = jnp.zeros_like(tied_ref)

    def selected(sc, ties):
        finite = jnp.abs(sc) < jnp.inf
        if not ties:
            return jnp.logical_and(sc >= thr, finite)
        tri_r = lax.broadcasted_iota(I32, (LANE, LANE), 0)
        tri_c = lax.broadcasted_iota(I32, (LANE, LANE), 1)
        before = jnp.where(tri_r < tri_c, 1.0, 0.0).astype(BF16)
        seen = jnp.max(tied_ref[...], axis=1, keepdims=True)
        cols = []
        for c in range(n_cols):
            sc_c = sc[:, c * LANE:(c + 1) * LANE]
            fin_c = finite[:, c * LANE:(c + 1) * LANE]
            tie = jnp.where(jnp.logical_and(sc_c == thr, fin_c), 1.0, 0.0)
            rank = seen + jnp.dot(tie.astype(BF16), before, preferred_element_type=F32)
            take = jnp.logical_and(tie > 0.0, rank < need)
            cols.append(jnp.logical_or(jnp.logical_and(sc_c > thr, fin_c), take))
            seen = seen + jnp.sum(tie, axis=1, keepdims=True)
        tied_ref[...] = jnp.broadcast_to(seen, tied_ref.shape)
        return jnp.concatenate(cols, axis=1) if n_cols > 1 else cols[0]

    def step(kb, near, ties):
        ks = pl.multiple_of(kb * tk, tk)
        kt = kt_ref[0, :, pl.ds(ks, tk)]
        v = v_ref[0, pl.ds(ks, tk), :]
        sc = sc_ref[0, :, pl.ds(ks, tk)]
        neg = jnp.where(selected(sc, ties), 0.0, NEG_BIG)
        for g in range(group):
            z = jnp.dot(qs[g], kt, preferred_element_type=F32) + neg
            z_cols = [z[:, c * LANE:(c + 1) * LANE] for c in range(n_cols)]
            if near:
                for c in range(n_cols):
                    parts = []
                    for r in range(0, tq, LANE):
                        d_idx = jnp.clip((q_lo + r - ks - c * LANE) // LANE, 0, n_off - 1)
                        parts.append(bias_ref[d_idx, g, 0:min(LANE, tq - r), :])
                    bias = jnp.concatenate(parts, axis=0) if len(parts) > 1 else parts[0]
                    z_cols[c] = z_cols[c] + bias * LOG2E
            _online_softmax_step(z_cols, v, m_ref.at[g], l_ref.at[g], acc_ref.at[g])

    def walk(ties):
        def far_step(kb, carry):
            step(kb, False, ties)
            return carry

        def near_step(kb, carry):
            step(kb, True, ties)
            return carry

        lax.fori_loop(0, n_far, far_step, 0)
        lax.fori_loop(n_far, n_blocks, near_step, 0)

    @pl.when(any_surplus)
    def _():
        walk(True)

    @pl.when(jnp.logical_not(any_surplus))
    def _():
        walk(False)

    for g in range(group):
        o_ref[0, :, g * hd:(g + 1) * hd] = (acc_ref[g] / jnp.sum(l_ref[g], axis=1, keepdims=True)).astype(o_ref.dtype)


def _dsa_attention(q, kt, v, score, sel_info, bias_tiles, *, kv_heads, group, hd, off, tq, tk):
    b, s, _ = q.shape
    lp = v.shape[1]
    n_off = bias_tiles.shape[0]
    return pl.pallas_call(
        functools.partial(_dsa_body, tq=tq, tk=tk, off=off, scale=hd ** -0.5, group=group, hd=hd),
        grid=(b, kv_heads, s // tq),
        in_specs=[pl.BlockSpec((1, tq, group * hd), lambda bb, h, i: (bb, i, h)),
                  pl.BlockSpec((1, hd, lp), lambda bb, h, i: (bb, h, 0)),
                  pl.BlockSpec((1, lp, hd), lambda bb, h, i: (bb, 0, h)),
                  pl.BlockSpec((1, tq, lp), lambda bb, h, i: (bb, i, 0)),
                  pl.BlockSpec((1, tq, LANE), lambda bb, h, i: (bb, i, 0)),
                  pl.BlockSpec((n_off, group, LANE, LANE), lambda bb, h, i: (0, h, 0, 0))],
        out_specs=pl.BlockSpec((1, tq, group * hd), lambda bb, h, i: (bb, i, h)),
        out_shape=jax.ShapeDtypeStruct(q.shape, BF16),
        scratch_shapes=[pltpu.VMEM((group, tq, LANE), F32), pltpu.VMEM((group, tq, LANE), F32),
                        pltpu.VMEM((group, tq, hd), F32), pltpu.VMEM((tq, LANE), F32)],
        compiler_params=_cparams(("parallel", "parallel", "arbitrary")),
        name="dsa_attention",
    )(q, kt, v, score, sel_info, bias_tiles)


def _merge_body(o0_ref, o1_ref, o2_ref, g0_ref, g1_ref, g2_ref, w_ref, out_ref):
    total = None
    for r, (o_ref, g_ref) in enumerate(((o0_ref, g0_ref), (o1_ref, g1_ref), (o2_ref, g2_ref))):
        y = jnp.dot(o_ref[0], w_ref[r], preferred_element_type=F32) * g_ref[0]
        total = y if total is None else total + y
    out_ref[0] = total.astype(out_ref.dtype)


def _merge(branches, gate, w_branch, layer):
    b, s, width = branches[0].shape
    d = w_branch.shape[3]
    tm = _pick(s, 512)
    tn = _pick(d, 512)
    nj = d // tn
    o_spec = pl.BlockSpec((1, tm, width), lambda bb, i, j: (bb, i, 0))
    g_specs = [pl.BlockSpec((1, tm, tn), lambda bb, i, j, r=r: (bb, i, r * nj + j)) for r in range(3)]
    return pl.pallas_call(
        _merge_body,
        grid=(b, s // tm, nj),
        in_specs=[o_spec, o_spec, o_spec, *g_specs,
                  pl.BlockSpec((None, 3, width, tn), lambda bb, i, j: (layer, 0, 0, j))],
        out_specs=pl.BlockSpec((1, tm, tn), lambda bb, i, j: (bb, i, j)),
        out_shape=jax.ShapeDtypeStruct((b, s, d), BF16),
        compiler_params=_cparams(("parallel", "parallel", "parallel")),
        name="branch_merge",
    )(*branches, gate, gate, gate, w_branch)


def _router_body(h_ref, w_ref, b_ref, start_ref, info_ref, cnt_ref, run_ref, *, n_groups, per_group):
    step = pl.program_id(0)

    @pl.when(step == 0)
    def _():
        run_ref[...] = start_ref[...]

    tm = h_ref.shape[0]
    logits = jnp.dot(h_ref[...].astype(BF16), w_ref[...], preferred_element_type=F32) + b_ref[...]
    lane = lax.broadcasted_iota(I32, (tm, LANE), 1).astype(F32)
    far = float(4 * LANE)
    is_grp = lane < n_groups
    g_logit = jnp.where(is_grp, logits, -jnp.inf)
    g_max = jnp.max(g_logit, axis=1, keepdims=True)
    grp = jnp.min(jnp.where(g_logit == g_max, lane, far), axis=1, keepdims=True)
    p_grp = 1.0 / jnp.sum(jnp.where(is_grp, jnp.exp(logits - g_max), 0.0), axis=1, keepdims=True)
    e_lo = n_groups + grp * per_group
    in_grp = jnp.logical_and(lane >= e_lo, lane < e_lo + per_group)
    e_logit = jnp.where(in_grp, logits, -jnp.inf)
    m1 = jnp.max(e_logit, axis=1, keepdims=True)
    i1 = jnp.min(jnp.where(e_logit == m1, lane, far), axis=1, keepdims=True)
    e_logit2 = jnp.where(lane == i1, -jnp.inf, e_logit)
    m2 = jnp.max(e_logit2, axis=1, keepdims=True)
    i2 = jnp.min(jnp.where(jnp.logical_and(e_logit2 == m2, in_grp), lane, far), axis=1, keepdims=True)
    e2 = jnp.exp(m2 - m1)
    w1 = p_grp / (1.0 + e2)
    w2 = p_grp * e2 / (1.0 + e2)
    hit1 = lane == i1
    hit2 = lane == i2
    onehot = jnp.where(jnp.logical_or(hit1, hit2), 1.0, 0.0)
    r_i = lax.broadcasted_iota(I32, (tm, tm), 0)
    c_i = lax.broadcasted_iota(I32, (tm, tm), 1)
    earlier = jnp.where(c_i < r_i, 1.0, 0.0).astype(BF16)
    before = jnp.dot(earlier, onehot.astype(BF16), preferred_element_type=F32) + run_ref[...]
    r1 = jnp.sum(jnp.where(hit1, before, 0.0), axis=1, keepdims=True)
    r2 = jnp.sum(jnp.where(hit2, before, 0.0), axis=1, keepdims=True)
    run_ref[...] = run_ref[...] + jnp.sum(onehot, axis=0, keepdims=True)
    cnt_ref[...] = run_ref[...]
    vals = (i1 - n_groups, i2 - n_groups, w1, w2, r1, r2)
    info = jnp.zeros((tm, LANE), F32)
    for pos, val in enumerate(vals):
        info = jnp.where(lane == pos, val, info)
    info_ref[...] = info


def _router(h, w_route, b_route, start_counts, *, n_groups, per_group):
    t, d = h.shape
    tm = _pick(t, 256)
    return pl.pallas_call(
        functools.partial(_router_body, n_groups=n_groups, per_group=per_group),
        grid=(t // tm,),
        in_specs=[pl.BlockSpec((tm, d), lambda i: (i, 0)),
                  pl.BlockSpec((d, LANE), lambda i: (0, 0)),
                  pl.BlockSpec((1, LANE), lambda i: (0, 0)),
                  pl.BlockSpec((1, LANE), lambda i: (0, 0))],
        out_specs=[pl.BlockSpec((tm, LANE), lambda i: (i, 0)),
                   pl.BlockSpec((1, LANE), lambda i: (0, 0))],
        out_shape=[jax.ShapeDtypeStruct((t, LANE), F32), jax.ShapeDtypeStruct((1, LANE), F32)],
        scratch_shapes=[pltpu.VMEM((1, LANE), F32)],
        compiler_params=_cparams(("arbitrary",)),
        name="moe_router",
    )(h, w_route, b_route, start_counts)


def _dispatch_body(dest_ref, h_ref, slots_in_ref, slots_ref, sem, *, tm):
    del slots_in_ref
    base = pl.program_id(0) * tm

    def copy(r, kk):
        return pltpu.make_async_copy(h_ref.at[pl.ds(r, 1)], slots_ref.at[pl.ds(dest_ref[(base + r) * TOP_K + kk], 1)], sem)

    def start(r, _):
        for kk in range(TOP_K):
            copy(r, kk).start()
        return 0

    def wait(r, _):
        for kk in range(TOP_K):
            copy(r, kk).wait()
        return 0

    lax.fori_loop(0, tm, start, 0)
    lax.fori_loop(0, tm, wait, 0)


def _dispatch(h, dest, slots):
    t, d = h.shape
    n_slots = slots.shape[0]
    tm = _pick(t, 256)
    grid_spec = pltpu.PrefetchScalarGridSpec(
        num_scalar_prefetch=1,
        grid=(t // tm,),
        in_specs=[pl.BlockSpec((tm, d), lambda i, dest_ref: (i, 0)),
                  pl.BlockSpec(memory_space=pl.ANY)],
        out_specs=pl.BlockSpec(memory_space=pl.ANY),
        scratch_shapes=[pltpu.SemaphoreType.DMA(())],
    )
    return pl.pallas_call(
        functools.partial(_dispatch_body, tm=tm),
        grid_spec=grid_spec,
        out_shape=jax.ShapeDtypeStruct((n_slots, d), h.dtype),
        input_output_aliases={2: 0},
        compiler_params=_cparams(("arbitrary",)),
        name="moe_dispatch",
    )(dest, h, slots)


def _experts_body(be_ref, nb_ref, x_ref, wg_ref, wu_ref, wd_ref, o_ref):
    blk = pl.program_id(0)
    f = pl.program_id(1)

    @pl.when(blk < nb_ref[0])
    def _():
        x = x_ref[...].astype(BF16)
        gate = jnp.dot(x, wg_ref[0], preferred_element_type=F32)
        up = jnp.dot(x, wu_ref[0], preferred_element_type=F32)
        act = (gate * jax.nn.sigmoid(gate) * up).astype(BF16)
        part = jnp.dot(act, wd_ref[0], preferred_element_type=F32)

        @pl.when(f == 0)
        def _():
            o_ref[...] = part

        @pl.when(f > 0)
        def _():
            o_ref[...] = o_ref[...] + part

    @pl.when(jnp.logical_and(blk >= nb_ref[0], f == 0))
    def _():
        o_ref[...] = jnp.zeros_like(o_ref)


def _experts(slots, block_expert, n_used, w_gate, w_up, w_down, layer, *, bm):
    n_slots, d = slots.shape
    ff = w_gate.shape[3]
    tf = _pick(ff, 256)
    nblk = n_slots // bm

    def x_map(i, f, be, nb):
        return (jnp.minimum(i, nb[0] - 1), 0)

    def w_in_map(i, f, be, nb):
        live = i < nb[0]
        return (layer, be[jnp.minimum(i, nb[0] - 1)], 0, jnp.where(live, f, ff // tf - 1))

    def w_out_map(i, f, be, nb):
        live = i < nb[0]
        return (layer, be[jnp.minimum(i, nb[0] - 1)], jnp.where(live, f, ff // tf - 1), 0)

    grid_spec = pltpu.PrefetchScalarGridSpec(
        num_scalar_prefetch=2,
        grid=(nblk, ff // tf),
        in_specs=[pl.BlockSpec((bm, d), x_map),
                  pl.BlockSpec((None, 1, d, tf), w_in_map),
                  pl.BlockSpec((None, 1, d, tf), w_in_map),
                  pl.BlockSpec((None, 1, tf, d), w_out_map)],
        out_specs=pl.BlockSpec((bm, d), lambda i, f, be, nb: (i, 0)),
    )
    return pl.pallas_call(
        _experts_body,
        grid_spec=grid_spec,
        out_shape=jax.ShapeDtypeStruct((n_slots, d), F32),
        compiler_params=_cparams(("arbitrary", "arbitrary")),
        name="moe_experts",
    )(block_expert, n_used, slots, w_gate, w_up, w_down)


def _combine_body(dest_ref, y_ref, x_ref, g_ref, wt_ref, o_ref, buf_ref, sem, *, tm, per_batch):
    base = (pl.program_id(0) * per_batch + pl.program_id(1)) * tm

    def copy(r, kk):
        return pltpu.make_async_copy(y_ref.at[pl.ds(dest_ref[(base + r) * TOP_K + kk], 1)],
                                     buf_ref.at[kk, pl.ds(r, 1)], sem)

    def start(r, _):
        for kk in range(TOP_K):
            copy(r, kk).start()
        return 0

    def wait(r, _):
        for kk in range(TOP_K):
            copy(r, kk).wait()
        return 0

    lax.fori_loop(0, tm, start, 0)
    lax.fori_loop(0, tm, wait, 0)
    wt = wt_ref[0]
    f = buf_ref[0] * wt[:, 0:1] + buf_ref[1] * wt[:, 1:2]
    o_ref[0] = x_ref[0] + g_ref[0] * f


def _combine(y_slots, dest, weights, x, gate):
    b, s, d = x.shape
    tm = _pick(s, 256)
    per_batch = s // tm
    grid_spec = pltpu.PrefetchScalarGridSpec(
        num_scalar_prefetch=1,
        grid=(b, per_batch),
        in_specs=[pl.BlockSpec(memory_space=pl.ANY),
                  pl.BlockSpec((1, tm, d), lambda bb, i, dest_ref: (bb, i, 0)),
                  pl.BlockSpec((1, 1, d), lambda bb, i, dest_ref: (bb, 0, 0)),
                  pl.BlockSpec((1, tm, TOP_K), lambda bb, i, dest_ref: (bb, i, 0))],
        out_specs=pl.BlockSpec((1, tm, d), lambda bb, i, dest_ref: (bb, i, 0)),
        scratch_shapes=[pltpu.VMEM((TOP_K, tm, d), F32), pltpu.SemaphoreType.DMA(())],
    )
    return pl.pallas_call(
        functools.partial(_combine_body, tm=tm, per_batch=per_batch),
        grid_spec=grid_spec,
        out_shape=jax.ShapeDtypeStruct(x.shape, F32),
        compiler_params=_cparams(("arbitrary", "arbitrary")),
        name="moe_combine",
    )(dest, y_slots, x, gate, weights)


def _history_body(c_ref, n_ref, o_ref, *, heads, hd, n_cache, transpose):
    j = pl.program_id(1)

    def emit(head_rows):
        for h in range(heads):
            x = head_rows(h)
            if transpose:
                o_ref[0, h * hd:(h + 1) * hd, :] = x.T.astype(BF16)
            else:
                o_ref[0, :, h * hd:(h + 1) * hd] = x.astype(BF16)

    @pl.when(j < n_cache)
    def _():
        emit(lambda h: c_ref[0, :, h, :])

    @pl.when(j >= n_cache)
    def _():
        emit(lambda h: n_ref[0, :, h * hd:(h + 1) * hd])


def _history_operand(cache, layer, new_rows, lp, *, transpose, rows=512):
    _, b, p, heads, hd = cache.shape
    w = heads * hd
    n_cache = p // rows
    n_new = (lp - p) // rows
    s = new_rows.shape[1]
    tail = jnp.concatenate([new_rows, jnp.zeros((b, lp - p - s, w), new_rows.dtype)], axis=1)
    if transpose:
        out_spec = pl.BlockSpec((1, w, rows), lambda bb, j: (bb, 0, j))
        out_shape = jax.ShapeDtypeStruct((b, w, lp), BF16)
    else:
        out_spec = pl.BlockSpec((1, rows, w), lambda bb, j: (bb, j, 0))
        out_shape = jax.ShapeDtypeStruct((b, lp, w), BF16)
    return pl.pallas_call(
        functools.partial(_history_body, heads=heads, hd=hd, n_cache=n_cache, transpose=transpose),
        grid=(b, n_cache + n_new),
        in_specs=[pl.BlockSpec((None, 1, rows, heads, hd),
                               lambda bb, j: (layer, bb, jnp.minimum(j, n_cache - 1), 0, 0)),
                  pl.BlockSpec((1, rows, w), lambda bb, j: (bb, jnp.maximum(j - n_cache, 0), 0))],
        out_specs=out_spec,
        out_shape=out_shape,
        compiler_params=_cparams(("parallel", "parallel")),
        name="history_operand",
    )(cache, tail)


def _mixer_tiles(s, n_keys, past_len=0, hist_rows=None):
    k_base = next((t for t in (1024, 512, 256) if n_keys % t == 0), None)
    if k_base is None:
        if hist_rows:
            lp = past_len + -(-(n_keys - past_len) // hist_rows) * hist_rows
            k_small = hist_rows
        else:
            lp = -(-n_keys // LANE) * LANE
            k_small = LANE
        whole = lp if lp <= 4096 else k_small
        return {"lp": lp, "sb": (_pick(s, 256), k_small), "fox": (_pick(s, 512), whole),
                "dsa": (_pick(s, 256), whole), "idx": (_pick(s, LANE), k_small)}
    return {
        "lp": n_keys,
        "sb": (_pick(s, 512), min(k_base, 512)),
        "fox": (_pick(s, 512), k_base),
        "dsa": (_pick(s, 512), k_base),
        "idx": (_pick(s, LANE), min(k_base, 512)),
    }


def _pad_keys(a, lp):
    pad = lp - a.shape[1]
    if pad == 0:
        return a
    return jnp.concatenate([a, jnp.zeros((a.shape[0], pad) + a.shape[2:], a.dtype)], axis=1)


def _token_mixers(h, past, wts, layer, dims, rows_so_far):
    b, s, d = h.shape
    hd, sbh, dsh, kvh, ixh, ixd, fxh = (dims[k] for k in ("hd", "sb_heads", "dsa_heads", "kv_heads",
                                                          "idx_heads", "idx_dim", "fox_heads"))
    seg = wts["w_in_segments"]
    past_len = 0 if past is None else past[0].shape[2]
    n_keys = past_len + s
    off = n_keys - s
    hist_rows = next((r for r in (512, 256, LANE) if past_len and past_len % r == 0), None)
    tiles = _mixer_tiles(s, n_keys, past_len, hist_rows)
    lp = tiles["lp"]
    direct = past is None and lp == s

    h_flat = h if direct else h.reshape(1, b * s, d)

    def proj(name, **kw):
        arr, start, width = seg[name]
        out = _matmul(h_flat, arr, b_cols=(layer, start, width), name="in_" + name, **kw)
        return out if direct else out.reshape(b, s, width)

    def stacked(new, idx):
        if rows_so_far is None:
            return new[None]
        return jnp.concatenate([rows_so_far[idx], new[None]], axis=0)

    def row_proj(name, idx, aux, **kw):
        if not direct:
            out = proj(name, **kw)
            return out, stacked(out, idx), None
        if rows_so_far is None:
            out, operand = proj(name, aux=aux, **kw)
            return out, out[None], operand
        stack, operand = proj(name, aux=aux, stack_on=rows_so_far[idx], **kw)
        return None, stack, operand

    norm = lambda key: [(wts[key][layer].reshape(1, hd), "const")]
    sb_q = proj("sb_q", out_dtype=BF16)
    sb_k, st_sb_k, kt_sb = row_proj("sb_k", 0, "bf16_t")
    sb_v, st_sb_v, vb_sb = row_proj("sb_v", 1, "bf16")
    ds_q = proj("ds_q", epi=_epi_head_rms, extras=norm("q_norm_dsa"), out_dtype=BF16)
    ds_k, st_ds_k, kt_ds = row_proj("ds_k", 2, "bf16_t", epi=_epi_head_rms, extras=norm("k_norm_dsa"))
    ds_v, st_ds_v, vb_ds = row_proj("ds_v", 3, "bf16")
    ix_q = proj("ix_q", out_dtype=BF16)
    small = proj("small", epi=_epi_masked_logsig,
                 extras=[(wts["small_bias"][layer], "row"), (wts["small_mask"], "row")])
    fx_q = proj("fx_q", epi=_epi_head_rms, extras=norm("q_norm_fox"), out_dtype=BF16)
    fx_k, st_fx_k, kt_fx = row_proj("fx_k", 5, "bf16_t", epi=_epi_head_rms, extras=norm("k_norm_fox"))
    fx_v, st_fx_v, vb_fx = row_proj("fx_v", 6, "bf16")
    gate = proj("gate", epi=_epi_sigmoid)
    ix_k = small[:, :, :ixd]
    ix_w = small[:, :, ixd:ixd + ixh]
    log_f = small[:, :, ixd + ixh:ixd + ixh + fxh]
    new_rows = (st_sb_k, st_sb_v, st_ds_k, st_ds_v, stacked(ix_k, 4), st_fx_k, st_fx_v, stacked(log_f, 7))

    def cat(pc, r):
        pc = pc[layer]
        return _pad_keys(jnp.concatenate([pc.reshape(pc.shape[0], pc.shape[1], -1), r], axis=1), lp)

    if past is None:
        k_ix, lf_all = _pad_keys(ix_k, lp), _pad_keys(log_f, lp)
    else:
        k_ix, lf_all = cat(past[4], ix_k), cat(past[7], log_f)
    if not direct:
        rows = (sb_k, sb_v, ds_k, ds_v, None, fx_k, fx_v, None)

        def operand(idx, transpose):
            if past is not None and hist_rows:
                return _history_operand(past[idx], layer, rows[idx], lp, transpose=transpose, rows=hist_rows)
            full = (_pad_keys(rows[idx], lp) if past is None else cat(past[idx], rows[idx])).astype(BF16)
            return jnp.transpose(full, (0, 2, 1)) if transpose else full

        kt_sb, kt_ds, kt_fx = (operand(i, True) for i in (0, 2, 5))
        vb_sb, vb_ds, vb_fx = (operand(i, False) for i in (1, 3, 6))

    tq, tk = tiles["sb"]
    o_sb = _sb_attention(sb_q, kt_sb, vb_sb, heads=sbh, hd=hd, off=off, tq=tq, tk=tk)

    d_all = _cumsum_lanes(jnp.transpose(lf_all, (0, 2, 1)))
    d_q = d_all[:, :, off:off + s, None]
    tq, tk = tiles["fox"]
    o_fx = _fox_attention(fx_q, kt_fx, vb_fx, d_q, d_all.reshape(b, fxh, 1, lp),
                          heads=fxh, hd=hd, off=off, tq=tq, tk=tk)

    topk = min(DSA_TOPK_MAX, n_keys // 4)
    zeros = jnp.zeros_like(k_ix)
    k_even = jnp.concatenate([k_ix, zeros], axis=2)
    k_odd = jnp.concatenate([zeros, k_ix], axis=2)
    tq, tk = tiles["idx"]
    score, thr = _indexer(ix_q, ix_w, k_even, k_odd, off=off, topk=topk, idx_heads=ixh, tq=tq, tk=tk)
    tq, tk = tiles["dsa"]
    o_ds = _dsa_attention(ds_q, kt_ds, vb_ds, score, thr, wts["bias_tiles"], kv_heads=kvh,
                          group=dsh // kvh, hd=hd, off=off, tq=tq, tk=tk)

    if direct:
        merged = _merge((o_sb, o_ds, o_fx), gate, wts["w_branch"], layer)
    else:
        flat = lambda a: a.reshape(1, b * s, a.shape[-1])
        merged = _merge(tuple(flat(o) for o in (o_sb, o_ds, o_fx)), flat(gate), wts["w_branch"], layer)
        merged = merged.reshape(b, s, d)
    return merged, new_rows


def _hier_moe(parts, wts, layer, dims):
    d = parts[0][0].shape[-1]
    ng, ne = dims["n_groups"], dims["n_experts"]
    counts = jnp.zeros((1, LANE), F32)
    routed = []
    for _, h, _ in parts:
        info, counts = _router(h.reshape(-1, d), wts["w_route"][layer], wts["b_route"][layer], counts,
                               n_groups=ng, per_group=ne // ng)
        routed.append(info)
    n_assign = sum(info.shape[0] for info in routed) * TOP_K
    counts = counts[0, ng:ng + ne].astype(I32)
    bm = 512 if n_assign >= 512 * ne else 256
    padded = (counts + bm - 1) // bm * bm
    pad_end = jnp.cumsum(padded)
    pad_start = pad_end - padded
    n_blocks = -(-n_assign // bm) + ne
    block_expert = jnp.minimum(
        jnp.searchsorted(pad_end, jnp.arange(n_blocks, dtype=I32) * bm, side="right"), ne - 1).astype(I32)
    n_used = (pad_end[-1:] // bm).astype(I32)
    slots = jnp.zeros((n_blocks * bm, d), F32)
    dests = []
    for (_, h, _), info in zip(parts, routed):
        expert = info[:, 0:TOP_K].astype(I32)
        rank = info[:, 2 * TOP_K:3 * TOP_K].astype(I32)
        dest = (pad_start[expert] + rank).reshape(-1)
        slots = _dispatch(h.reshape(-1, d), dest, slots)
        dests.append(dest)
    y_slots = _experts(slots, block_expert, n_used, wts["w_exp_gate"], wts["w_exp_up"], wts["w_exp_down"], layer,
                       bm=bm)
    outs = []
    for (x, _, g2), info, dest in zip(parts, routed, dests):
        b, s, _ = x.shape
        outs.append(_combine(y_slots, dest, info[:, TOP_K:2 * TOP_K].reshape(b, s, TOP_K), x, g2))
    return outs


def _mixer_sublayer(x, mod, past, wts, layer, dims, rows):
    b, s, d = x.shape
    sh1, sc1, g1, sh2, sc2, g2 = mod
    h = _norm_mod(x, wts["g_norm1"][layer], sc1, sh1)
    merged, rows = _token_mixers(h, past, wts, layer, dims, rows)
    if s >= MM_TM:
        x = _matmul(merged, wts["w_out"], b_cols=(layer, 0, d), epi=_epi_residual,
                    extras=[(x, "tile"), (g1, "batchrow")], name="out_proj")
    else:
        flat = lambda a: jnp.broadcast_to(a, (b, s, d)).reshape(1, b * s, d)
        x = _matmul(flat(merged), wts["w_out"], b_cols=(layer, 0, d), epi=_epi_residual,
                    extras=[(flat(x), "tile"), (flat(g1), "tile")], name="out_proj").reshape(b, s, d)
    h = _norm_mod(x, wts["g_norm2"][layer], sc2, sh2, out_dtype=F32)
    return (x, h, g2), rows


def _trunks(xs, mods, pasts, wts, depth, dims):
    rows = [None] * len(xs)
    for l in range(depth):
        parts = []
        for r in range(len(xs)):
            part, rows[r] = _mixer_sublayer(xs[r], mods[r][l], pasts[r], wts, l, dims, rows[r])
            parts.append(part)
        xs = _hier_moe(parts, wts, l, dims)
    return xs, rows


def _prepare_weights(dims, w_in, q_norm_dsa, k_norm_dsa, q_norm_fox, k_norm_fox, b_forget, w_branch, w_out,
                     bias_tiles, w_route_grp, b_route_grp, w_route_exp, b_route_exp, w_exp_gate, w_exp_up,
                     w_exp_down, g_norm1, g_norm2):
    hd, sbh, dsh, kvh, ixh, ixd, fxh, d = (dims[k] for k in ("hd", "sb_heads", "dsa_heads", "kv_heads",
                                                             "idx_heads", "idx_dim", "fox_heads", "d"))
    depth = w_in.shape[0]
    widths = [("sb_q", sbh * hd), ("sb_k", sbh * hd), ("sb_v", sbh * hd), ("ds_q", dsh * hd), ("ds_k", kvh * hd),
              ("ds_v", kvh * hd), ("ix_q", ixh * ixd), ("ix_k", ixd), ("ix_w", ixh), ("fx_q", fxh * hd),
              ("fx_k", fxh * hd), ("fx_v", fxh * hd), ("fx_f", fxh), ("gate", 3 * d)]
    cols = {}
    start = 0
    for name, w in widths:
        cols[name] = (start, start + w)
        start += w
    runs = []
    for name, (a, bnd) in cols.items():
        width = bnd - a
        if width % LANE:
            continue
        if runs and runs[-1][1] == a and (a - runs[-1][0]) % _pick(width, MM_TN) == 0:
            runs[-1][1] = bnd
            runs[-1][2].append((name, a - runs[-1][0], width))
        else:
            runs.append([a, bnd, [(name, 0, width)]])
    seg = {}
    for first, last, members in runs:
        arr = w_in[:, :, first:last].astype(BF16)
        for name, rel, width in members:
            seg[name] = (arr, rel, width)
    n_small = ixd + ixh + fxh
    small_w = -(-n_small // LANE) * LANE
    small = jnp.concatenate([w_in[:, :, cols[name][0]:cols[name][1]] for name in ("ix_k", "ix_w", "fx_f")]
                            + [jnp.zeros((depth, d, small_w - n_small), F32)], axis=2)
    seg["small"] = (small.astype(BF16), 0, small_w)
    lane = jnp.arange(small_w)
    is_f = jnp.logical_and(lane >= ixd + ixh, lane < n_small)
    small_bias = jnp.zeros((depth, 1, small_w), F32).at[:, 0, ixd + ixh:n_small].set(b_forget.astype(F32))
    ng, ne = dims["n_groups"], dims["n_experts"]
    w_route = jnp.concatenate([w_route_grp, w_route_exp, jnp.zeros((depth, d, LANE - ng - ne), F32)], axis=2)
    b_route = jnp.concatenate([b_route_grp, b_route_exp, jnp.zeros((depth, LANE - ng - ne), F32)], axis=1)
    return {
        "w_in_segments": seg,
        "small_bias": small_bias,
        "small_mask": is_f.astype(F32).reshape(1, small_w),
        "q_norm_dsa": q_norm_dsa, "k_norm_dsa": k_norm_dsa, "q_norm_fox": q_norm_fox, "k_norm_fox": k_norm_fox,
        "w_branch": w_branch.astype(BF16), "w_out": w_out.astype(BF16),
        "bias_tiles": bias_tiles,
        "w_route": w_route.astype(BF16), "b_route": b_route.astype(F32).reshape(depth, 1, LANE),
        "w_exp_gate": w_exp_gate.astype(BF16), "w_exp_up": w_exp_up.astype(BF16),
        "w_exp_down": w_exp_down.astype(BF16),
        "g_norm1": g_norm1, "g_norm2": g_norm2,
    }


def kernel(x_prompt, x_sample, c_prompt, c_sample, cache_sb_k, cache_sb_v, cache_dsa_k, cache_dsa_v, cache_dsa_kidx, cache_fox_k, cache_fox_v, cache_fox_logf, w_mod, b_mod, g_norm1, g_norm2, w_in, q_norm_dsa, k_norm_dsa, q_norm_fox, k_norm_fox, b_forget, w_branch, w_out, rel_bias, w_route_grp, b_route_grp, w_route_exp, b_route_exp, w_exp_gate, w_exp_up, w_exp_down):
    depth = w_in.shape[0]
    d = x_prompt.shape[-1]
    hd = q_norm_dsa.shape[-1]
    dims = {
        "d": d, "hd": hd,
        "sb_heads": cache_sb_k.shape[3], "dsa_heads": rel_bias.shape[1], "kv_heads": cache_dsa_k.shape[3],
        "idx_dim": cache_dsa_kidx.shape[-1], "fox_heads": cache_fox_k.shape[3],
        "n_groups": w_route_grp.shape[-1], "n_experts": w_route_exp.shape[-1],
    }
    fixed = (3 * dims["sb_heads"] * hd + dims["dsa_heads"] * hd + 2 * dims["kv_heads"] * hd + dims["idx_dim"]
             + 3 * dims["fox_heads"] * hd + dims["fox_heads"] + 3 * d)
    dims["idx_heads"] = (w_in.shape[2] - fixed) // (dims["idx_dim"] + 1)

    bias_tiles = _bias_tiles(rel_bias)
    wts = _prepare_weights(dims, w_in, q_norm_dsa, k_norm_dsa, q_norm_fox, k_norm_fox, b_forget, w_branch, w_out,
                           bias_tiles, w_route_grp, b_route_grp, w_route_exp, b_route_exp, w_exp_gate, w_exp_up,
                           w_exp_down, g_norm1, g_norm2)

    nb_p, nb_s = c_prompt.shape[0], c_sample.shape[0]
    rows = -(-(nb_p + nb_s) // 8) * 8
    c_all = jnp.concatenate([c_prompt, c_sample, jnp.zeros((rows - nb_p - nb_s, d), F32)], axis=0)[None]
    mods_p, mods_s = [], []
    for l in range(depth):
        mod = _matmul(c_all, w_mod, b_cols=(l, 0, w_mod.shape[2]), epi=_epi_bias,
                      extras=[(b_mod[l].reshape(1, -1), "row")], silu_a=True, name="adaln_mod")[0]
        mods_p.append([m[:, None, :] for m in jnp.split(mod[:nb_p], 6, axis=-1)])
        mods_s.append([m[:, None, :] for m in jnp.split(mod[nb_p:nb_p + nb_s], 6, axis=-1)])

    past = (cache_sb_k, cache_sb_v, cache_dsa_k, cache_dsa_v, cache_dsa_kidx, cache_fox_k, cache_fox_v,
            cache_fox_logf)
    (y_p, y_s), (rows_p, rows_s) = _trunks([x_prompt, x_sample], [mods_p, mods_s], [None, past], wts, depth, dims)

    head_counts = (dims["sb_heads"], dims["sb_heads"], dims["kv_heads"], dims["kv_heads"], None,
                   dims["fox_heads"], dims["fox_heads"], None)

    def split_heads(rows):
        return [a if heads is None else a.reshape(a.shape[0], a.shape[1], a.shape[2], heads, hd)
                for a, heads in zip(rows, head_counts)]

    return (y_p, y_s, *split_heads(rows_p), *split_heads(rows_s))
```

```python
import functools
import math

import jax
import jax.numpy as jnp
from jax import lax
from jax.experimental import pallas as pl
from jax.experimental.pallas import tpu as pltpu

F32 = jnp.float32
BF16 = jnp.bfloat16
I32 = jnp.int32

EPS = 1e-6
CHUNK = 64
DSA_TOPK_MAX = 256
MAX_DISTANCE = 128
TOP_K = 2
LANE = 128
NEG_BIG = -1e30
LOG2E = math.log2(math.e)
F32_EXP2_UNDERFLOW = -151.0
VMEM_LIMIT = 56 * 1024 * 1024
MM_TM, MM_TN = 1024, 512


def _cparams(sem):
    return pltpu.CompilerParams(dimension_semantics=sem, vmem_limit_bytes=VMEM_LIMIT)


def _pick(n, pref):
    if n <= pref:
        return n
    t = pref
    while n % t:
        t //= 2
    return t


def _norm_mod_body(x_ref, g_ref, sc_ref, sh_ref, o_ref):
    x = x_ref[0]
    ms = jnp.mean(x * x, axis=-1, keepdims=True)
    y = x * lax.rsqrt(ms + EPS) * g_ref[...]
    o_ref[0] = (y * (1.0 + sc_ref[0]) + sh_ref[0]).astype(o_ref.dtype)


def _norm_mod(x, g, sc, sh, out_dtype=BF16):
    b, s, d = x.shape
    ts = _pick(s, 256)
    return pl.pallas_call(
        _norm_mod_body,
        grid=(b, s // ts),
        in_specs=[pl.BlockSpec((1, ts, d), lambda i, j: (i, j, 0)),
                  pl.BlockSpec((1, d), lambda i, j: (0, 0)),
                  pl.BlockSpec((1, 1, d), lambda i, j: (i, 0, 0)),
                  pl.BlockSpec((1, 1, d), lambda i, j: (i, 0, 0))],
        out_specs=pl.BlockSpec((1, ts, d), lambda i, j: (i, j, 0)),
        out_shape=jax.ShapeDtypeStruct((b, s, d), out_dtype),
        compiler_params=_cparams(("parallel", "parallel")),
        name="norm_mod",
    )(x, g.reshape(1, d), sc, sh)


def _log_sigmoid(x):
    return jnp.minimum(x, 0.0) - jnp.log1p(jnp.exp(-jnp.abs(x)))


def _epi_none(acc):
    return acc


def _epi_bias(acc, bias):
    return acc + bias


def _epi_head_rms(acc, gain):
    hd = gain.shape[-1]
    outs = []
    for c in range(acc.shape[-1] // hd):
        blk = acc[:, c * hd:(c + 1) * hd]
        ms = jnp.mean(blk * blk, axis=-1, keepdims=True)
        outs.append(blk * lax.rsqrt(ms + EPS) * gain)
    return jnp.concatenate(outs, axis=-1) if len(outs) > 1 else outs[0]


def _epi_masked_logsig(acc, bias, mask):
    return jnp.where(mask > 0.0, _log_sigmoid(acc + bias), acc)


def _epi_sigmoid(acc):
    return jax.nn.sigmoid(acc)


def _epi_residual(acc, res, gate):
    return res + gate * acc


def _mm_body(*refs, epi, n_extra, silu_a, aux, stacked, b_transposed):
    a_ref, b_ref = refs[0], refs[1]
    extras = refs[2:2 + n_extra]
    n_in = 2 + n_extra + (1 if stacked else 0)
    o_ref = refs[n_in]
    a = a_ref[0]
    if silu_a:
        a = a.astype(F32)
        a = a * jax.nn.sigmoid(a)
    contract_b = 1 if b_transposed else 0
    acc = lax.dot_general(a.astype(BF16), b_ref[...].astype(BF16), (((1,), (contract_b,)), ((), ())),
                          preferred_element_type=F32)
    vals = []
    for r in extras:
        v = r[...]
        vals.append(v[0] if v.ndim == 3 else v)
    res = epi(acc, *vals)
    if stacked:
        prev_ref = refs[n_in - 1]
        n_prev = prev_ref.shape[0]
        o_ref[0:n_prev, 0] = prev_ref[:, 0]
        o_ref[n_prev, 0] = res.astype(o_ref.dtype)
    else:
        o_ref[0] = res.astype(o_ref.dtype)
    if aux == "bf16":
        refs[n_in + 1][0] = res.astype(BF16)
    elif aux == "bf16_t":
        refs[n_in + 1][0] = res.T.astype(BF16)


def _matmul(a, b, *, epi=_epi_none, extras=(), out_dtype=F32, tm=MM_TM, tn=MM_TN, silu_a=False, aux=None,
            b_cols=None, b_rows=None, stack_on=None, name="matmul"):
    bsz, s, k = a.shape
    tm = _pick(s, tm)
    if b_rows is not None:
        layer, start, n = b_rows
        tn = _pick(n, tn)
        assert start % tn == 0
        b_spec = pl.BlockSpec((None, tn, k), lambda bb, i, j: (layer, start // tn + j, 0))
    elif b_cols is None:
        n = b.shape[1]
        tn = _pick(n, tn)
        b_spec = pl.BlockSpec((k, tn), lambda bb, i, j: (0, j))
    else:
        layer, start, n = b_cols
        tn = _pick(n, tn)
        assert start % tn == 0
        b_spec = pl.BlockSpec((None, k, tn), lambda bb, i, j: (layer, 0, start // tn + j))
    in_specs = [pl.BlockSpec((1, tm, k), lambda bb, i, j: (bb, i, 0)), b_spec]
    args = [a, b]
    for arr, kind in extras:
        if kind == "row":
            in_specs.append(pl.BlockSpec((1, tn), lambda bb, i, j: (0, j)))
        elif kind == "batchrow":
            in_specs.append(pl.BlockSpec((1, 1, tn), lambda bb, i, j: (bb, 0, j)))
        elif kind == "tile":
            in_specs.append(pl.BlockSpec((1, tm, tn), lambda bb, i, j: (bb, i, j)))
        else:
            in_specs.append(pl.BlockSpec(arr.shape, lambda bb, i, j, nd=arr.ndim: (0,) * nd))
        args.append(arr)
    if stack_on is None:
        out_specs = pl.BlockSpec((1, tm, tn), lambda bb, i, j: (bb, i, j))
        out_shape = jax.ShapeDtypeStruct((bsz, s, n), out_dtype)
    else:
        n_prev = stack_on.shape[0]
        in_specs.append(pl.BlockSpec((n_prev, 1, tm, tn), lambda bb, i, j: (0, bb, i, j)))
        args.append(stack_on)
        out_specs = pl.BlockSpec((n_prev + 1, 1, tm, tn), lambda bb, i, j: (0, bb, i, j))
        out_shape = jax.ShapeDtypeStruct((n_prev + 1, bsz, s, n), out_dtype)
    if aux == "bf16":
        out_specs = [out_specs, pl.BlockSpec((1, tm, tn), lambda bb, i, j: (bb, i, j))]
        out_shape = [out_shape, jax.ShapeDtypeStruct((bsz, s, n), BF16)]
    elif aux == "bf16_t":
        out_specs = [out_specs, pl.BlockSpec((1, tn, tm), lambda bb, i, j: (bb, j, i))]
        out_shape = [out_shape, jax.ShapeDtypeStruct((bsz, n, s), BF16)]
    return pl.pallas_call(
        functools.partial(_mm_body, epi=epi, n_extra=len(extras), silu_a=silu_a, aux=aux,
                          stacked=stack_on is not None, b_transposed=b_rows is not None),
        grid=(bsz, s // tm, n // tn),
        in_specs=in_specs,
        out_specs=out_specs,
        out_shape=out_shape,
        compiler_params=_cparams(("parallel", "parallel", "parallel")),
        name=name,
    )(*args)


def _sb_body(q_ref, kt_ref, v_ref, o_ref, run_ref, acc_ref, *, tq, tk, cw, off, scale):
    i = pl.program_id(2)
    q_lo = off + i * tq
    q = (q_ref[0].astype(F32) * (scale * LOG2E)).astype(BF16)
    tri_r = lax.broadcasted_iota(I32, (2 * cw, cw), 0)
    tri_c = lax.broadcasted_iota(I32, (2 * cw, cw), 1)
    suffix = jnp.where(jnp.where(tri_r >= cw, tri_r - cw, tri_r) >= tri_c, 1.0, 0.0).astype(BF16)
    n_blocks = (q_lo + tq - 2) // tk + 1
    n_plain = q_lo // tk
    run_ref[...] = jnp.zeros_like(run_ref)
    acc_ref[...] = jnp.zeros_like(acc_ref)

    def tile(kb, masked):
        ks = pl.multiple_of(kb * tk, tk)
        kt = kt_ref[0, :, pl.ds(ks, tk)]
        v = v_ref[0, pl.ds(ks, tk), :]
        z_all = jnp.dot(q, kt, preferred_element_type=F32)
        run = jnp.max(run_ref[...], axis=1, keepdims=True)
        ws = [None] * (tk // cw)
        for j in reversed(range(tk // cw)):
            z = z_all[:, j * cw:(j + 1) * cw]
            log_keep = jnp.minimum(-z, 0.0) - jnp.log2(1.0 + jnp.exp2(jnp.minimum(z, -z)))
            if masked:
                row = lax.broadcasted_iota(I32, (tq, cw), 0)
                col = lax.broadcasted_iota(I32, (tq, cw), 1)
                mask = (ks + j * cw + col) < (q_lo + row)
                log_keep = jnp.where(mask, log_keep, 0.0)
            hi = log_keep.astype(BF16)
            lo = (log_keep - hi.astype(F32)).astype(BF16)
            incl = jnp.dot(jnp.concatenate([hi, lo], axis=1), suffix,
                           preferred_element_type=F32)
            w = jnp.exp2(z + incl + run)
            if masked:
                w = jnp.where(mask, w, 0.0)
            ws[j] = w.astype(BF16)
            run = run + jnp.sum(log_keep, axis=1, keepdims=True)
        w_all = jnp.concatenate(ws, axis=1) if len(ws) > 1 else ws[0]
        acc_ref[...] = acc_ref[...] + jnp.dot(w_all, v, preferred_element_type=F32)
        run_ref[...] = jnp.broadcast_to(run, run_ref.shape)

    def masked_step(n, carry):
        tile(n_blocks - 1 - n, True)
        return carry

    lax.fori_loop(0, n_blocks - n_plain, masked_step, 0)

    def alive():
        return jnp.max(run_ref[...]) > F32_EXP2_UNDERFLOW

    def cond(carry):
        kb, go = carry
        return jnp.logical_and(kb >= 0, go)

    def plain_step(carry):
        kb, _ = carry
        tile(kb, False)
        return kb - 1, alive()

    lax.while_loop(cond, plain_step, (n_plain - 1, alive()))
    o_ref[0] = acc_ref[...].astype(o_ref.dtype)


def _sb_attention(q, kt, v, *, heads, hd, off, tq, tk):
    b, s, _ = q.shape
    lp = v.shape[1]
    cw = min(tk, 256)
    return pl.pallas_call(
        functools.partial(_sb_body, tq=tq, tk=tk, cw=cw, off=off, scale=hd ** -0.5),
        grid=(b, heads, s // tq),
        in_specs=[pl.BlockSpec((1, tq, hd), lambda bb, h, i: (bb, i, h)),
                  pl.BlockSpec((1, hd, lp), lambda bb, h, i: (bb, h, 0)),
                  pl.BlockSpec((1, lp, hd), lambda bb, h, i: (bb, 0, h))],
        out_specs=pl.BlockSpec((1, tq, hd), lambda bb, h, i: (bb, i, h)),
        out_shape=jax.ShapeDtypeStruct(q.shape, BF16),
        scratch_shapes=[pltpu.VMEM((tq, LANE), F32), pltpu.VMEM((tq, hd), F32)],
        compiler_params=_cparams(("parallel", "parallel", "arbitrary")),
        name="sb_attention",
    )(q, kt, v)


def _cumsum_body(x_ref, o_ref):
    x = x_ref[0]
    n = x.shape[1]
    lane = lax.broadcasted_iota(I32, x.shape, 1)
    shift = 1
    while shift < n:
        x = x + jnp.where(lane >= shift, pltpu.roll(x, shift, axis=1), 0.0)
        shift *= 2
    o_ref[0] = x


def _cumsum_lanes(x):
    b, h, n = x.shape
    return pl.pallas_call(
        _cumsum_body,
        grid=(b,),
        in_specs=[pl.BlockSpec((1, h, n), lambda i: (i, 0, 0))],
        out_specs=pl.BlockSpec((1, h, n), lambda i: (i, 0, 0)),
        out_shape=jax.ShapeDtypeStruct(x.shape, F32),
        compiler_params=_cparams(("parallel",)),
        name="cumsum_logf",
    )(x)


def _online_softmax_step(z_cols, v, m_ref, l_ref, acc_ref, shift_extra=None):
    m_prev = m_ref[...]
    m_cur = z_cols[0]
    for z in z_cols[1:]:
        m_cur = jnp.maximum(m_cur, z)
    m_cur = jnp.max(m_cur, axis=1, keepdims=True)
    if shift_extra is not None:
        m_cur = m_cur + shift_extra
    m_next = jnp.maximum(m_prev, m_cur)
    shift = m_next if shift_extra is None else m_next - shift_extra
    ps = [jnp.exp2(z - shift) for z in z_cols]
    alpha = jnp.exp2(m_prev - m_next)
    l_new = alpha * l_ref[...]
    for p in ps:
        l_new = l_new + p
    l_ref[...] = l_new
    m_ref[...] = m_next
    p_all = jnp.concatenate([p.astype(BF16) for p in ps], axis=1) if len(ps) > 1 else ps[0].astype(BF16)
    acc_ref[...] = alpha * acc_ref[...] + jnp.dot(p_all, v, preferred_element_type=F32)


def _fox_body(q_ref, kt_ref, v_ref, dq_ref, dk_ref, o_ref, m_ref, l_ref, acc_ref, kmax_ref, *, tq, tk, off, scale):
    i = pl.program_id(2)
    q_lo = off + i * tq
    q = (q_ref[0].astype(F32) * (scale * LOG2E)).astype(BF16)
    dq = dq_ref[0, 0] * LOG2E
    n_blocks = (q_lo + tq - 1) // tk + 1
    n_plain = q_lo // tk
    n_cols = tk // LANE
    lp = kt_ref.shape[2]
    m_ref[...] = jnp.full_like(m_ref, NEG_BIG)
    l_ref[...] = jnp.zeros_like(l_ref)
    acc_ref[...] = jnp.zeros_like(acc_ref)

    @pl.when(i == 0)
    def _():
        kf = kt_ref[0].astype(F32)
        norm2 = jnp.max(jnp.sum(kf * kf, axis=0, keepdims=True), axis=1, keepdims=True)
        kmax_ref[...] = jnp.broadcast_to(jnp.sqrt(norm2), kmax_ref.shape)

    qf = q.astype(F32)
    cap = jnp.sqrt(jnp.sum(qf * qf, axis=1, keepdims=True)) * kmax_ref[0:1, 0:1] * 1.001 + 0.001 + dq

    def step(kb, masked):
        ks = pl.multiple_of(kb * tk, tk)
        kt = kt_ref[0, :, pl.ds(ks, tk)]
        v = v_ref[0, pl.ds(ks, tk), :]
        dk = dk_ref[0, 0, :, pl.ds(ks, tk)] * LOG2E
        z = jnp.dot(q, kt, preferred_element_type=F32) - dk
        if masked:
            row = lax.broadcasted_iota(I32, (tq, tk), 0)
            col = lax.broadcasted_iota(I32, (tq, tk), 1)
            z = jnp.where((ks + col) <= (q_lo + row), z, NEG_BIG)
        _online_softmax_step([z[:, c * LANE:(c + 1) * LANE] for c in range(n_cols)], v, m_ref, l_ref, acc_ref,
                             shift_extra=dq)

    def masked_step(n, carry):
        step(n_blocks - 1 - n, True)
        return carry

    lax.fori_loop(0, n_blocks - n_plain, masked_step, 0)

    def alive(kb):
        lane = lax.broadcasted_iota(I32, (1, lp), 1)
        dk_lo = jnp.min(jnp.where(lane < (kb + 1) * tk, dk_ref[0, 0] * LOG2E, jnp.inf), axis=1, keepdims=True)
        m_now = jnp.max(m_ref[...], axis=1, keepdims=True)
        return jnp.max(cap - dk_lo - m_now) > F32_EXP2_UNDERFLOW

    def cond(carry):
        kb, go = carry
        return jnp.logical_and(kb >= 0, go)

    def plain_step(carry):
        kb, _ = carry
        step(kb, False)
        return kb - 1, alive(kb - 1)

    lax.while_loop(cond, plain_step, (n_plain - 1, alive(n_plain - 1)))
    o_ref[0] = (acc_ref[...] / jnp.sum(l_ref[...], axis=1, keepdims=True)).astype(o_ref.dtype)


def _fox_attention(q, kt, v, d_q, d_k, *, heads, hd, off, tq, tk):
    b, s, _ = q.shape
    lp = v.shape[1]
    return pl.pallas_call(
        functools.partial(_fox_body, tq=tq, tk=tk, off=off, scale=hd ** -0.5),
        grid=(b, heads, s // tq),
        in_specs=[pl.BlockSpec((1, tq, hd), lambda bb, h, i: (bb, i, h)),
                  pl.BlockSpec((1, hd, lp), lambda bb, h, i: (bb, h, 0)),
                  pl.BlockSpec((1, lp, hd), lambda bb, h, i: (bb, 0, h)),
                  pl.BlockSpec((1, 1, tq, 1), lambda bb, h, i: (bb, h, i, 0)),
                  pl.BlockSpec((1, 1, 1, lp), lambda bb, h, i: (bb, h, 0, 0))],
        out_specs=pl.BlockSpec((1, tq, hd), lambda bb, h, i: (bb, i, h)),
        out_shape=jax.ShapeDtypeStruct(q.shape, BF16),
        scratch_shapes=[pltpu.VMEM((tq, LANE), F32), pltpu.VMEM((tq, LANE), F32), pltpu.VMEM((tq, hd), F32),
                        pltpu.VMEM((8, LANE), F32)],
        compiler_params=_cparams(("arbitrary", "arbitrary", "arbitrary")),
        name="fox_attention",
    )(q, kt, v, d_q, d_k)


def _sortable(x):
    u = lax.bitcast_convert_type(x, I32)
    return u ^ (lax.shift_right_arithmetic(u, 31) & 0x7FFFFFFF)


def _indexer_body(qi_ref, wi_ref, ka_ref, kb_ref, sc_ref, thr_ref, key_ref, *, tq, tk, off, topk, idx_heads, w_scale):
    i = pl.program_id(1)
    q_lo = off + i * tq
    lp = ka_ref.shape[1]
    adm_end = ((q_lo + tq - 1) // CHUNK + 1) * CHUNK
    n_adm = (adm_end + tk - 1) // tk
    n_all = lp // tk
    wi = wi_ref[0] * w_scale
    qi = qi_ref[0].astype(BF16)
    pair = ka_ref.shape[2]
    w_cols = [wi[:, j:j + 1] for j in range(idx_heads)]
    q_pairs = [qi[:, p * pair:(p + 1) * pair] for p in range(idx_heads // 2)]
    row = lax.broadcasted_iota(I32, (tq, tk), 0)
    col = lax.broadcasted_iota(I32, (tq, tk), 1)
    q_chunk = (q_lo + row) // CHUNK

    def score_step(kb, _):
        ks = pl.multiple_of(kb * tk, tk)
        k_even = ka_ref[0, pl.ds(ks, tk), :].astype(BF16)
        k_odd = kb_ref[0, pl.ds(ks, tk), :].astype(BF16)
        total = jnp.zeros((tq, tk), F32)
        for p in range(idx_heads // 2):
            s0 = lax.dot_general(q_pairs[p], k_even, (((1,), (1,)), ((), ())), preferred_element_type=F32)
            s1 = lax.dot_general(q_pairs[p], k_odd, (((1,), (1,)), ((), ())), preferred_element_type=F32)
            total = total + jnp.maximum(s0, 0.0) * w_cols[2 * p] + jnp.maximum(s1, 0.0) * w_cols[2 * p + 1]
        admissible = ((ks + col) // CHUNK) <= q_chunk
        total = jnp.where(admissible, total, -jnp.inf)
        sc_ref[0, :, pl.ds(ks, tk)] = total
        key_ref[:, pl.ds(ks, tk)] = _sortable(total)
        return 0

    lax.fori_loop(0, n_adm, score_step, 0)

    def fill_step(kb, _):
        ks = pl.multiple_of(kb * tk, tk)
        sc_ref[0, :, pl.ds(ks, tk)] = jnp.full((tq, tk), -jnp.inf, F32)
        return 0

    lax.fori_loop(n_adm, n_all, fill_step, 0)

    sign = jnp.int32(-2 ** 31)

    def count(hit_fn):
        def count_step(kb, cnt):
            ks = pl.multiple_of(kb * tk, tk)
            hit = jnp.where(hit_fn(key_ref[:, pl.ds(ks, tk)]), 1.0, 0.0)
            for c in range(tk // LANE):
                cnt = cnt + hit[:, c * LANE:(c + 1) * LANE]
            return cnt

        cnt = lax.fori_loop(0, n_adm, count_step, jnp.zeros((tq, LANE), F32))
        return jnp.sum(cnt, axis=1, keepdims=True)

    def bit_step(carry):
        n, t_bits, settled = carry
        bit = lax.shift_left(jnp.int32(1), 31 - n)
        cand_bits = t_bits | bit
        cand = cand_bits ^ sign
        cnt = count(lambda keys: keys >= cand)
        settled = jnp.maximum(settled, jnp.where(cnt == float(topk), 1.0, 0.0))
        return n + 1, jnp.where(cnt >= float(topk), cand_bits, t_bits), settled

    def bits_left(carry):
        n, _, settled = carry
        return jnp.logical_and(n < 32, jnp.min(settled) < 1.0)

    _, t_bits, _ = lax.while_loop(bits_left, bit_step,
                                  (jnp.int32(0), jnp.zeros((tq, 1), I32), jnp.zeros((tq, 1), F32)))
    t_key = t_bits ^ sign
    t_u = t_key ^ (lax.shift_right_arithmetic(t_key, 31) & 0x7FFFFFFF)
    thr = lax.bitcast_convert_type(t_u, F32)
    need = float(topk) - count(lambda keys: keys > t_key)
    surplus = jnp.where(thr > -jnp.inf, count(lambda keys: keys >= t_key) - float(topk), 0.0)
    lane = lax.broadcasted_iota(I32, (tq, LANE), 1)
    thr_ref[0] = jnp.where(lane == 0, thr, jnp.where(lane == 1, need, jnp.where(lane == 2, surplus, 0.0)))


def _indexer(qi, wi, k_even, k_odd, *, off, topk, idx_heads, tq, tk):
    b, s, _ = qi.shape
    lp = k_even.shape[1]
    idx_dim = qi.shape[2] // idx_heads
    w_scale = (idx_heads ** -0.5) * (idx_dim ** -0.5)
    return pl.pallas_call(
        functools.partial(_indexer_body, tq=tq, tk=tk, off=off, topk=topk, idx_heads=idx_heads, w_scale=w_scale),
        grid=(b, s // tq),
        in_specs=[pl.BlockSpec((1, tq, qi.shape[2]), lambda bb, i: (bb, i, 0)),
                  pl.BlockSpec((1, tq, idx_heads), lambda bb, i: (bb, i, 0)),
                  pl.BlockSpec((1, lp, k_even.shape[2]), lambda bb, i: (bb, 0, 0)),
                  pl.BlockSpec((1, lp, k_odd.shape[2]), lambda bb, i: (bb, 0, 0))],
        out_specs=[pl.BlockSpec((1, tq, lp), lambda bb, i: (bb, i, 0)),
                   pl.BlockSpec((1, tq, LANE), lambda bb, i: (bb, i, 0))],
        out_shape=[jax.ShapeDtypeStruct((b, s, lp), F32), jax.ShapeDtypeStruct((b, s, LANE), F32)],
        scratch_shapes=[pltpu.VMEM((tq, lp), I32)],
        compiler_params=_cparams(("parallel", "arbitrary")),
        name="dsa_indexer",
    )(qi, wi, k_even, k_odd)


def _bias_tile_body(rb_ref, o_ref, *, n_buckets, heads, n_off):
    half = n_buckets // 2
    exact = half // 2
    row = lax.broadcasted_iota(I32, (LANE, LANE), 0)
    col = lax.broadcasted_iota(I32, (LANE, LANE), 1)
    buckets = []
    for d in range(n_off):
        rel = d * LANE + row - col
        side = jnp.where(rel < 0, half, 0)
        a = jnp.abs(rel)
        far = exact + (jnp.log(jnp.maximum(a, 1).astype(F32) / exact)
                       / math.log(MAX_DISTANCE / exact) * (half - exact)).astype(I32)
        far = jnp.minimum(far, half - 1)
        buckets.append(side + jnp.where(a < exact, a, far))
    for h in range(heads):
        tiles = []
        for bucket in buckets:
            tile = jnp.zeros((LANE, LANE), F32)
            for bkt in range(n_buckets):
                tile = jnp.where(bucket == bkt, rb_ref[bkt, h], tile)
            tiles.append(tile)
        for d in range(n_off):
            o_ref[d, h] = tiles[d] - tiles[n_off - 1]


def _bias_tiles(rel_bias, n_off=3):
    n_buckets, heads = rel_bias.shape
    return pl.pallas_call(
        functools.partial(_bias_tile_body, n_buckets=n_buckets, heads=heads, n_off=n_off),
        in_specs=[pl.BlockSpec(memory_space=pltpu.SMEM)],
        out_specs=pl.BlockSpec(memory_space=pltpu.VMEM),
        out_shape=jax.ShapeDtypeStruct((n_off, heads, LANE, LANE), F32),
        name="dsa_bias_tiles",
    )(rel_bias)


def _dsa_body(q_ref, kt_ref, v_ref, sc_ref, sel_ref, bias_ref, o_ref, m_ref, l_ref, acc_ref, tied_ref, *, tq, tk,
              off, scale, group, hd):
    i = pl.program_id(2)
    q_lo = off + i * tq
    n_off = bias_ref.shape[0]
    adm_end = ((q_lo + tq - 1) // CHUNK + 1) * CHUNK
    n_blocks = (adm_end + tk - 1) // tk
    n_far = jnp.maximum(q_lo - LANE, 0) // tk
    n_cols = tk // LANE
    thr = sel_ref[0, :, 0:1]
    need = sel_ref[0, :, 1:2]
    any_surplus = jnp.max(sel_ref[0, :, 2:3]) > 0.0
    qs = [(q_ref[0, :, g * hd:(g + 1) * hd].astype(F32) * (scale * LOG2E)).astype(BF16) for g in range(group)]
    m_ref[...] = jnp.full_like(m_ref, NEG_BIG)
    l_ref[...] = jnp.zeros_like(l_ref)
    acc_ref[...] = jnp.zeros_like(acc_ref)
    tied_ref[...] = jnp.zeros_like(tied_ref)

    def selected(sc, ties):
        finite = jnp.abs(sc) < jnp.inf
        if not ties:
            return jnp.logical_and(sc >= thr, finite)
        tri_r = lax.broadcasted_iota(I32, (LANE, LANE), 0)
        tri_c = lax.broadcasted_iota(I32, (LANE, LANE), 1)
        before = jnp.where(tri_r < tri_c, 1.0, 0.0).astype(BF16)
        seen = jnp.max(tied_ref[...], axis=1, keepdims=True)
        cols = []
        for c in range(n_cols):
            sc_c = sc[:, c * LANE:(c + 1) * LANE]
            fin_c = finite[:, c * LANE:(c + 1) * LANE]
            tie = jnp.where(jnp.logical_and(sc_c == thr, fin_c), 1.0, 0.0)
            rank = seen + jnp.dot(tie.astype(BF16), before, preferred_element_type=F32)
            take = jnp.logical_and(tie > 0.0, rank < need)
            cols.append(jnp.logical_or(jnp.logical_and(sc_c > thr, fin_c), take))
            seen = seen + jnp.sum(tie, axis=1, keepdims=True)
        tied_ref[...] = jnp.broadcast_to(seen, tied_ref.shape)
        return jnp.concatenate(cols, axis=1) if n_cols > 1 else cols[0]

    def step(kb, near, ties):
        ks = pl.multiple_of(kb * tk, tk)
        kt = kt_ref[0, :, pl.ds(ks, tk)]
        v = v_ref[0, pl.ds(ks, tk), :]
        sc = sc_ref[0, :, pl.ds(ks, tk)]
        neg = jnp.where(selected(sc, ties), 0.0, NEG_BIG)
        for g in range(group):
            z = jnp.dot(qs[g], kt, preferred_element_type=F32) + neg
            z_cols = [z[:, c * LANE:(c + 1) * LANE] for c in range(n_cols)]
            if near:
                for c in range(n_cols):
                    parts = []
                    for r in range(0, tq, LANE):
                        d_idx = jnp.clip((q_lo + r - ks - c * LANE) // LANE, 0, n_off - 1)
                        parts.append(bias_ref[d_idx, g, 0:min(LANE, tq - r), :])
                    bias = jnp.concatenate(parts, axis=0) if len(parts) > 1 else parts[0]
                    z_cols[c] = z_cols[c] + bias * LOG2E
            _online_softmax_step(z_cols, v, m_ref.at[g], l_ref.at[g], acc_ref.at[g])

    def walk(ties):
        def far_step(kb, carry):
            step(kb, False, ties)
            return carry

        def near_step(kb, carry):
            step(kb, True, ties)
            return carry

        lax.fori_loop(0, n_far, far_step, 0)
        lax.fori_loop(n_far, n_blocks, near_step, 0)

    @pl.when(any_surplus)
    def _():
        walk(True)

    @pl.when(jnp.logical_not(any_surplus))
    def _():
        walk(False)

    for g in range(group):
        o_ref[0, :, g * hd:(g + 1) * hd] = (acc_ref[g] / jnp.sum(l_ref[g], axis=1, keepdims=True)).astype(o_ref.dtype)


def _dsa_attention(q, kt, v, score, sel_info, bias_tiles, *, kv_heads, group, hd, off, tq, tk):
    b, s, _ = q.shape
    lp = v.shape[1]
    n_off = bias_tiles.shape[0]
    return pl.pallas_call(
        functools.partial(_dsa_body, tq=tq, tk=tk, off=off, scale=hd ** -0.5, group=group, hd=hd),
        grid=(b, kv_heads, s // tq),
        in_specs=[pl.BlockSpec((1, tq, group * hd), lambda bb, h, i: (bb, i, h)),
                  pl.BlockSpec((1, hd, lp), lambda bb, h, i: (bb, h, 0)),
                  pl.BlockSpec((1, lp, hd), lambda bb, h, i: (bb, 0, h)),
                  pl.BlockSpec((1, tq, lp), lambda bb, h, i: (bb, i, 0)),
                  pl.BlockSpec((1, tq, LANE), lambda bb, h, i: (bb, i, 0)),
                  pl.BlockSpec((n_off, group, LANE, LANE), lambda bb, h, i: (0, h, 0, 0))],
        out_specs=pl.BlockSpec((1, tq, group * hd), lambda bb, h, i: (bb, i, h)),
        out_shape=jax.ShapeDtypeStruct(q.shape, BF16),
        scratch_shapes=[pltpu.VMEM((group, tq, LANE), F32), pltpu.VMEM((group, tq, LANE), F32),
                        pltpu.VMEM((group, tq, hd), F32), pltpu.VMEM((tq, LANE), F32)],
        compiler_params=_cparams(("parallel", "parallel", "arbitrary")),
        name="dsa_attention",
    )(q, kt, v, score, sel_info, bias_tiles)


def _merge_body(o0_ref, o1_ref, o2_ref, g0_ref, g1_ref, g2_ref, w_ref, out_ref):
    total = None
    for r, (o_ref, g_ref) in enumerate(((o0_ref, g0_ref), (o1_ref, g1_ref), (o2_ref, g2_ref))):
        y = jnp.dot(o_ref[0], w_ref[r], preferred_element_type=F32) * g_ref[0]
        total = y if total is None else total + y
    out_ref[0] = total.astype(out_ref.dtype)


def _merge(branches, gate, w_branch, layer):
    b, s, width = branches[0].shape
    d = w_branch.shape[3]
    tm = _pick(s, 512)
    tn = _pick(d, 512)
    nj = d // tn
    o_spec = pl.BlockSpec((1, tm, width), lambda bb, i, j: (bb, i, 0))
    g_specs = [pl.BlockSpec((1, tm, tn), lambda bb, i, j, r=r: (bb, i, r * nj + j)) for r in range(3)]
    return pl.pallas_call(
        _merge_body,
        grid=(b, s // tm, nj),
        in_specs=[o_spec, o_spec, o_spec, *g_specs,
                  pl.BlockSpec((None, 3, width, tn), lambda bb, i, j: (layer, 0, 0, j))],
        out_specs=pl.BlockSpec((1, tm, tn), lambda bb, i, j: (bb, i, j)),
        out_shape=jax.ShapeDtypeStruct((b, s, d), BF16),
        compiler_params=_cparams(("parallel", "parallel", "parallel")),
        name="branch_merge",
    )(*branches, gate, gate, gate, w_branch)


def _router_body(h_ref, w_ref, b_ref, start_ref, info_ref, cnt_ref, run_ref, *, n_groups, per_group):
    step = pl.program_id(0)

    @pl.when(step == 0)
    def _():
        run_ref[...] = start_ref[...]

    tm = h_ref.shape[0]
    logits = jnp.dot(h_ref[...].astype(BF16), w_ref[...], preferred_element_type=F32) + b_ref[...]
    lane = lax.broadcasted_iota(I32, (tm, LANE), 1).astype(F32)
    far = float(4 * LANE)
    is_grp = lane < n_groups
    g_logit = jnp.where(is_grp, logits, -jnp.inf)
    g_max = jnp.max(g_logit, axis=1, keepdims=True)
    grp = jnp.min(jnp.where(g_logit == g_max, lane, far), axis=1, keepdims=True)
    p_grp = 1.0 / jnp.sum(jnp.where(is_grp, jnp.exp(logits - g_max), 0.0), axis=1, keepdims=True)
    e_lo = n_groups + grp * per_group
    in_grp = jnp.logical_and(lane >= e_lo, lane < e_lo + per_group)
    e_logit = jnp.where(in_grp, logits, -jnp.inf)
    m1 = jnp.max(e_logit, axis=1, keepdims=True)
    i1 = jnp.min(jnp.where(e_logit == m1, lane, far), axis=1, keepdims=True)
    e_logit2 = jnp.where(lane == i1, -jnp.inf, e_logit)
    m2 = jnp.max(e_logit2, axis=1, keepdims=True)
    i2 = jnp.min(jnp.where(jnp.logical_and(e_logit2 == m2, in_grp), lane, far), axis=1, keepdims=True)
    e2 = jnp.exp(m2 - m1)
    w1 = p_grp / (1.0 + e2)
    w2 = p_grp * e2 / (1.0 + e2)
    hit1 = lane == i1
    hit2 = lane == i2
    onehot = jnp.where(jnp.logical_or(hit1, hit2), 1.0, 0.0)
    r_i = lax.broadcasted_iota(I32, (tm, tm), 0)
    c_i = lax.broadcasted_iota(I32, (tm, tm), 1)
    earlier = jnp.where(c_i < r_i, 1.0, 0.0).astype(BF16)
    before = jnp.dot(earlier, onehot.astype(BF16), preferred_element_type=F32) + run_ref[...]
    r1 = jnp.sum(jnp.where(hit1, before, 0.0), axis=1, keepdims=True)
    r2 = jnp.sum(jnp.where(hit2, before, 0.0), axis=1, keepdims=True)
    run_ref[...] = run_ref[...] + jnp.sum(onehot, axis=0, keepdims=True)
    cnt_ref[...] = run_ref[...]
    vals = (i1 - n_groups, i2 - n_groups, w1, w2, r1, r2)
    info = jnp.zeros((tm, LANE), F32)
    for pos, val in enumerate(vals):
        info = jnp.where(lane == pos, val, info)
    info_ref[...] = info


def _router(h, w_route, b_route, start_counts, *, n_groups, per_group):
    t, d = h.shape
    tm = _pick(t, 256)
    return pl.pallas_call(
        functools.partial(_router_body, n_groups=n_groups, per_group=per_group),
        grid=(t // tm,),
        in_specs=[pl.BlockSpec((tm, d), lambda i: (i, 0)),
                  pl.BlockSpec((d, LANE), lambda i: (0, 0)),
                  pl.BlockSpec((1, LANE), lambda i: (0, 0)),
                  pl.BlockSpec((1, LANE), lambda i: (0, 0))],
        out_specs=[pl.BlockSpec((tm, LANE), lambda i: (i, 0)),
                   pl.BlockSpec((1, LANE), lambda i: (0, 0))],
        out_shape=[jax.ShapeDtypeStruct((t, LANE), F32), jax.ShapeDtypeStruct((1, LANE), F32)],
        scratch_shapes=[pltpu.VMEM((1, LANE), F32)],
        compiler_params=_cparams(("arbitrary",)),
        name="moe_router",
    )(h, w_route, b_route, start_counts)


def _dispatch_body(dest_ref, h_ref, slots_in_ref, slots_ref, sem, *, tm):
    del slots_in_ref
    base = pl.program_id(0) * tm

    def copy(r, kk):
        return pltpu.make_async_copy(h_ref.at[pl.ds(r, 1)], slots_ref.at[pl.ds(dest_ref[(base + r) * TOP_K + kk], 1)], sem)

    def start(r, _):
        for kk in range(TOP_K):
            copy(r, kk).start()
        return 0

    def wait(r, _):
        for kk in range(TOP_K):
            copy(r, kk).wait()
        return 0

    lax.fori_loop(0, tm, start, 0)
    lax.fori_loop(0, tm, wait, 0)


def _dispatch(h, dest, slots):
    t, d = h.shape
    n_slots = slots.shape[0]
    tm = _pick(t, 256)
    grid_spec = pltpu.PrefetchScalarGridSpec(
        num_scalar_prefetch=1,
        grid=(t // tm,),
        in_specs=[pl.BlockSpec((tm, d), lambda i, dest_ref: (i, 0)),
                  pl.BlockSpec(memory_space=pl.ANY)],
        out_specs=pl.BlockSpec(memory_space=pl.ANY),
        scratch_shapes=[pltpu.SemaphoreType.DMA(())],
    )
    return pl.pallas_call(
        functools.partial(_dispatch_body, tm=tm),
        grid_spec=grid_spec,
        out_shape=jax.ShapeDtypeStruct((n_slots, d), h.dtype),
        input_output_aliases={2: 0},
        compiler_params=_cparams(("arbitrary",)),
        name="moe_dispatch",
    )(dest, h, slots)


def _experts_body(be_ref, nb_ref, x_ref, wg_ref, wu_ref, wd_ref, o_ref):
    blk = pl.program_id(0)
    f = pl.program_id(1)

    @pl.when(blk < nb_ref[0])
    def _():
        x = x_ref[...].astype(BF16)
        gate = jnp.dot(x, wg_ref[0], preferred_element_type=F32)
        up = jnp.dot(x, wu_ref[0], preferred_element_type=F32)
        act = (gate * jax.nn.sigmoid(gate) * up).astype(BF16)
        part = jnp.dot(act, wd_ref[0], preferred_element_type=F32)

        @pl.when(f == 0)
        def _():
            o_ref[...] = part

        @pl.when(f > 0)
        def _():
            o_ref[...] = o_ref[...] + part

    @pl.when(jnp.logical_and(blk >= nb_ref[0], f == 0))
    def _():
        o_ref[...] = jnp.zeros_like(o_ref)


def _experts(slots, block_expert, n_used, w_gate, w_up, w_down, layer, *, bm):
    n_slots, d = slots.shape
    ff = w_gate.shape[3]
    tf = _pick(ff, 256)
    nblk = n_slots // bm

    def x_map(i, f, be, nb):
        return (jnp.minimum(i, nb[0] - 1), 0)

    def w_in_map(i, f, be, nb):
        live = i < nb[0]
        return (layer, be[jnp.minimum(i, nb[0] - 1)], 0, jnp.where(live, f, ff // tf - 1))

    def w_out_map(i, f, be, nb):
        live = i < nb[0]
        return (layer, be[jnp.minimum(i, nb[0] - 1)], jnp.where(live, f, ff // tf - 1), 0)

    grid_spec = pltpu.PrefetchScalarGridSpec(
        num_scalar_prefetch=2,
        grid=(nblk, ff // tf),
        in_specs=[pl.BlockSpec((bm, d), x_map),
                  pl.BlockSpec((None, 1, d, tf), w_in_map),
                  pl.BlockSpec((None, 1, d, tf), w_in_map),
                  pl.BlockSpec((None, 1, tf, d), w_out_map)],
        out_specs=pl.BlockSpec((bm, d), lambda i, f, be, nb: (i, 0)),
    )
    return pl.pallas_call(
        _experts_body,
        grid_spec=grid_spec,
        out_shape=jax.ShapeDtypeStruct((n_slots, d), F32),
        compiler_params=_cparams(("arbitrary", "arbitrary")),
        name="moe_experts",
    )(block_expert, n_used, slots, w_gate, w_up, w_down)


def _combine_body(dest_ref, y_ref, x_ref, g_ref, wt_ref, o_ref, buf_ref, sem, *, tm, per_batch):
    base = (pl.program_id(0) * per_batch + pl.program_id(1)) * tm

    def copy(r, kk):
        return pltpu.make_async_copy(y_ref.at[pl.ds(dest_ref[(base + r) * TOP_K + kk], 1)],
                                     buf_ref.at[kk, pl.ds(r, 1)], sem)

    def start(r, _):
        for kk in range(TOP_K):
            copy(r, kk).start()
        return 0

    def wait(r, _):
        for kk in range(TOP_K):
            copy(r, kk).wait()
        return 0

    lax.fori_loop(0, tm, start, 0)
    lax.fori_loop(0, tm, wait, 0)
    wt = wt_ref[0]
    f = buf_ref[0] * wt[:, 0:1] + buf_ref[1] * wt[:, 1:2]
    o_ref[0] = x_ref[0] + g_ref[0] * f


def _combine(y_slots, dest, weights, x, gate):
    b, s, d = x.shape
    tm = _pick(s, 256)
    per_batch = s // tm
    grid_spec = pltpu.PrefetchScalarGridSpec(
        num_scalar_prefetch=1,
        grid=(b, per_batch),
        in_specs=[pl.BlockSpec(memory_space=pl.ANY),
                  pl.BlockSpec((1, tm, d), lambda bb, i, dest_ref: (bb, i, 0)),
                  pl.BlockSpec((1, 1, d), lambda bb, i, dest_ref: (bb, 0, 0)),
                  pl.BlockSpec((1, tm, TOP_K), lambda bb, i, dest_ref: (bb, i, 0))],
        out_specs=pl.BlockSpec((1, tm, d), lambda bb, i, dest_ref: (bb, i, 0)),
        scratch_shapes=[pltpu.VMEM((TOP_K, tm, d), F32), pltpu.SemaphoreType.DMA(())],
    )
    return pl.pallas_call(
        functools.partial(_combine_body, tm=tm, per_batch=per_batch),
        grid_spec=grid_spec,
        out_shape=jax.ShapeDtypeStruct(x.shape, F32),
        compiler_params=_cparams(("arbitrary", "arbitrary")),
        name="moe_combine",
    )(dest, y_slots, x, gate, weights)


def _history_body(c_ref, n_ref, o_ref, *, heads, hd, n_cache, transpose):
    j = pl.program_id(1)

    def emit(head_rows):
        for h in range(heads):
            x = head_rows(h)
            if transpose:
                o_ref[0, h * hd:(h + 1) * hd, :] = x.T.astype(BF16)
            else:
                o_ref[0, :, h * hd:(h + 1) * hd] = x.astype(BF16)

    @pl.when(j < n_cache)
    def _():
        emit(lambda h: c_ref[0, :, h, :])

    @pl.when(j >= n_cache)
    def _():
        emit(lambda h: n_ref[0, :, h * hd:(h + 1) * hd])


def _history_operand(cache, layer, new_rows, lp, *, transpose, rows=512):
    _, b, p, heads, hd = cache.shape
    w = heads * hd
    n_cache = p // rows
    n_new = (lp - p) // rows
    s = new_rows.shape[1]
    tail = jnp.concatenate([new_rows, jnp.zeros((b, lp - p - s, w), new_rows.dtype)], axis=1)
    if transpose:
        out_spec = pl.BlockSpec((1, w, rows), lambda bb, j: (bb, 0, j))
        out_shape = jax.ShapeDtypeStruct((b, w, lp), BF16)
    else:
        out_spec = pl.BlockSpec((1, rows, w), lambda bb, j: (bb, j, 0))
        out_shape = jax.ShapeDtypeStruct((b, lp, w), BF16)
    return pl.pallas_call(
        functools.partial(_history_body, heads=heads, hd=hd, n_cache=n_cache, transpose=transpose),
        grid=(b, n_cache + n_new),
        in_specs=[pl.BlockSpec((None, 1, rows, heads, hd),
                               lambda bb, j: (layer, bb, jnp.minimum(j, n_cache - 1), 0, 0)),
                  pl.BlockSpec((1, rows, w), lambda bb, j: (bb, jnp.maximum(j - n_cache, 0), 0))],
        out_specs=out_spec,
        out_shape=out_shape,
        compiler_params=_cparams(("parallel", "parallel")),
        name="history_operand",
    )(cache, tail)


def _mixer_tiles(s, n_keys, past_len=0, hist_rows=None):
    k_base = next((t for t in (1024, 512, 256) if n_keys % t == 0), None)
    if k_base is None:
        if hist_rows:
            lp = past_len + -(-(n_keys - past_len) // hist_rows) * hist_rows
            k_small = hist_rows
        else:
            lp = -(-n_keys // LANE) * LANE
            k_small = LANE
        whole = lp if lp <= 4096 else k_small
        return {"lp": lp, "sb": (_pick(s, 256), k_small), "fox": (_pick(s, 512), whole),
                "dsa": (_pick(s, 256), whole), "idx": (_pick(s, LANE), k_small)}
    return {
        "lp": n_keys,
        "sb": (_pick(s, 512), min(k_base, 512)),
        "fox": (_pick(s, 512), k_base),
        "dsa": (_pick(s, 512), k_base),
        "idx": (_pick(s, LANE), min(k_base, 512)),
    }


def _pad_keys(a, lp):
    pad = lp - a.shape[1]
    if pad == 0:
        return a
    return jnp.concatenate([a, jnp.zeros((a.shape[0], pad) + a.shape[2:], a.dtype)], axis=1)


def _token_mixers(h, past, wts, layer, dims, rows_so_far):
    b, s, d = h.shape
    hd, sbh, dsh, kvh, ixh, ixd, fxh = (dims[k] for k in ("hd", "sb_heads", "dsa_heads", "kv_heads",
                                                          "idx_heads", "idx_dim", "fox_heads"))
    seg = wts["w_in_segments"]
    past_len = 0 if past is None else past[0].shape[2]
    n_keys = past_len + s
    off = n_keys - s
    hist_rows = next((r for r in (512, 256, LANE) if past_len and past_len % r == 0), None)
    tiles = _mixer_tiles(s, n_keys, past_len, hist_rows)
    lp = tiles["lp"]
    direct = past is None and lp == s

    h_flat = h if direct else h.reshape(1, b * s, d)

    def proj(name, **kw):
        arr, start, width = seg[name]
        out = _matmul(h_flat, arr, b_rows=(layer, start, width), name="in_" + name, **kw)
        return out if direct else out.reshape(b, s, width)

    def stacked(new, idx):
        if rows_so_far is None:
            return new[None]
        return jnp.concatenate([rows_so_far[idx], new[None]], axis=0)

    def row_proj(name, idx, aux, **kw):
        if not direct:
            out = proj(name, **kw)
            return out, stacked(out, idx), None
        if rows_so_far is None:
            out, operand = proj(name, aux=aux, **kw)
            return out, out[None], operand
        stack, operand = proj(name, aux=aux, stack_on=rows_so_far[idx], **kw)
        return None, stack, operand

    norm = lambda key: [(wts[key][layer].reshape(1, hd), "const")]
    sb_q = proj("sb_q", out_dtype=BF16)
    sb_k, st_sb_k, kt_sb = row_proj("sb_k", 0, "bf16_t")
    sb_v, st_sb_v, vb_sb = row_proj("sb_v", 1, "bf16")
    ds_q = proj("ds_q", epi=_epi_head_rms, extras=norm("q_norm_dsa"), out_dtype=BF16)
    ds_k, st_ds_k, kt_ds = row_proj("ds_k", 2, "bf16_t", epi=_epi_head_rms, extras=norm("k_norm_dsa"))
    ds_v, st_ds_v, vb_ds = row_proj("ds_v", 3, "bf16")
    ix_q = proj("ix_q", out_dtype=BF16)
    small = proj("small", epi=_epi_masked_logsig,
                 extras=[(wts["small_bias"][layer], "row"), (wts["small_mask"], "row")])
    fx_q = proj("fx_q", epi=_epi_head_rms, extras=norm("q_norm_fox"), out_dtype=BF16)
    fx_k, st_fx_k, kt_fx = row_proj("fx_k", 5, "bf16_t", epi=_epi_head_rms, extras=norm("k_norm_fox"))
    fx_v, st_fx_v, vb_fx = row_proj("fx_v", 6, "bf16")
    gate = proj("gate", epi=_epi_sigmoid)
    at = wts["small_at"]
    ix_k = small[:, :, at["ix_k"]:at["ix_k"] + ixd]
    ix_w = small[:, :, at["ix_w"]:at["ix_w"] + ixh]
    log_f = small[:, :, at["fx_f"]:at["fx_f"] + fxh]
    new_rows = (st_sb_k, st_sb_v, st_ds_k, st_ds_v, stacked(ix_k, 4), st_fx_k, st_fx_v, stacked(log_f, 7))

    def cat(pc, r):
        pc = pc[layer]
        return _pad_keys(jnp.concatenate([pc.reshape(pc.shape[0], pc.shape[1], -1), r], axis=1), lp)

    if past is None:
        k_ix, lf_all = _pad_keys(ix_k, lp), _pad_keys(log_f, lp)
    else:
        k_ix, lf_all = cat(past[4], ix_k), cat(past[7], log_f)
    if not direct:
        rows = (sb_k, sb_v, ds_k, ds_v, None, fx_k, fx_v, None)

        def operand(idx, transpose):
            if past is not None and hist_rows:
                return _history_operand(past[idx], layer, rows[idx], lp, transpose=transpose, rows=hist_rows)
            full = (_pad_keys(rows[idx], lp) if past is None else cat(past[idx], rows[idx])).astype(BF16)
            return jnp.transpose(full, (0, 2, 1)) if transpose else full

        kt_sb, kt_ds, kt_fx = (operand(i, True) for i in (0, 2, 5))
        vb_sb, vb_ds, vb_fx = (operand(i, False) for i in (1, 3, 6))

    tq, tk = tiles["sb"]
    o_sb = _sb_attention(sb_q, kt_sb, vb_sb, heads=sbh, hd=hd, off=off, tq=tq, tk=tk)

    d_all = _cumsum_lanes(jnp.transpose(lf_all, (0, 2, 1)))
    d_q = d_all[:, :, off:off + s, None]
    tq, tk = tiles["fox"]
    o_fx = _fox_attention(fx_q, kt_fx, vb_fx, d_q, d_all.reshape(b, fxh, 1, lp),
                          heads=fxh, hd=hd, off=off, tq=tq, tk=tk)

    topk = min(DSA_TOPK_MAX, n_keys // 4)
    zeros = jnp.zeros_like(k_ix)
    k_even = jnp.concatenate([k_ix, zeros], axis=2)
    k_odd = jnp.concatenate([zeros, k_ix], axis=2)
    tq, tk = tiles["idx"]
    score, thr = _indexer(ix_q, ix_w, k_even, k_odd, off=off, topk=topk, idx_heads=ixh, tq=tq, tk=tk)
    tq, tk = tiles["dsa"]
    o_ds = _dsa_attention(ds_q, kt_ds, vb_ds, score, thr, wts["bias_tiles"], kv_heads=kvh,
                          group=dsh // kvh, hd=hd, off=off, tq=tq, tk=tk)

    if direct:
        merged = _merge((o_sb, o_ds, o_fx), gate, wts["w_branch"], layer)
    else:
        flat = lambda a: a.reshape(1, b * s, a.shape[-1])
        merged = _merge(tuple(flat(o) for o in (o_sb, o_ds, o_fx)), flat(gate), wts["w_branch"], layer)
        merged = merged.reshape(b, s, d)
    return merged, new_rows


def _hier_moe(parts, wts, layer, dims):
    d = parts[0][0].shape[-1]
    ng, ne = dims["n_groups"], dims["n_experts"]
    counts = jnp.zeros((1, LANE), F32)
    routed = []
    for _, h, _ in parts:
        info, counts = _router(h.reshape(-1, d), wts["w_route"][layer], wts["b_route"][layer], counts,
                               n_groups=ng, per_group=ne // ng)
        routed.append(info)
    n_assign = sum(info.shape[0] for info in routed) * TOP_K
    counts = counts[0, ng:ng + ne].astype(I32)
    bm = 512 if n_assign >= 512 * ne else 256
    padded = (counts + bm - 1) // bm * bm
    pad_end = jnp.cumsum(padded)
    pad_start = pad_end - padded
    n_blocks = -(-n_assign // bm) + ne
    block_expert = jnp.minimum(
        jnp.searchsorted(pad_end, jnp.arange(n_blocks, dtype=I32) * bm, side="right"), ne - 1).astype(I32)
    n_used = (pad_end[-1:] // bm).astype(I32)
    slots = jnp.zeros((n_blocks * bm, d), F32)
    dests = []
    for (_, h, _), info in zip(parts, routed):
        expert = info[:, 0:TOP_K].astype(I32)
        rank = info[:, 2 * TOP_K:3 * TOP_K].astype(I32)
        dest = (pad_start[expert] + rank).reshape(-1)
        slots = _dispatch(h.reshape(-1, d), dest, slots)
        dests.append(dest)
    y_slots = _experts(slots, block_expert, n_used, wts["w_exp_gate"], wts["w_exp_up"], wts["w_exp_down"], layer,
                       bm=bm)
    outs = []
    for (x, _, g2), info, dest in zip(parts, routed, dests):
        b, s, _ = x.shape
        outs.append(_combine(y_slots, dest, info[:, TOP_K:2 * TOP_K].reshape(b, s, TOP_K), x, g2))
    return outs


def _mixer_sublayer(x, mod, past, wts, layer, dims, rows):
    b, s, d = x.shape
    sh1, sc1, g1, sh2, sc2, g2 = mod
    h = _norm_mod(x, wts["g_norm1"][layer], sc1, sh1)
    merged, rows = _token_mixers(h, past, wts, layer, dims, rows)
    if s >= MM_TM:
        x = _matmul(merged, wts["w_out"], b_cols=(layer, 0, d), epi=_epi_residual,
                    extras=[(x, "tile"), (g1, "batchrow")], name="out_proj")
    else:
        flat = lambda a: jnp.broadcast_to(a, (b, s, d)).reshape(1, b * s, d)
        x = _matmul(flat(merged), wts["w_out"], b_cols=(layer, 0, d), epi=_epi_residual,
                    extras=[(flat(x), "tile"), (flat(g1), "tile")], name="out_proj").reshape(b, s, d)
    h = _norm_mod(x, wts["g_norm2"][layer], sc2, sh2, out_dtype=F32)
    return (x, h, g2), rows


def _trunks(xs, mods, pasts, wts, depth, dims):
    rows = [None] * len(xs)
    for l in range(depth):
        parts = []
        for r in range(len(xs)):
            part, rows[r] = _mixer_sublayer(xs[r], mods[r][l], pasts[r], wts, l, dims, rows[r])
            parts.append(part)
        xs = _hier_moe(parts, wts, l, dims)
    return xs, rows


def _prepare_weights(dims, w_in, q_norm_dsa, k_norm_dsa, q_norm_fox, k_norm_fox, b_forget, w_branch, w_out,
                     bias_tiles, w_route_grp, b_route_grp, w_route_exp, b_route_exp, w_exp_gate, w_exp_up,
                     w_exp_down, g_norm1, g_norm2):
    hd, sbh, dsh, kvh, ixh, ixd, fxh, d = (dims[k] for k in ("hd", "sb_heads", "dsa_heads", "kv_heads",
                                                             "idx_heads", "idx_dim", "fox_heads", "d"))
    depth = w_in.shape[0]
    widths = [("sb_q", sbh * hd), ("sb_k", sbh * hd), ("sb_v", sbh * hd), ("ds_q", dsh * hd), ("ds_k", kvh * hd),
              ("ds_v", kvh * hd), ("ix_q", ixh * ixd), ("ix_k", ixd), ("ix_w", ixh), ("fx_q", fxh * hd),
              ("fx_k", fxh * hd), ("fx_v", fxh * hd), ("fx_f", fxh), ("gate", 3 * d)]
    cols = {}
    start = 0
    for name, w in widths:
        cols[name] = (start, start + w)
        start += w
    runs = []
    for name, (a, bnd) in cols.items():
        width = bnd - a
        if width % LANE:
            continue
        if runs and runs[-1][1] == a and (a - runs[-1][0]) % _pick(width, MM_TN) == 0:
            runs[-1][1] = bnd
            runs[-1][2].append((name, a - runs[-1][0], width))
        else:
            runs.append([a, bnd, [(name, 0, width)]])
    w_in_t = jnp.swapaxes(w_in, 1, 2)
    seg = {}
    for first, last, members in runs:
        arr = w_in_t[:, first:last, :].astype(BF16)
        for name, rel, width in members:
            seg[name] = (arr, rel, width)
    n_small = ixd + ixh + fxh
    small_w = -(-n_small // LANE) * LANE
    small = jnp.concatenate([w_in_t[:, cols[name][0]:cols[name][1], :] for name in ("ix_k", "ix_w", "fx_f")]
                            + [jnp.zeros((depth, small_w - n_small, d), F32)], axis=1)
    seg["small"] = (small.astype(BF16), 0, small_w)
    small_at = {"ix_k": 0, "ix_w": ixd, "fx_f": ixd + ixh}
    lane = jnp.arange(small_w)
    is_f = jnp.logical_and(lane >= ixd + ixh, lane < n_small)
    small_bias = jnp.zeros((depth, 1, small_w), F32).at[:, 0, ixd + ixh:n_small].set(b_forget.astype(F32))
    ng, ne = dims["n_groups"], dims["n_experts"]
    w_route = jnp.concatenate([w_route_grp, w_route_exp, jnp.zeros((depth, d, LANE - ng - ne), F32)], axis=2)
    b_route = jnp.concatenate([b_route_grp, b_route_exp, jnp.zeros((depth, LANE - ng - ne), F32)], axis=1)
    return {
        "w_in_segments": seg,
        "small_at": small_at,
        "small_bias": small_bias,
        "small_mask": is_f.astype(F32).reshape(1, small_w),
        "q_norm_dsa": q_norm_dsa, "k_norm_dsa": k_norm_dsa, "q_norm_fox": q_norm_fox, "k_norm_fox": k_norm_fox,
        "w_branch": w_branch.astype(BF16), "w_out": w_out.astype(BF16),
        "bias_tiles": bias_tiles,
        "w_route": w_route.astype(BF16), "b_route": b_route.astype(F32).reshape(depth, 1, LANE),
        "w_exp_gate": w_exp_gate.astype(BF16), "w_exp_up": w_exp_up.astype(BF16),
        "w_exp_down": w_exp_down.astype(BF16),
        "g_norm1": g_norm1, "g_norm2": g_norm2,
    }


def kernel(x_prompt, x_sample, c_prompt, c_sample, cache_sb_k, cache_sb_v, cache_dsa_k, cache_dsa_v, cache_dsa_kidx, cache_fox_k, cache_fox_v, cache_fox_logf, w_mod, b_mod, g_norm1, g_norm2, w_in, q_norm_dsa, k_norm_dsa, q_norm_fox, k_norm_fox, b_forget, w_branch, w_out, rel_bias, w_route_grp, b_route_grp, w_route_exp, b_route_exp, w_exp_gate, w_exp_up, w_exp_down):
    depth = w_in.shape[0]
    d = x_prompt.shape[-1]
    hd = q_norm_dsa.shape[-1]
    dims = {
        "d": d, "hd": hd,
        "sb_heads": cache_sb_k.shape[3], "dsa_heads": rel_bias.shape[1], "kv_heads": cache_dsa_k.shape[3],
        "idx_dim": cache_dsa_kidx.shape[-1], "fox_heads": cache_fox_k.shape[3],
        "n_groups": w_route_grp.shape[-1], "n_experts": w_route_exp.shape[-1],
    }
    fixed = (3 * dims["sb_heads"] * hd + dims["dsa_heads"] * hd + 2 * dims["kv_heads"] * hd + dims["idx_dim"]
             + 3 * dims["fox_heads"] * hd + dims["fox_heads"] + 3 * d)
    dims["idx_heads"] = (w_in.shape[2] - fixed) // (dims["idx_dim"] + 1)

    bias_tiles = _bias_tiles(rel_bias)
    wts = _prepare_weights(dims, w_in, q_norm_dsa, k_norm_dsa, q_norm_fox, k_norm_fox, b_forget, w_branch, w_out,
                           bias_tiles, w_route_grp, b_route_grp, w_route_exp, b_route_exp, w_exp_gate, w_exp_up,
                           w_exp_down, g_norm1, g_norm2)

    nb_p, nb_s = c_prompt.shape[0], c_sample.shape[0]
    rows = -(-(nb_p + nb_s) // 8) * 8
    c_all = jnp.concatenate([c_prompt, c_sample, jnp.zeros((rows - nb_p - nb_s, d), F32)], axis=0)[None]
    mods_p, mods_s = [], []
    for l in range(depth):
        mod = _matmul(c_all, w_mod, b_cols=(l, 0, w_mod.shape[2]), epi=_epi_bias,
                      extras=[(b_mod[l].reshape(1, -1), "row")], silu_a=True, name="adaln_mod")[0]
        mods_p.append([m[:, None, :] for m in jnp.split(mod[:nb_p], 6, axis=-1)])
        mods_s.append([m[:, None, :] for m in jnp.split(mod[nb_p:nb_p + nb_s], 6, axis=-1)])

    past = (cache_sb_k, cache_sb_v, cache_dsa_k, cache_dsa_v, cache_dsa_kidx, cache_fox_k, cache_fox_v,
            cache_fox_logf)
    (y_p, y_s), (rows_p, rows_s) = _trunks([x_prompt, x_sample], [mods_p, mods_s], [None, past], wts, depth, dims)

    head_counts = (dims["sb_heads"], dims["sb_heads"], dims["kv_heads"], dims["kv_heads"], None,
                   dims["fox_heads"], dims["fox_heads"], None)

    def split_heads(rows):
        return [a if heads is None else a.reshape(a.shape[0], a.shape[1], a.shape[2], heads, hd)
                for a, heads in zip(rows, head_counts)]

    return (y_p, y_s, *split_heads(rows_p), *split_heads(rows_s))
```

```python
import functools
import math

import jax
import jax.numpy as jnp
from jax import lax
from jax.experimental import pallas as pl
from jax.experimental.pallas import tpu as pltpu

F32 = jnp.float32
BF16 = jnp.bfloat16
I32 = jnp.int32

EPS = 1e-6
CHUNK = 64
DSA_TOPK_MAX = 256
MAX_DISTANCE = 128
TOP_K = 2
LANE = 128
NEG_BIG = -1e30
LOG2E = math.log2(math.e)
F32_EXP2_UNDERFLOW = -151.0
VMEM_LIMIT = 56 * 1024 * 1024
MM_TM, MM_TN = 1024, 512
MM_TN_WIDE = 1024


def _cparams(sem):
    return pltpu.CompilerParams(dimension_semantics=sem, vmem_limit_bytes=VMEM_LIMIT)


def _pick(n, pref):
    if n <= pref:
        return n
    t = pref
    while n % t:
        t //= 2
    return t


def _norm_mod_body(x_ref, g_ref, sc_ref, sh_ref, o_ref):
    x = x_ref[0]
    ms = jnp.mean(x * x, axis=-1, keepdims=True)
    y = x * lax.rsqrt(ms + EPS) * g_ref[...]
    o_ref[0] = (y * (1.0 + sc_ref[0]) + sh_ref[0]).astype(o_ref.dtype)


def _norm_mod(x, g, sc, sh, out_dtype=BF16):
    b, s, d = x.shape
    ts = _pick(s, 256)
    return pl.pallas_call(
        _norm_mod_body,
        grid=(b, s // ts),
        in_specs=[pl.BlockSpec((1, ts, d), lambda i, j: (i, j, 0)),
                  pl.BlockSpec((1, d), lambda i, j: (0, 0)),
                  pl.BlockSpec((1, 1, d), lambda i, j: (i, 0, 0)),
                  pl.BlockSpec((1, 1, d), lambda i, j: (i, 0, 0))],
        out_specs=pl.BlockSpec((1, ts, d), lambda i, j: (i, j, 0)),
        out_shape=jax.ShapeDtypeStruct((b, s, d), out_dtype),
        compiler_params=_cparams(("parallel", "parallel")),
        name="norm_mod",
    )(x, g.reshape(1, d), sc, sh)


def _log_sigmoid(x):
    return jnp.minimum(x, 0.0) - jnp.log1p(jnp.exp(-jnp.abs(x)))


def _epi_none(acc):
    return acc


def _epi_bias(acc, bias):
    return acc + bias


def _epi_head_rms(acc, gain):
    hd = gain.shape[-1]
    outs = []
    for c in range(acc.shape[-1] // hd):
        blk = acc[:, c * hd:(c + 1) * hd]
        ms = jnp.mean(blk * blk, axis=-1, keepdims=True)
        outs.append(blk * lax.rsqrt(ms + EPS) * gain)
    return jnp.concatenate(outs, axis=-1) if len(outs) > 1 else outs[0]


def _epi_masked_logsig(acc, bias, mask):
    return jnp.where(mask > 0.0, _log_sigmoid(acc + bias), acc)


def _epi_sigmoid(acc):
    return jax.nn.sigmoid(acc)


def _epi_residual(acc, res, gate):
    return res + gate * acc


def _mm_body(*refs, epi, n_extra, silu_a, aux, stacked, b_transposed):
    a_ref, b_ref = refs[0], refs[1]
    extras = refs[2:2 + n_extra]
    n_in = 2 + n_extra + (1 if stacked else 0)
    o_ref = refs[n_in]
    a = a_ref[0]
    if silu_a:
        a = a.astype(F32)
        a = a * jax.nn.sigmoid(a)
    contract_b = 1 if b_transposed else 0
    acc = lax.dot_general(a.astype(BF16), b_ref[...].astype(BF16), (((1,), (contract_b,)), ((), ())),
                          preferred_element_type=F32)
    vals = []
    for r in extras:
        v = r[...]
        vals.append(v[0] if v.ndim == 3 else v)
    res = epi(acc, *vals)
    if stacked:
        prev_ref = refs[n_in - 1]
        n_prev = prev_ref.shape[0]
        o_ref[0:n_prev, 0] = prev_ref[:, 0]
        o_ref[n_prev, 0] = res.astype(o_ref.dtype)
    else:
        o_ref[0] = res.astype(o_ref.dtype)
    if aux == "bf16":
        refs[n_in + 1][0] = res.astype(BF16)
    elif aux == "bf16_t":
        refs[n_in + 1][0] = res.T.astype(BF16)


def _matmul(a, b, *, epi=_epi_none, extras=(), out_dtype=F32, tm=MM_TM, tn=MM_TN, silu_a=False, aux=None,
            b_cols=None, b_rows=None, stack_on=None, name="matmul"):
    bsz, s, k = a.shape
    tm = _pick(s, tm)
    if b_rows is not None:
        layer, start, n = b_rows
        tn = _pick(n, tn)
        assert start % tn == 0
        b_spec = pl.BlockSpec((None, tn, k), lambda bb, i, j: (layer, start // tn + j, 0))
    elif b_cols is None:
        n = b.shape[1]
        tn = _pick(n, tn)
        b_spec = pl.BlockSpec((k, tn), lambda bb, i, j: (0, j))
    else:
        layer, start, n = b_cols
        tn = _pick(n, tn)
        assert start % tn == 0
        b_spec = pl.BlockSpec((None, k, tn), lambda bb, i, j: (layer, 0, start // tn + j))
    in_specs = [pl.BlockSpec((1, tm, k), lambda bb, i, j: (bb, i, 0)), b_spec]
    args = [a, b]
    for arr, kind in extras:
        if kind == "row":
            in_specs.append(pl.BlockSpec((1, tn), lambda bb, i, j: (0, j)))
        elif kind == "batchrow":
            in_specs.append(pl.BlockSpec((1, 1, tn), lambda bb, i, j: (bb, 0, j)))
        elif kind == "tile":
            in_specs.append(pl.BlockSpec((1, tm, tn), lambda bb, i, j: (bb, i, j)))
        else:
            in_specs.append(pl.BlockSpec(arr.shape, lambda bb, i, j, nd=arr.ndim: (0,) * nd))
        args.append(arr)
    if stack_on is None:
        out_specs = pl.BlockSpec((1, tm, tn), lambda bb, i, j: (bb, i, j))
        out_shape = jax.ShapeDtypeStruct((bsz, s, n), out_dtype)
    else:
        n_prev = stack_on.shape[0]
        in_specs.append(pl.BlockSpec((n_prev, 1, tm, tn), lambda bb, i, j: (0, bb, i, j)))
        args.append(stack_on)
        out_specs = pl.BlockSpec((n_prev + 1, 1, tm, tn), lambda bb, i, j: (0, bb, i, j))
        out_shape = jax.ShapeDtypeStruct((n_prev + 1, bsz, s, n), out_dtype)
    if aux == "bf16":
        out_specs = [out_specs, pl.BlockSpec((1, tm, tn), lambda bb, i, j: (bb, i, j))]
        out_shape = [out_shape, jax.ShapeDtypeStruct((bsz, s, n), BF16)]
    elif aux == "bf16_t":
        out_specs = [out_specs, pl.BlockSpec((1, tn, tm), lambda bb, i, j: (bb, j, i))]
        out_shape = [out_shape, jax.ShapeDtypeStruct((bsz, n, s), BF16)]
    return pl.pallas_call(
        functools.partial(_mm_body, epi=epi, n_extra=len(extras), silu_a=silu_a, aux=aux,
                          stacked=stack_on is not None, b_transposed=b_rows is not None),
        grid=(bsz, s // tm, n // tn),
        in_specs=in_specs,
        out_specs=out_specs,
        out_shape=out_shape,
        compiler_params=_cparams(("parallel", "parallel", "parallel")),
        name=name,
    )(*args)


def _sb_body(q_ref, kt_ref, v_ref, o_ref, run_ref, acc_ref, *, tq, tk, cw, off, scale):
    i = pl.program_id(2)
    q_lo = off + i * tq
    q = (q_ref[0].astype(F32) * (scale * LOG2E)).astype(BF16)
    tri_r = lax.broadcasted_iota(I32, (2 * cw, cw), 0)
    tri_c = lax.broadcasted_iota(I32, (2 * cw, cw), 1)
    suffix = jnp.where(jnp.where(tri_r >= cw, tri_r - cw, tri_r) >= tri_c, 1.0, 0.0).astype(BF16)
    n_blocks = (q_lo + tq - 2) // tk + 1
    n_plain = q_lo // tk
    run_ref[...] = jnp.zeros_like(run_ref)
    acc_ref[...] = jnp.zeros_like(acc_ref)

    def tile(ks, width, masked):
        kt = kt_ref[0, :, pl.ds(ks, width)]
        v = v_ref[0, pl.ds(ks, width), :]
        z_all = jnp.dot(q, kt, preferred_element_type=F32)
        run = jnp.max(run_ref[...], axis=1, keepdims=True)
        ws = [None] * (width // cw)
        for j in reversed(range(width // cw)):
            z = z_all[:, j * cw:(j + 1) * cw]
            log_keep = jnp.minimum(-z, 0.0) - jnp.log2(1.0 + jnp.exp2(jnp.minimum(z, -z)))
            if masked:
                row = lax.broadcasted_iota(I32, (tq, cw), 0)
                col = lax.broadcasted_iota(I32, (tq, cw), 1)
                mask = (ks + j * cw + col) < (q_lo + row)
                log_keep = jnp.where(mask, log_keep, 0.0)
            hi = log_keep.astype(BF16)
            lo = (log_keep - hi.astype(F32)).astype(BF16)
            incl = jnp.dot(jnp.concatenate([hi, lo], axis=1), suffix,
                           preferred_element_type=F32)
            w = jnp.exp2(z + incl + run)
            if masked:
                w = jnp.where(mask, w, 0.0)
            ws[j] = w.astype(BF16)
            run = run + jnp.sum(log_keep, axis=1, keepdims=True)
        w_all = jnp.concatenate(ws, axis=1) if len(ws) > 1 else ws[0]
        acc_ref[...] = acc_ref[...] + jnp.dot(w_all, v, preferred_element_type=F32)
        run_ref[...] = jnp.broadcast_to(run, run_ref.shape)

    def masked_step(n, carry):
        tile(pl.multiple_of((n_blocks - 1 - n) * tk, tk), tk, True)
        return carry

    lax.fori_loop(0, n_blocks - n_plain, masked_step, 0)

    def alive():
        return jnp.max(run_ref[...]) > F32_EXP2_UNDERFLOW

    def cond(carry):
        sub, go = carry
        return jnp.logical_and(sub >= 0, go)

    def plain_step(carry):
        sub, _ = carry
        tile(pl.multiple_of(sub * cw, cw), cw, False)
        return sub - 1, alive()

    lax.while_loop(cond, plain_step, (n_plain * (tk // cw) - 1, alive()))
    o_ref[0] = acc_ref[...].astype(o_ref.dtype)


def _sb_attention(q, kt, v, *, heads, hd, off, tq, tk):
    b, s, _ = q.shape
    lp = v.shape[1]
    cw = min(tk, 256)
    return pl.pallas_call(
        functools.partial(_sb_body, tq=tq, tk=tk, cw=cw, off=off, scale=hd ** -0.5),
        grid=(b, heads, s // tq),
        in_specs=[pl.BlockSpec((1, tq, hd), lambda bb, h, i: (bb, i, h)),
                  pl.BlockSpec((1, hd, lp), lambda bb, h, i: (bb, h, 0)),
                  pl.BlockSpec((1, lp, hd), lambda bb, h, i: (bb, 0, h))],
        out_specs=pl.BlockSpec((1, tq, hd), lambda bb, h, i: (bb, i, h)),
        out_shape=jax.ShapeDtypeStruct(q.shape, BF16),
        scratch_shapes=[pltpu.VMEM((tq, LANE), F32), pltpu.VMEM((tq, hd), F32)],
        compiler_params=_cparams(("parallel", "parallel", "arbitrary")),
        name="sb_attention",
    )(q, kt, v)


def _cumsum_body(x_ref, o_ref):
    x = x_ref[0]
    n = x.shape[1]
    lane = lax.broadcasted_iota(I32, x.shape, 1)
    shift = 1
    while shift < n:
        x = x + jnp.where(lane >= shift, pltpu.roll(x, shift, axis=1), 0.0)
        shift *= 2
    o_ref[0] = x


def _cumsum_lanes(x):
    b, h, n = x.shape
    return pl.pallas_call(
        _cumsum_body,
        grid=(b,),
        in_specs=[pl.BlockSpec((1, h, n), lambda i: (i, 0, 0))],
        out_specs=pl.BlockSpec((1, h, n), lambda i: (i, 0, 0)),
        out_shape=jax.ShapeDtypeStruct(x.shape, F32),
        compiler_params=_cparams(("parallel",)),
        name="cumsum_logf",
    )(x)


def _online_softmax_step(z_cols, v, m_ref, l_ref, acc_ref, shift_extra=None):
    m_prev = m_ref[...]
    m_cur = z_cols[0]
    for z in z_cols[1:]:
        m_cur = jnp.maximum(m_cur, z)
    m_cur = jnp.max(m_cur, axis=1, keepdims=True)
    if shift_extra is not None:
        m_cur = m_cur + shift_extra
    m_next = jnp.maximum(m_prev, m_cur)
    shift = m_next if shift_extra is None else m_next - shift_extra
    ps = [jnp.exp2(z - shift) for z in z_cols]
    alpha = jnp.exp2(m_prev - m_next)
    l_new = alpha * l_ref[...]
    for p in ps:
        l_new = l_new + p
    l_ref[...] = l_new
    m_ref[...] = m_next
    p_all = jnp.concatenate([p.astype(BF16) for p in ps], axis=1) if len(ps) > 1 else ps[0].astype(BF16)
    acc_ref[...] = alpha * acc_ref[...] + jnp.dot(p_all, v, preferred_element_type=F32)


def _fox_body(q_ref, kt_ref, v_ref, dq_ref, dk_ref, o_ref, m_ref, l_ref, acc_ref, kmax_ref, *, tq, tk, off, scale):
    i = pl.program_id(2)
    q_lo = off + i * tq
    q = (q_ref[0].astype(F32) * (scale * LOG2E)).astype(BF16)
    dq = dq_ref[0, 0] * LOG2E
    n_blocks = (q_lo + tq - 1) // tk + 1
    n_plain = q_lo // tk
    n_cols = tk // LANE
    lp = kt_ref.shape[2]
    m_ref[...] = jnp.full_like(m_ref, NEG_BIG)
    l_ref[...] = jnp.zeros_like(l_ref)
    acc_ref[...] = jnp.zeros_like(acc_ref)

    @pl.when(i == 0)
    def _():
        kf = kt_ref[0].astype(F32)
        norm2 = jnp.max(jnp.sum(kf * kf, axis=0, keepdims=True), axis=1, keepdims=True)
        kmax_ref[...] = jnp.broadcast_to(jnp.sqrt(norm2), kmax_ref.shape)

    qf = q.astype(F32)
    cap = jnp.sqrt(jnp.sum(qf * qf, axis=1, keepdims=True)) * kmax_ref[0:1, 0:1] * 1.001 + 0.001 + dq

    def step(kb, masked):
        ks = pl.multiple_of(kb * tk, tk)
        kt = kt_ref[0, :, pl.ds(ks, tk)]
        v = v_ref[0, pl.ds(ks, tk), :]
        dk = dk_ref[0, 0, :, pl.ds(ks, tk)] * LOG2E
        z = jnp.dot(q, kt, preferred_element_type=F32) - dk
        if masked:
            row = lax.broadcasted_iota(I32, (tq, tk), 0)
            col = lax.broadcasted_iota(I32, (tq, tk), 1)
            z = jnp.where((ks + col) <= (q_lo + row), z, NEG_BIG)
        _online_softmax_step([z[:, c * LANE:(c + 1) * LANE] for c in range(n_cols)], v, m_ref, l_ref, acc_ref,
                             shift_extra=dq)

    def masked_step(n, carry):
        step(n_blocks - 1 - n, True)
        return carry

    lax.fori_loop(0, n_blocks - n_plain, masked_step, 0)

    def alive(kb):
        lane = lax.broadcasted_iota(I32, (1, lp), 1)
        dk_lo = jnp.min(jnp.where(lane < (kb + 1) * tk, dk_ref[0, 0] * LOG2E, jnp.inf), axis=1, keepdims=True)
        m_now = jnp.max(m_ref[...], axis=1, keepdims=True)
        return jnp.max(cap - dk_lo - m_now) > F32_EXP2_UNDERFLOW

    def cond(carry):
        kb, go = carry
        return jnp.logical_and(kb >= 0, go)

    def plain_step(carry):
        kb, _ = carry
        step(kb, False)
        return kb - 1, alive(kb - 1)

    lax.while_loop(cond, plain_step, (n_plain - 1, alive(n_plain - 1)))
    o_ref[0] = (acc_ref[...] / jnp.sum(l_ref[...], axis=1, keepdims=True)).astype(o_ref.dtype)


def _fox_attention(q, kt, v, d_q, d_k, *, heads, hd, off, tq, tk):
    b, s, _ = q.shape
    lp = v.shape[1]
    return pl.pallas_call(
        functools.partial(_fox_body, tq=tq, tk=tk, off=off, scale=hd ** -0.5),
        grid=(b, heads, s // tq),
        in_specs=[pl.BlockSpec((1, tq, hd), lambda bb, h, i: (bb, i, h)),
                  pl.BlockSpec((1, hd, lp), lambda bb, h, i: (bb, h, 0)),
                  pl.BlockSpec((1, lp, hd), lambda bb, h, i: (bb, 0, h)),
                  pl.BlockSpec((1, 1, tq, 1), lambda bb, h, i: (bb, h, i, 0)),
                  pl.BlockSpec((1, 1, 1, lp), lambda bb, h, i: (bb, h, 0, 0))],
        out_specs=pl.BlockSpec((1, tq, hd), lambda bb, h, i: (bb, i, h)),
        out_shape=jax.ShapeDtypeStruct(q.shape, BF16),
        scratch_shapes=[pltpu.VMEM((tq, LANE), F32), pltpu.VMEM((tq, LANE), F32), pltpu.VMEM((tq, hd), F32),
                        pltpu.VMEM((8, LANE), F32)],
        compiler_params=_cparams(("arbitrary", "arbitrary", "arbitrary")),
        name="fox_attention",
    )(q, kt, v, d_q, d_k)


def _sortable(x):
    u = lax.bitcast_convert_type(x, I32)
    return u ^ (lax.shift_right_arithmetic(u, 31) & 0x7FFFFFFF)


def _indexer_body(qi_ref, wi_ref, ka_ref, kb_ref, sc_ref, thr_ref, key_ref, *, tq, tk, off, topk, idx_heads, w_scale):
    i = pl.program_id(1)
    q_lo = off + i * tq
    lp = ka_ref.shape[1]
    adm_end = ((q_lo + tq - 1) // CHUNK + 1) * CHUNK
    n_adm = (adm_end + tk - 1) // tk
    n_all = lp // tk
    wi = wi_ref[0] * w_scale
    qi = qi_ref[0].astype(BF16)
    pair = ka_ref.shape[2]
    w_cols = [wi[:, j:j + 1] for j in range(idx_heads)]
    q_pairs = [qi[:, p * pair:(p + 1) * pair] for p in range(idx_heads // 2)]
    row = lax.broadcasted_iota(I32, (tq, tk), 0)
    col = lax.broadcasted_iota(I32, (tq, tk), 1)
    q_chunk = (q_lo + row) // CHUNK

    def score_step(kb, _):
        ks = pl.multiple_of(kb * tk, tk)
        k_even = ka_ref[0, pl.ds(ks, tk), :].astype(BF16)
        k_odd = kb_ref[0, pl.ds(ks, tk), :].astype(BF16)
        total = jnp.zeros((tq, tk), F32)
        for p in range(idx_heads // 2):
            s0 = lax.dot_general(q_pairs[p], k_even, (((1,), (1,)), ((), ())), preferred_element_type=F32)
            s1 = lax.dot_general(q_pairs[p], k_odd, (((1,), (1,)), ((), ())), preferred_element_type=F32)
            total = total + jnp.maximum(s0, 0.0) * w_cols[2 * p] + jnp.maximum(s1, 0.0) * w_cols[2 * p + 1]
        admissible = ((ks + col) // CHUNK) <= q_chunk
        total = jnp.where(admissible, total, -jnp.inf)
        sc_ref[0, :, pl.ds(ks, tk)] = total
        key_ref[:, pl.ds(ks, tk)] = _sortable(total)
        return 0

    lax.fori_loop(0, n_adm, score_step, 0)

    def fill_step(kb, _):
        ks = pl.multiple_of(kb * tk, tk)
        sc_ref[0, :, pl.ds(ks, tk)] = jnp.full((tq, tk), -jnp.inf, F32)
        return 0

    lax.fori_loop(n_adm, n_all, fill_step, 0)

    sign = jnp.int32(-2 ** 31)

    def count(hit_fn):
        def count_step(kb, cnt):
            ks = pl.multiple_of(kb * tk, tk)
            hit = jnp.where(hit_fn(key_ref[:, pl.ds(ks, tk)]), 1.0, 0.0)
            for c in range(tk // LANE):
                cnt = cnt + hit[:, c * LANE:(c + 1) * LANE]
            return cnt

        cnt = lax.fori_loop(0, n_adm, count_step, jnp.zeros((tq, LANE), F32))
        return jnp.sum(cnt, axis=1, keepdims=True)

    def bit_step(carry):
        n, t_bits, settled = carry
        bit = lax.shift_left(jnp.int32(1), 31 - n)
        cand_bits = t_bits | bit
        cand = cand_bits ^ sign
        cnt = count(lambda keys: keys >= cand)
        settled = jnp.maximum(settled, jnp.where(cnt == float(topk), 1.0, 0.0))
        return n + 1, jnp.where(cnt >= float(topk), cand_bits, t_bits), settled

    def bits_left(carry):
        n, _, settled = carry
        return jnp.logical_and(n < 32, jnp.min(settled) < 1.0)

    _, t_bits, _ = lax.while_loop(bits_left, bit_step,
                                  (jnp.int32(0), jnp.zeros((tq, 1), I32), jnp.zeros((tq, 1), F32)))
    t_key = t_bits ^ sign
    t_u = t_key ^ (lax.shift_right_arithmetic(t_key, 31) & 0x7FFFFFFF)
    thr = lax.bitcast_convert_type(t_u, F32)
    need = float(topk) - count(lambda keys: keys > t_key)
    surplus = jnp.where(thr > -jnp.inf, count(lambda keys: keys >= t_key) - float(topk), 0.0)
    lane = lax.broadcasted_iota(I32, (tq, LANE), 1)
    thr_ref[0] = jnp.where(lane == 0, thr, jnp.where(lane == 1, need, jnp.where(lane == 2, surplus, 0.0)))


def _indexer(qi, wi, k_even, k_odd, *, off, topk, idx_heads, tq, tk):
    b, s, _ = qi.shape
    lp = k_even.shape[1]
    idx_dim = qi.shape[2] // idx_heads
    w_scale = (idx_heads ** -0.5) * (idx_dim ** -0.5)
    return pl.pallas_call(
        functools.partial(_indexer_body, tq=tq, tk=tk, off=off, topk=topk, idx_heads=idx_heads, w_scale=w_scale),
        grid=(b, s // tq),
        in_specs=[pl.BlockSpec((1, tq, qi.shape[2]), lambda bb, i: (bb, i, 0)),
                  pl.BlockSpec((1, tq, idx_heads), lambda bb, i: (bb, i, 0)),
                  pl.BlockSpec((1, lp, k_even.shape[2]), lambda bb, i: (bb, 0, 0)),
                  pl.BlockSpec((1, lp, k_odd.shape[2]), lambda bb, i: (bb, 0, 0))],
        out_specs=[pl.BlockSpec((1, tq, lp), lambda bb, i: (bb, i, 0)),
                   pl.BlockSpec((1, tq, LANE), lambda bb, i: (bb, i, 0))],
        out_shape=[jax.ShapeDtypeStruct((b, s, lp), F32), jax.ShapeDtypeStruct((b, s, LANE), F32)],
        scratch_shapes=[pltpu.VMEM((tq, lp), I32)],
        compiler_params=_cparams(("parallel", "arbitrary")),
        name="dsa_indexer",
    )(qi, wi, k_even, k_odd)


def _bias_tile_body(rb_ref, o_ref, *, n_buckets, heads, n_off):
    half = n_buckets // 2
    exact = half // 2
    row = lax.broadcasted_iota(I32, (LANE, LANE), 0)
    col = lax.broadcasted_iota(I32, (LANE, LANE), 1)
    buckets = []
    for d in range(n_off):
        rel = d * LANE + row - col
        side = jnp.where(rel < 0, half, 0)
        a = jnp.abs(rel)
        far = exact + (jnp.log(jnp.maximum(a, 1).astype(F32) / exact)
                       / math.log(MAX_DISTANCE / exact) * (half - exact)).astype(I32)
        far = jnp.minimum(far, half - 1)
        buckets.append(side + jnp.where(a < exact, a, far))
    for h in range(heads):
        tiles = []
        for bucket in buckets:
            tile = jnp.zeros((LANE, LANE), F32)
            for bkt in range(n_buckets):
                tile = jnp.where(bucket == bkt, rb_ref[bkt, h], tile)
            tiles.append(tile)
        for d in range(n_off):
            o_ref[d, h] = tiles[d] - tiles[n_off - 1]


def _bias_tiles(rel_bias, n_off=3):
    n_buckets, heads = rel_bias.shape
    return pl.pallas_call(
        functools.partial(_bias_tile_body, n_buckets=n_buckets, heads=heads, n_off=n_off),
        in_specs=[pl.BlockSpec(memory_space=pltpu.SMEM)],
        out_specs=pl.BlockSpec(memory_space=pltpu.VMEM),
        out_shape=jax.ShapeDtypeStruct((n_off, heads, LANE, LANE), F32),
        name="dsa_bias_tiles",
    )(rel_bias)


def _dsa_body(q_ref, kt_ref, v_ref, sc_ref, sel_ref, bias_ref, o_ref, m_ref, l_ref, acc_ref, tied_ref, *, tq, tk,
              off, scale, group, hd):
    i = pl.program_id(2)
    q_lo = off + i * tq
    n_off = bias_ref.shape[0]
    adm_end = ((q_lo + tq - 1) // CHUNK + 1) * CHUNK
    n_blocks = (adm_end + tk - 1) // tk
    n_far = jnp.maximum(q_lo - LANE, 0) // tk
    n_cols = tk // LANE
    thr = sel_ref[0, :, 0:1]
    need = sel_ref[0, :, 1:2]
    any_surplus = jnp.max(sel_ref[0, :, 2:3]) > 0.0
    qs = [(q_ref[0, :, g * hd:(g + 1) * hd].astype(F32) * (scale * LOG2E)).astype(BF16) for g in range(group)]
    m_ref[...] = jnp.full_like(m_ref, NEG_BIG)
    l_ref[...] = jnp.zeros_like(l_ref)
    acc_ref[...] = jnp.zeros_like(acc_ref)
    tied_ref[...] = jnp.zeros_like(tied_ref)

    def selected(sc, ties):
        finite = jnp.abs(sc) < jnp.inf
        if not ties:
            return jnp.logical_and(sc >= thr, finite)
        tri_r = lax.broadcasted_iota(I32, (LANE, LANE), 0)
        tri_c = lax.broadcasted_iota(I32, (LANE, LANE), 1)
        before = jnp.where(tri_r < tri_c, 1.0, 0.0).astype(BF16)
        seen = jnp.max(tied_ref[...], axis=1, keepdims=True)
        cols = []
        for c in range(n_cols):
            sc_c = sc[:, c * LANE:(c + 1) * LANE]
            fin_c = finite[:, c * LANE:(c + 1) * LANE]
            tie = jnp.where(jnp.logical_and(sc_c == thr, fin_c), 1.0, 0.0)
            rank = seen + jnp.dot(tie.astype(BF16), before, preferred_element_type=F32)
            take = jnp.logical_and(tie > 0.0, rank < need)
            cols.append(jnp.logical_or(jnp.logical_and(sc_c > thr, fin_c), take))
            seen = seen + jnp.sum(tie, axis=1, keepdims=True)
        tied_ref[...] = jnp.broadcast_to(seen, tied_ref.shape)
        return jnp.concatenate(cols, axis=1) if n_cols > 1 else cols[0]

    def step(kb, near, ties):
        ks = pl.multiple_of(kb * tk, tk)
        kt = kt_ref[0, :, pl.ds(ks, tk)]
        v = v_ref[0, pl.ds(ks, tk), :]
        sc = sc_ref[0, :, pl.ds(ks, tk)]
        neg = jnp.where(selected(sc, ties), 0.0, NEG_BIG)
        for g in range(group):
            z = jnp.dot(qs[g], kt, preferred_element_type=F32) + neg
            z_cols = [z[:, c * LANE:(c + 1) * LANE] for c in range(n_cols)]
            if near:
                for c in range(n_cols):
                    parts = []
                    for r in range(0, tq, LANE):
                        d_idx = jnp.clip((q_lo + r - ks - c * LANE) // LANE, 0, n_off - 1)
                        parts.append(bias_ref[d_idx, g, 0:min(LANE, tq - r), :])
                    bias = jnp.concatenate(parts, axis=0) if len(parts) > 1 else parts[0]
                    z_cols[c] = z_cols[c] + bias * LOG2E
            _online_softmax_step(z_cols, v, m_ref.at[g], l_ref.at[g], acc_ref.at[g])

    def walk(ties):
        def far_step(kb, carry):
            step(kb, False, ties)
            return carry

        def near_step(kb, carry):
            step(kb, True, ties)
            return carry

        lax.fori_loop(0, n_far, far_step, 0)
        lax.fori_loop(n_far, n_blocks, near_step, 0)

    @pl.when(any_surplus)
    def _():
        walk(True)

    @pl.when(jnp.logical_not(any_surplus))
    def _():
        walk(False)

    for g in range(group):
        o_ref[0, :, g * hd:(g + 1) * hd] = (acc_ref[g] / jnp.sum(l_ref[g], axis=1, keepdims=True)).astype(o_ref.dtype)


def _dsa_attention(q, kt, v, score, sel_info, bias_tiles, *, kv_heads, group, hd, off, tq, tk):
    b, s, _ = q.shape
    lp = v.shape[1]
    n_off = bias_tiles.shape[0]
    return pl.pallas_call(
        functools.partial(_dsa_body, tq=tq, tk=tk, off=off, scale=hd ** -0.5, group=group, hd=hd),
        grid=(b, kv_heads, s // tq),
        in_specs=[pl.BlockSpec((1, tq, group * hd), lambda bb, h, i: (bb, i, h)),
                  pl.BlockSpec((1, hd, lp), lambda bb, h, i: (bb, h, 0)),
                  pl.BlockSpec((1, lp, hd), lambda bb, h, i: (bb, 0, h)),
                  pl.BlockSpec((1, tq, lp), lambda bb, h, i: (bb, i, 0)),
                  pl.BlockSpec((1, tq, LANE), lambda bb, h, i: (bb, i, 0)),
                  pl.BlockSpec((n_off, group, LANE, LANE), lambda bb, h, i: (0, h, 0, 0))],
        out_specs=pl.BlockSpec((1, tq, group * hd), lambda bb, h, i: (bb, i, h)),
        out_shape=jax.ShapeDtypeStruct(q.shape, BF16),
        scratch_shapes=[pltpu.VMEM((group, tq, LANE), F32), pltpu.VMEM((group, tq, LANE), F32),
                        pltpu.VMEM((group, tq, hd), F32), pltpu.VMEM((tq, LANE), F32)],
        compiler_params=_cparams(("parallel", "parallel", "arbitrary")),
        name="dsa_attention",
    )(q, kt, v, score, sel_info, bias_tiles)


def _merge_body(o0_ref, o1_ref, o2_ref, g0_ref, g1_ref, g2_ref, w_ref, out_ref):
    total = None
    for r, (o_ref, g_ref) in enumerate(((o0_ref, g0_ref), (o1_ref, g1_ref), (o2_ref, g2_ref))):
        y = jnp.dot(o_ref[0], w_ref[r], preferred_element_type=F32) * g_ref[0]
        total = y if total is None else total + y
    out_ref[0] = total.astype(out_ref.dtype)


def _merge(branches, gate, w_branch, layer):
    b, s, width = branches[0].shape
    d = w_branch.shape[3]
    tm = _pick(s, 512)
    tn = _pick(d, 512)
    nj = d // tn
    o_spec = pl.BlockSpec((1, tm, width), lambda bb, i, j: (bb, i, 0))
    g_specs = [pl.BlockSpec((1, tm, tn), lambda bb, i, j, r=r: (bb, i, r * nj + j)) for r in range(3)]
    return pl.pallas_call(
        _merge_body,
        grid=(b, s // tm, nj),
        in_specs=[o_spec, o_spec, o_spec, *g_specs,
                  pl.BlockSpec((None, 3, width, tn), lambda bb, i, j: (layer, 0, 0, j))],
        out_specs=pl.BlockSpec((1, tm, tn), lambda bb, i, j: (bb, i, j)),
        out_shape=jax.ShapeDtypeStruct((b, s, d), BF16),
        compiler_params=_cparams(("parallel", "parallel", "parallel")),
        name="branch_merge",
    )(*branches, gate, gate, gate, w_branch)


def _router_body(h_ref, w_ref, b_ref, start_ref, info_ref, cnt_ref, run_ref, *, n_groups, per_group):
    step = pl.program_id(0)

    @pl.when(step == 0)
    def _():
        run_ref[...] = start_ref[...]

    tm = h_ref.shape[0]
    logits = jnp.dot(h_ref[...].astype(BF16), w_ref[...], preferred_element_type=F32) + b_ref[...]
    lane = lax.broadcasted_iota(I32, (tm, LANE), 1).astype(F32)
    far = float(4 * LANE)
    is_grp = lane < n_groups
    g_logit = jnp.where(is_grp, logits, -jnp.inf)
    g_max = jnp.max(g_logit, axis=1, keepdims=True)
    grp = jnp.min(jnp.where(g_logit == g_max, lane, far), axis=1, keepdims=True)
    p_grp = 1.0 / jnp.sum(jnp.where(is_grp, jnp.exp(logits - g_max), 0.0), axis=1, keepdims=True)
    e_lo = n_groups + grp * per_group
    in_grp = jnp.logical_and(lane >= e_lo, lane < e_lo + per_group)
    e_logit = jnp.where(in_grp, logits, -jnp.inf)
    m1 = jnp.max(e_logit, axis=1, keepdims=True)
    i1 = jnp.min(jnp.where(e_logit == m1, lane, far), axis=1, keepdims=True)
    e_logit2 = jnp.where(lane == i1, -jnp.inf, e_logit)
    m2 = jnp.max(e_logit2, axis=1, keepdims=True)
    i2 = jnp.min(jnp.where(jnp.logical_and(e_logit2 == m2, in_grp), lane, far), axis=1, keepdims=True)
    e2 = jnp.exp(m2 - m1)
    w1 = p_grp / (1.0 + e2)
    w2 = p_grp * e2 / (1.0 + e2)
    hit1 = lane == i1
    hit2 = lane == i2
    onehot = jnp.where(jnp.logical_or(hit1, hit2), 1.0, 0.0)
    r_i = lax.broadcasted_iota(I32, (tm, tm), 0)
    c_i = lax.broadcasted_iota(I32, (tm, tm), 1)
    earlier = jnp.where(c_i < r_i, 1.0, 0.0).astype(BF16)
    before = jnp.dot(earlier, onehot.astype(BF16), preferred_element_type=F32) + run_ref[...]
    r1 = jnp.sum(jnp.where(hit1, before, 0.0), axis=1, keepdims=True)
    r2 = jnp.sum(jnp.where(hit2, before, 0.0), axis=1, keepdims=True)
    run_ref[...] = run_ref[...] + jnp.sum(onehot, axis=0, keepdims=True)
    cnt_ref[...] = run_ref[...]
    vals = (i1 - n_groups, i2 - n_groups, w1, w2, r1, r2)
    info = jnp.zeros((tm, LANE), F32)
    for pos, val in enumerate(vals):
        info = jnp.where(lane == pos, val, info)
    info_ref[...] = info


def _router(h, w_route, b_route, start_counts, *, n_groups, per_group):
    t, d = h.shape
    tm = _pick(t, 256)
    return pl.pallas_call(
        functools.partial(_router_body, n_groups=n_groups, per_group=per_group),
        grid=(t // tm,),
        in_specs=[pl.BlockSpec((tm, d), lambda i: (i, 0)),
                  pl.BlockSpec((d, LANE), lambda i: (0, 0)),
                  pl.BlockSpec((1, LANE), lambda i: (0, 0)),
                  pl.BlockSpec((1, LANE), lambda i: (0, 0))],
        out_specs=[pl.BlockSpec((tm, LANE), lambda i: (i, 0)),
                   pl.BlockSpec((1, LANE), lambda i: (0, 0))],
        out_shape=[jax.ShapeDtypeStruct((t, LANE), F32), jax.ShapeDtypeStruct((1, LANE), F32)],
        scratch_shapes=[pltpu.VMEM((1, LANE), F32)],
        compiler_params=_cparams(("arbitrary",)),
        name="moe_router",
    )(h, w_route, b_route, start_counts)


def _dispatch_body(dest_ref, h_ref, slots_in_ref, slots_ref, sem, *, tm):
    del slots_in_ref
    base = pl.program_id(0) * tm

    def copy(r, kk):
        return pltpu.make_async_copy(h_ref.at[pl.ds(r, 1)], slots_ref.at[pl.ds(dest_ref[(base + r) * TOP_K + kk], 1)], sem)

    def start(r, _):
        for kk in range(TOP_K):
            copy(r, kk).start()
        return 0

    def wait(r, _):
        for kk in range(TOP_K):
            copy(r, kk).wait()
        return 0

    lax.fori_loop(0, tm, start, 0)
    lax.fori_loop(0, tm, wait, 0)


def _dispatch(h, dest, slots):
    t, d = h.shape
    n_slots = slots.shape[0]
    tm = _pick(t, 256)
    grid_spec = pltpu.PrefetchScalarGridSpec(
        num_scalar_prefetch=1,
        grid=(t // tm,),
        in_specs=[pl.BlockSpec((tm, d), lambda i, dest_ref: (i, 0)),
                  pl.BlockSpec(memory_space=pl.ANY)],
        out_specs=pl.BlockSpec(memory_space=pl.ANY),
        scratch_shapes=[pltpu.SemaphoreType.DMA(())],
    )
    return pl.pallas_call(
        functools.partial(_dispatch_body, tm=tm),
        grid_spec=grid_spec,
        out_shape=jax.ShapeDtypeStruct((n_slots, d), h.dtype),
        input_output_aliases={2: 0},
        compiler_params=_cparams(("arbitrary",)),
        name="moe_dispatch",
    )(dest, h, slots)


def _experts_body(be_ref, nb_ref, x_ref, wg_ref, wu_ref, wd_ref, o_ref):
    blk = pl.program_id(0)
    f = pl.program_id(1)

    @pl.when(blk < nb_ref[0])
    def _():
        x = x_ref[...].astype(BF16)
        gate = jnp.dot(x, wg_ref[0], preferred_element_type=F32)
        up = jnp.dot(x, wu_ref[0], preferred_element_type=F32)
        act = (gate * jax.nn.sigmoid(gate) * up).astype(BF16)
        part = jnp.dot(act, wd_ref[0], preferred_element_type=F32)

        @pl.when(f == 0)
        def _():
            o_ref[...] = part

        @pl.when(f > 0)
        def _():
            o_ref[...] = o_ref[...] + part

    @pl.when(jnp.logical_and(blk >= nb_ref[0], f == 0))
    def _():
        o_ref[...] = jnp.zeros_like(o_ref)


def _experts(slots, block_expert, n_used, w_gate, w_up, w_down, layer, *, bm):
    n_slots, d = slots.shape
    ff = w_gate.shape[3]
    tf = _pick(ff, 256)
    nblk = n_slots // bm

    def x_map(i, f, be, nb):
        return (jnp.minimum(i, nb[0] - 1), 0)

    def w_in_map(i, f, be, nb):
        live = i < nb[0]
        return (layer, be[jnp.minimum(i, nb[0] - 1)], 0, jnp.where(live, f, ff // tf - 1))

    def w_out_map(i, f, be, nb):
        live = i < nb[0]
        return (layer, be[jnp.minimum(i, nb[0] - 1)], jnp.where(live, f, ff // tf - 1), 0)

    grid_spec = pltpu.PrefetchScalarGridSpec(
        num_scalar_prefetch=2,
        grid=(nblk, ff // tf),
        in_specs=[pl.BlockSpec((bm, d), x_map),
                  pl.BlockSpec((None, 1, d, tf), w_in_map),
                  pl.BlockSpec((None, 1, d, tf), w_in_map),
                  pl.BlockSpec((None, 1, tf, d), w_out_map)],
        out_specs=pl.BlockSpec((bm, d), lambda i, f, be, nb: (i, 0)),
    )
    return pl.pallas_call(
        _experts_body,
        grid_spec=grid_spec,
        out_shape=jax.ShapeDtypeStruct((n_slots, d), F32),
        compiler_params=_cparams(("arbitrary", "arbitrary")),
        name="moe_experts",
    )(block_expert, n_used, slots, w_gate, w_up, w_down)


def _combine_body(dest_ref, y_ref, x_ref, g_ref, wt_ref, o_ref, buf_ref, sem, *, tm, per_batch):
    base = (pl.program_id(0) * per_batch + pl.program_id(1)) * tm

    def copy(r, kk):
        return pltpu.make_async_copy(y_ref.at[pl.ds(dest_ref[(base + r) * TOP_K + kk], 1)],
                                     buf_ref.at[kk, pl.ds(r, 1)], sem)

    def start(r, _):
        for kk in range(TOP_K):
            copy(r, kk).start()
        return 0

    def wait(r, _):
        for kk in range(TOP_K):
            copy(r, kk).wait()
        return 0

    lax.fori_loop(0, tm, start, 0)
    lax.fori_loop(0, tm, wait, 0)
    wt = wt_ref[0]
    f = buf_ref[0] * wt[:, 0:1] + buf_ref[1] * wt[:, 1:2]
    o_ref[0] = x_ref[0] + g_ref[0] * f


def _combine(y_slots, dest, weights, x, gate):
    b, s, d = x.shape
    tm = _pick(s, 256)
    per_batch = s // tm
    grid_spec = pltpu.PrefetchScalarGridSpec(
        num_scalar_prefetch=1,
        grid=(b, per_batch),
        in_specs=[pl.BlockSpec(memory_space=pl.ANY),
                  pl.BlockSpec((1, tm, d), lambda bb, i, dest_ref: (bb, i, 0)),
                  pl.BlockSpec((1, 1, d), lambda bb, i, dest_ref: (bb, 0, 0)),
                  pl.BlockSpec((1, tm, TOP_K), lambda bb, i, dest_ref: (bb, i, 0))],
        out_specs=pl.BlockSpec((1, tm, d), lambda bb, i, dest_ref: (bb, i, 0)),
        scratch_shapes=[pltpu.VMEM((TOP_K, tm, d), F32), pltpu.SemaphoreType.DMA(())],
    )
    return pl.pallas_call(
        functools.partial(_combine_body, tm=tm, per_batch=per_batch),
        grid_spec=grid_spec,
        out_shape=jax.ShapeDtypeStruct(x.shape, F32),
        compiler_params=_cparams(("arbitrary", "arbitrary")),
        name="moe_combine",
    )(dest, y_slots, x, gate, weights)


def _history_body(c_ref, n_ref, o_ref, *, heads, hd, n_cache, transpose):
    j = pl.program_id(1)

    def emit(head_rows):
        for h in range(heads):
            x = head_rows(h)
            if transpose:
                o_ref[0, h * hd:(h + 1) * hd, :] = x.T.astype(BF16)
            else:
                o_ref[0, :, h * hd:(h + 1) * hd] = x.astype(BF16)

    @pl.when(j < n_cache)
    def _():
        emit(lambda h: c_ref[0, :, h, :])

    @pl.when(j >= n_cache)
    def _():
        emit(lambda h: n_ref[0, :, h * hd:(h + 1) * hd])


def _history_operand(cache, layer, new_rows, lp, *, transpose, rows=512):
    _, b, p, heads, hd = cache.shape
    w = heads * hd
    n_cache = p // rows
    n_new = (lp - p) // rows
    s = new_rows.shape[1]
    tail = jnp.concatenate([new_rows, jnp.zeros((b, lp - p - s, w), new_rows.dtype)], axis=1)
    if transpose:
        out_spec = pl.BlockSpec((1, w, rows), lambda bb, j: (bb, 0, j))
        out_shape = jax.ShapeDtypeStruct((b, w, lp), BF16)
    else:
        out_spec = pl.BlockSpec((1, rows, w), lambda bb, j: (bb, j, 0))
        out_shape = jax.ShapeDtypeStruct((b, lp, w), BF16)
    return pl.pallas_call(
        functools.partial(_history_body, heads=heads, hd=hd, n_cache=n_cache, transpose=transpose),
        grid=(b, n_cache + n_new),
        in_specs=[pl.BlockSpec((None, 1, rows, heads, hd),
                               lambda bb, j: (layer, bb, jnp.minimum(j, n_cache - 1), 0, 0)),
                  pl.BlockSpec((1, rows, w), lambda bb, j: (bb, jnp.maximum(j - n_cache, 0), 0))],
        out_specs=out_spec,
        out_shape=out_shape,
        compiler_params=_cparams(("parallel", "parallel")),
        name="history_operand",
    )(cache, tail)


def _mixer_tiles(s, n_keys, past_len=0, hist_rows=None):
    k_base = next((t for t in (1024, 512, 256) if n_keys % t == 0), None)
    if k_base is None:
        if hist_rows:
            lp = past_len + -(-(n_keys - past_len) // hist_rows) * hist_rows
            k_small = hist_rows
        else:
            lp = -(-n_keys // LANE) * LANE
            k_small = LANE
        whole = lp if lp <= 4096 else k_small
        return {"lp": lp, "sb": (_pick(s, 256), k_small), "fox": (_pick(s, 512), whole),
                "dsa": (_pick(s, 256), whole), "idx": (_pick(s, LANE), k_small)}
    return {
        "lp": n_keys,
        "sb": (_pick(s, 512), min(k_base, 512)),
        "fox": (_pick(s, 512), k_base),
        "dsa": (_pick(s, 512), k_base),
        "idx": (_pick(s, LANE), min(k_base, 512)),
    }


def _pad_keys(a, lp):
    pad = lp - a.shape[1]
    if pad == 0:
        return a
    return jnp.concatenate([a, jnp.zeros((a.shape[0], pad) + a.shape[2:], a.dtype)], axis=1)


def _token_mixers(h, past, wts, layer, dims, rows_so_far):
    b, s, d = h.shape
    hd, sbh, dsh, kvh, ixh, ixd, fxh = (dims[k] for k in ("hd", "sb_heads", "dsa_heads", "kv_heads",
                                                          "idx_heads", "idx_dim", "fox_heads"))
    seg = wts["w_in_segments"]
    past_len = 0 if past is None else past[0].shape[2]
    n_keys = past_len + s
    off = n_keys - s
    hist_rows = next((r for r in (512, 256, LANE) if past_len and past_len % r == 0), None)
    tiles = _mixer_tiles(s, n_keys, past_len, hist_rows)
    lp = tiles["lp"]
    direct = past is None and lp == s

    h_flat = h if direct else h.reshape(1, b * s, d)

    def proj(name, **kw):
        arr, start, width = seg[name]
        out = _matmul(h_flat, arr, b_rows=(layer, start, width), name="in_" + name, **kw)
        return out if direct else out.reshape(b, s, width)

    def stacked(new, idx):
        if rows_so_far is None:
            return new[None]
        return jnp.concatenate([rows_so_far[idx], new[None]], axis=0)

    def row_proj(name, idx, aux, **kw):
        if not direct:
            out = proj(name, **kw)
            return out, stacked(out, idx), None
        if rows_so_far is None:
            out, operand = proj(name, aux=aux, **kw)
            return out, out[None], operand
        stack, operand = proj(name, aux=aux, stack_on=rows_so_far[idx], **kw)
        return None, stack, operand

    norm = lambda key: [(wts[key][layer].reshape(1, hd), "const")]
    sb_q = proj("sb_q", out_dtype=BF16, tn=MM_TN_WIDE)
    sb_k, st_sb_k, kt_sb = row_proj("sb_k", 0, "bf16_t")
    sb_v, st_sb_v, vb_sb = row_proj("sb_v", 1, "bf16")
    ds_q = proj("ds_q", epi=_epi_head_rms, extras=norm("q_norm_dsa"), out_dtype=BF16, tn=MM_TN_WIDE)
    ds_k, st_ds_k, kt_ds = row_proj("ds_k", 2, "bf16_t", epi=_epi_head_rms, extras=norm("k_norm_dsa"))
    ds_v, st_ds_v, vb_ds = row_proj("ds_v", 3, "bf16")
    ix_q = proj("ix_q", out_dtype=BF16)
    small = proj("small", epi=_epi_masked_logsig,
                 extras=[(wts["small_bias"][layer], "row"), (wts["small_mask"], "row")])
    fx_q = proj("fx_q", epi=_epi_head_rms, extras=norm("q_norm_fox"), out_dtype=BF16, tn=MM_TN_WIDE)
    fx_k, st_fx_k, kt_fx = row_proj("fx_k", 5, "bf16_t", epi=_epi_head_rms, extras=norm("k_norm_fox"))
    fx_v, st_fx_v, vb_fx = row_proj("fx_v", 6, "bf16")
    gate = proj("gate", epi=_epi_sigmoid, tn=MM_TN_WIDE)
    at = wts["small_at"]
    ix_k = small[:, :, at["ix_k"]:at["ix_k"] + ixd]
    ix_w = small[:, :, at["ix_w"]:at["ix_w"] + ixh]
    log_f = small[:, :, at["fx_f"]:at["fx_f"] + fxh]
    new_rows = (st_sb_k, st_sb_v, st_ds_k, st_ds_v, stacked(ix_k, 4), st_fx_k, st_fx_v, stacked(log_f, 7))

    def cat(pc, r):
        pc = pc[layer]
        return _pad_keys(jnp.concatenate([pc.reshape(pc.shape[0], pc.shape[1], -1), r], axis=1), lp)

    if past is None:
        k_ix, lf_all = _pad_keys(ix_k, lp), _pad_keys(log_f, lp)
    else:
        k_ix, lf_all = cat(past[4], ix_k), cat(past[7], log_f)
    if not direct:
        rows = (sb_k, sb_v, ds_k, ds_v, None, fx_k, fx_v, None)

        def operand(idx, transpose):
            if past is not None and hist_rows:
                return _history_operand(past[idx], layer, rows[idx], lp, transpose=transpose, rows=hist_rows)
            full = (_pad_keys(rows[idx], lp) if past is None else cat(past[idx], rows[idx])).astype(BF16)
            return jnp.transpose(full, (0, 2, 1)) if transpose else full

        kt_sb, kt_ds, kt_fx = (operand(i, True) for i in (0, 2, 5))
        vb_sb, vb_ds, vb_fx = (operand(i, False) for i in (1, 3, 6))

    tq, tk = tiles["sb"]
    o_sb = _sb_attention(sb_q, kt_sb, vb_sb, heads=sbh, hd=hd, off=off, tq=tq, tk=tk)

    d_all = _cumsum_lanes(jnp.transpose(lf_all, (0, 2, 1)))
    d_q = d_all[:, :, off:off + s, None]
    tq, tk = tiles["fox"]
    o_fx = _fox_attention(fx_q, kt_fx, vb_fx, d_q, d_all.reshape(b, fxh, 1, lp),
                          heads=fxh, hd=hd, off=off, tq=tq, tk=tk)

    topk = min(DSA_TOPK_MAX, n_keys // 4)
    zeros = jnp.zeros_like(k_ix)
    k_even = jnp.concatenate([k_ix, zeros], axis=2)
    k_odd = jnp.concatenate([zeros, k_ix], axis=2)
    tq, tk = tiles["idx"]
    score, thr = _indexer(ix_q, ix_w, k_even, k_odd, off=off, topk=topk, idx_heads=ixh, tq=tq, tk=tk)
    tq, tk = tiles["dsa"]
    o_ds = _dsa_attention(ds_q, kt_ds, vb_ds, score, thr, wts["bias_tiles"], kv_heads=kvh,
                          group=dsh // kvh, hd=hd, off=off, tq=tq, tk=tk)

    if direct:
        merged = _merge((o_sb, o_ds, o_fx), gate, wts["w_branch"], layer)
    else:
        flat = lambda a: a.reshape(1, b * s, a.shape[-1])
        merged = _merge(tuple(flat(o) for o in (o_sb, o_ds, o_fx)), flat(gate), wts["w_branch"], layer)
        merged = merged.reshape(b, s, d)
    return merged, new_rows


def _hier_moe(parts, wts, layer, dims):
    d = parts[0][0].shape[-1]
    ng, ne = dims["n_groups"], dims["n_experts"]
    counts = jnp.zeros((1, LANE), F32)
    routed = []
    for _, h, _ in parts:
        info, counts = _router(h.reshape(-1, d), wts["w_route"][layer], wts["b_route"][layer], counts,
                               n_groups=ng, per_group=ne // ng)
        routed.append(info)
    n_assign = sum(info.shape[0] for info in routed) * TOP_K
    counts = counts[0, ng:ng + ne].astype(I32)
    bm = 512 if n_assign >= 512 * ne else 256
    padded = (counts + bm - 1) // bm * bm
    pad_end = jnp.cumsum(padded)
    pad_start = pad_end - padded
    n_blocks = -(-n_assign // bm) + ne
    block_expert = jnp.minimum(
        jnp.searchsorted(pad_end, jnp.arange(n_blocks, dtype=I32) * bm, side="right"), ne - 1).astype(I32)
    n_used = (pad_end[-1:] // bm).astype(I32)
    slots = jnp.zeros((n_blocks * bm, d), F32)
    dests = []
    for (_, h, _), info in zip(parts, routed):
        expert = info[:, 0:TOP_K].astype(I32)
        rank = info[:, 2 * TOP_K:3 * TOP_K].astype(I32)
        dest = (pad_start[expert] + rank).reshape(-1)
        slots = _dispatch(h.reshape(-1, d), dest, slots)
        dests.append(dest)
    y_slots = _experts(slots, block_expert, n_used, wts["w_exp_gate"], wts["w_exp_up"], wts["w_exp_down"], layer,
                       bm=bm)
    outs = []
    for (x, _, g2), info, dest in zip(parts, routed, dests):
        b, s, _ = x.shape
        outs.append(_combine(y_slots, dest, info[:, TOP_K:2 * TOP_K].reshape(b, s, TOP_K), x, g2))
    return outs


def _mixer_sublayer(x, mod, past, wts, layer, dims, rows):
    b, s, d = x.shape
    sh1, sc1, g1, sh2, sc2, g2 = mod
    h = _norm_mod(x, wts["g_norm1"][layer], sc1, sh1)
    merged, rows = _token_mixers(h, past, wts, layer, dims, rows)
    if s >= MM_TM:
        x = _matmul(merged, wts["w_out"], b_cols=(layer, 0, d), epi=_epi_residual,
                    extras=[(x, "tile"), (g1, "batchrow")], name="out_proj")
    else:
        flat = lambda a: jnp.broadcast_to(a, (b, s, d)).reshape(1, b * s, d)
        x = _matmul(flat(merged), wts["w_out"], b_cols=(layer, 0, d), epi=_epi_residual,
                    extras=[(flat(x), "tile"), (flat(g1), "tile")], name="out_proj").reshape(b, s, d)
    h = _norm_mod(x, wts["g_norm2"][layer], sc2, sh2, out_dtype=F32)
    return (x, h, g2), rows


def _trunks(xs, mods, pasts, wts, depth, dims):
    rows = [None] * len(xs)
    for l in range(depth):
        parts = []
        for r in range(len(xs)):
            part, rows[r] = _mixer_sublayer(xs[r], mods[r][l], pasts[r], wts, l, dims, rows[r])
            parts.append(part)
        xs = _hier_moe(parts, wts, l, dims)
    return xs, rows


def _prepare_weights(dims, w_in, q_norm_dsa, k_norm_dsa, q_norm_fox, k_norm_fox, b_forget, w_branch, w_out,
                     bias_tiles, w_route_grp, b_route_grp, w_route_exp, b_route_exp, w_exp_gate, w_exp_up,
                     w_exp_down, g_norm1, g_norm2):
    hd, sbh, dsh, kvh, ixh, ixd, fxh, d = (dims[k] for k in ("hd", "sb_heads", "dsa_heads", "kv_heads",
                                                             "idx_heads", "idx_dim", "fox_heads", "d"))
    depth = w_in.shape[0]
    widths = [("sb_q", sbh * hd), ("sb_k", sbh * hd), ("sb_v", sbh * hd), ("ds_q", dsh * hd), ("ds_k", kvh * hd),
              ("ds_v", kvh * hd), ("ix_q", ixh * ixd), ("ix_k", ixd), ("ix_w", ixh), ("fx_q", fxh * hd),
              ("fx_k", fxh * hd), ("fx_v", fxh * hd), ("fx_f", fxh), ("gate", 3 * d)]
    cols = {}
    start = 0
    for name, w in widths:
        cols[name] = (start, start + w)
        start += w
    runs = []
    for name, (a, bnd) in cols.items():
        width = bnd - a
        if width % LANE:
            continue
        if runs and runs[-1][1] == a and (a - runs[-1][0]) % _pick(width, MM_TN) == 0:
            runs[-1][1] = bnd
            runs[-1][2].append((name, a - runs[-1][0], width))
        else:
            runs.append([a, bnd, [(name, 0, width)]])
    w_in_t = jnp.swapaxes(w_in, 1, 2)
    seg = {}
    for first, last, members in runs:
        arr = w_in_t[:, first:last, :].astype(BF16)
        for name, rel, width in members:
            seg[name] = (arr, rel, width)
    n_small = ixd + ixh + fxh
    small_w = -(-n_small // LANE) * LANE
    small = jnp.concatenate([w_in_t[:, cols[name][0]:cols[name][1], :] for name in ("ix_k", "ix_w", "fx_f")]
                            + [jnp.zeros((depth, small_w - n_small, d), F32)], axis=1)
    seg["small"] = (small.astype(BF16), 0, small_w)
    small_at = {"ix_k": 0, "ix_w": ixd, "fx_f": ixd + ixh}
    lane = jnp.arange(small_w)
    is_f = jnp.logical_and(lane >= ixd + ixh, lane < n_small)
    small_bias = jnp.zeros((depth, 1, small_w), F32).at[:, 0, ixd + ixh:n_small].set(b_forget.astype(F32))
    ng, ne = dims["n_groups"], dims["n_experts"]
    w_route = jnp.concatenate([w_route_grp, w_route_exp, jnp.zeros((depth, d, LANE - ng - ne), F32)], axis=2)
    b_route = jnp.concatenate([b_route_grp, b_route_exp, jnp.zeros((depth, LANE - ng - ne), F32)], axis=1)
    return {
        "w_in_segments": seg,
        "small_at": small_at,
        "small_bias": small_bias,
        "small_mask": is_f.astype(F32).reshape(1, small_w),
        "q_norm_dsa": q_norm_dsa, "k_norm_dsa": k_norm_dsa, "q_norm_fox": q_norm_fox, "k_norm_fox": k_norm_fox,
        "w_branch": w_branch.astype(BF16), "w_out": w_out.astype(BF16),
        "bias_tiles": bias_tiles,
        "w_route": w_route.astype(BF16), "b_route": b_route.astype(F32).reshape(depth, 1, LANE),
        "w_exp_gate": w_exp_gate.astype(BF16), "w_exp_up": w_exp_up.astype(BF16),
        "w_exp_down": w_exp_down.astype(BF16),
        "g_norm1": g_norm1, "g_norm2": g_norm2,
    }


def kernel(x_prompt, x_sample, c_prompt, c_sample, cache_sb_k, cache_sb_v, cache_dsa_k, cache_dsa_v, cache_dsa_kidx, cache_fox_k, cache_fox_v, cache_fox_logf, w_mod, b_mod, g_norm1, g_norm2, w_in, q_norm_dsa, k_norm_dsa, q_norm_fox, k_norm_fox, b_forget, w_branch, w_out, rel_bias, w_route_grp, b_route_grp, w_route_exp, b_route_exp, w_exp_gate, w_exp_up, w_exp_down):
    depth = w_in.shape[0]
    d = x_prompt.shape[-1]
    hd = q_norm_dsa.shape[-1]
    dims = {
        "d": d, "hd": hd,
        "sb_heads": cache_sb_k.shape[3], "dsa_heads": rel_bias.shape[1], "kv_heads": cache_dsa_k.shape[3],
        "idx_dim": cache_dsa_kidx.shape[-1], "fox_heads": cache_fox_k.shape[3],
        "n_groups": w_route_grp.shape[-1], "n_experts": w_route_exp.shape[-1],
    }
    fixed = (3 * dims["sb_heads"] * hd + dims["dsa_heads"] * hd + 2 * dims["kv_heads"] * hd + dims["idx_dim"]
             + 3 * dims["fox_heads"] * hd + dims["fox_heads"] + 3 * d)
    dims["idx_heads"] = (w_in.shape[2] - fixed) // (dims["idx_dim"] + 1)

    bias_tiles = _bias_tiles(rel_bias)
    wts = _prepare_weights(dims, w_in, q_norm_dsa, k_norm_dsa, q_norm_fox, k_norm_fox, b_forget, w_branch, w_out,
                           bias_tiles, w_route_grp, b_route_grp, w_route_exp, b_route_exp, w_exp_gate, w_exp_up,
                           w_exp_down, g_norm1, g_norm2)

    nb_p, nb_s = c_prompt.shape[0], c_sample.shape[0]
    rows = -(-(nb_p + nb_s) // 8) * 8
    c_all = jnp.concatenate([c_prompt, c_sample, jnp.zeros((rows - nb_p - nb_s, d), F32)], axis=0)[None]
    mods_p, mods_s = [], []
    for l in range(depth):
        mod = _matmul(c_all, w_mod, b_cols=(l, 0, w_mod.shape[2]), epi=_epi_bias,
                      extras=[(b_mod[l].reshape(1, -1), "row")], silu_a=True, name="adaln_mod")[0]
        mods_p.append([m[:, None, :] for m in jnp.split(mod[:nb_p], 6, axis=-1)])
        mods_s.append([m[:, None, :] for m in jnp.split(mod[nb_p:nb_p + nb_s], 6, axis=-1)])

    past = (cache_sb_k, cache_sb_v, cache_dsa_k, cache_dsa_v, cache_dsa_kidx, cache_fox_k, cache_fox_v,
            cache_fox_logf)
    (y_p, y_s), (rows_p, rows_s) = _trunks([x_prompt, x_sample], [mods_p, mods_s], [None, past], wts, depth, dims)

    head_counts = (dims["sb_heads"], dims["sb_heads"], dims["kv_heads"], dims["kv_heads"], None,
                   dims["fox_heads"], dims["fox_heads"], None)

    def split_heads(rows):
        return [a if heads is None else a.reshape(a.shape[0], a.shape[1], a.shape[2], heads, hd)
                for a, heads in zip(rows, head_counts)]

    return (y_p, y_s, *split_heads(rows_p), *split_heads(rows_s))
```
